```python
import jax, jax.numpy as jnp
from jax import lax
import numpy as np

D_MODEL = 1024
BATCH = 8
SEQ = 4096
DEPTH = 4

MLA_HEADS = 4
MLA_Q_RANK = 256
MLA_KV_RANK = 128
MLA_NOPE = 64
MLA_ROPE = 32
MLA_V = 64
ROPE_THETA = 10000.0
CONV_WIDTH = 256
CONV_K = 3
DSA_HEADS = 4
DSA_HEAD_DIM = 64
IDX_HEADS = 8
IDX_DIM = 32
DSA_TOPK_MAX = 256
GLA_HEADS = 4
GLA_DK = 32
GLA_DV = 64
GLA_GATE_RANK = 16
GLA_TAU = 16.0
GLA_CHUNK = 64
N_BRANCH = 4
BRANCH_WIDTH = 256
D_FF = ((8 * D_MODEL + 3 * 256 - 1) // (3 * 256)) * 256
Q_BLOCK = 128
EPS = 1e-6
NEG = -1e30

IN_SIZES = (
    MLA_Q_RANK, MLA_KV_RANK, MLA_ROPE,
    CONV_WIDTH, CONV_WIDTH, CONV_WIDTH,
    DSA_HEADS * DSA_HEAD_DIM, DSA_HEAD_DIM, DSA_HEAD_DIM,
    IDX_HEADS * IDX_DIM, IDX_DIM, IDX_HEADS,
    GLA_HEADS * GLA_DK, GLA_HEADS * GLA_DK, GLA_HEADS * GLA_DV,
    GLA_GATE_RANK, GLA_HEADS * GLA_DV,
    N_BRANCH * D_MODEL,
)
IN_TOTAL = sum(IN_SIZES)

kernel_name = 'hybrid_gated_parallel_mixer_trunk'


def rmsnorm(x, g):
    x32 = x.astype(jnp.float32)
    y = x32 * lax.rsqrt(jnp.mean(x32 * x32, axis=-1, keepdims=True) + EPS)
    return (y * g.astype(jnp.float32)).astype(x.dtype)


def apply_rope(x, positions):
    half = x.shape[-1] // 2
    inv_freq = ROPE_THETA ** (-jnp.arange(half, dtype=jnp.float32) / half)
    ang = positions.astype(jnp.float32)[:, :, None, None] * inv_freq
    cos, sin = jnp.cos(ang), jnp.sin(ang)
    x32 = x.astype(jnp.float32)
    x1, x2 = x32[..., :half], x32[..., half:]
    return jnp.concatenate([x1 * cos - x2 * sin, x1 * sin + x2 * cos], axis=-1).astype(x.dtype)


def to_blocks(a):
    b, s = a.shape[:2]
    return jnp.moveaxis(a.reshape((b, s // Q_BLOCK, Q_BLOCK) + a.shape[2:]), 1, 0)


def from_blocks(a):
    nb, b, qb = a.shape[:3]
    return jnp.moveaxis(a, 0, 1).reshape((b, nb * qb) + a.shape[3:])


def causal_block_attention(q, k, v, scale):
    s_len = k.shape[1]
    key_pos = jnp.arange(s_len)

    def one_block(args):
        qb, start = args
        qpos = start + jnp.arange(Q_BLOCK)
        s = jnp.einsum('bqhd,bkhd->bhqk', qb, k).astype(jnp.float32) * scale
        s = jnp.where(key_pos[None, :] <= qpos[:, None], s, NEG)
        p = jax.nn.softmax(s, axis=-1).astype(v.dtype)
        return jnp.einsum('bhqk,bkhd->bqhd', p, v)

    o = lax.map(one_block, (to_blocks(q), jnp.arange(s_len // Q_BLOCK) * Q_BLOCK))
    return from_blocks(o)


def mla_branch(cq, ckv, krope, positions, q_norm_g, w_q_up, kv_norm_g, w_kv_up):
    b, s, _ = cq.shape
    q = (rmsnorm(cq, q_norm_g) @ w_q_up).reshape(b, s, MLA_HEADS, MLA_NOPE + MLA_ROPE)
    q = jnp.concatenate([q[..., :MLA_NOPE], apply_rope(q[..., MLA_NOPE:], positions)], axis=-1)
    kv = (rmsnorm(ckv, kv_norm_g) @ w_kv_up).reshape(b, s, MLA_HEADS, MLA_NOPE + MLA_V)
    k_rope = apply_rope(krope[:, :, None, :], positions)
    k = jnp.concatenate([kv[..., :MLA_NOPE], jnp.broadcast_to(k_rope, (b, s, MLA_HEADS, MLA_ROPE))], axis=-1)
    v = kv[..., MLA_NOPE:]
    o = causal_block_attention(q, k, v, (MLA_NOPE + MLA_ROPE) ** -0.5)
    return o.reshape(b, s, MLA_HEADS * MLA_V)


def shortconv_branch(cb, cc, cx, conv_w):
    u = cc * cx
    y = lax.conv_general_dilated(
        u, conv_w[:, None, :], window_strides=(1,), padding=[(CONV_K - 1, 0)],
        dimension_numbers=('NWC', 'WIO', 'NWC'), feature_group_count=CONV_WIDTH)
    return cb * y


def dsa_branch(dq, dk, dv, iq, ik, iw):
    b, s, _ = dq.shape
    topk = min(DSA_TOPK_MAX, s // 4)
    q = dq.reshape(b, s, DSA_HEADS, DSA_HEAD_DIM)
    q_idx = iq.reshape(b, s, IDX_HEADS, IDX_DIM)
    key_pos = jnp.arange(s)
    gather = jax.vmap(lambda a, i: a[i])

    def one_block(args):
        qb, qib, wb, start = args
        qpos = start + jnp.arange(Q_BLOCK)
        rel = jax.nn.relu(jnp.einsum('bqhd,bsd->bqhs', qib, ik).astype(jnp.float32) * IDX_DIM ** -0.5)
        score = jnp.einsum('bqh,bqhs->bqs', wb.astype(jnp.float32) * IDX_HEADS ** -0.5, rel)
        score = jnp.where(key_pos[None, None, :] <= qpos[None, :, None], score, NEG)
        _, idx = lax.top_k(score, topk)
        valid = idx <= qpos[None, :, None]
        k_sel = gather(dk, idx)
        v_sel = gather(dv, idx)
        att = jnp.einsum('bqhd,bqkd->bqhk', qb, k_sel).astype(jnp.float32) * DSA_HEAD_DIM ** -0.5
        att = jnp.where(valid[:, :, None, :], att, NEG)
        p = jax.nn.softmax(att, axis=-1).astype(dv.dtype)
        return jnp.einsum('bqhk,bqkd->bqhd', p, v_sel)

    o = lax.map(one_block, (to_blocks(q), to_blocks(q_idx), to_blocks(iw), jnp.arange(s // Q_BLOCK) * Q_BLOCK))
    return from_blocks(o).reshape(b, s, DSA_HEADS * DSA_HEAD_DIM)


def gla_branch(gq, gk, gv, glr, gr, w_gate_up, b_gate, norm_g):
    b, s, _ = gq.shape
    nc = s // GLA_CHUNK
    f32 = jnp.float32

    def chunks(a, d):
        return a.astype(f32).reshape(b, nc, GLA_CHUNK, GLA_HEADS, d).transpose(1, 0, 3, 2, 4)

    q = chunks(gq, GLA_DK) * GLA_DK ** -0.5
    k = chunks(gk, GLA_DK)
    v = chunks(gv, GLA_DV)
    log_alpha = jax.nn.log_sigmoid((glr @ w_gate_up + b_gate).astype(f32)) / GLA_TAU
    g = chunks(log_alpha, GLA_DK)
    causal = jnp.tril(jnp.ones((GLA_CHUNK, GLA_CHUNK), dtype=bool))

    def step(state, inp):
        qc, kc, vc, gc = inp
        cum = lax.cumsum(gc, axis=2)
        o_inter = jnp.einsum('bhik,bhkv->bhiv', qc * jnp.exp(cum), state)
        decay = jnp.exp(jnp.where(causal[:, :, None], cum[:, :, :, None, :] - cum[:, :, None, :, :], NEG))
        attn = jnp.einsum('bhik,bhjk,bhijk->bhij', qc, kc, decay)
        o_intra = jnp.einsum('bhij,bhjv->bhiv', attn, vc)
        last = cum[:, :, -1:, :]
        state = jnp.exp(last[:, :, 0, :])[..., None] * state + jnp.einsum('bhjk,bhjv->bhkv', kc * jnp.exp(last - cum), vc)
        return state, o_inter + o_intra

    state0 = jnp.zeros((b, GLA_HEADS, GLA_DK, GLA_DV), f32)
    _, o = lax.scan(step, state0, (q, k, v, g))
    o = o.transpose(1, 0, 3, 2, 4).reshape(b, s, GLA_HEADS, GLA_DV)
    o = rmsnorm(o, norm_g).astype(gr.dtype).reshape(b, s, GLA_HEADS * GLA_DV)
    return jax.nn.silu(gr) * o


def setup_inputs(seed: int = 0) -> dict:
    key = jax.random.key(seed)
    ks = jax.random.split(key, 20)
    f32 = jnp.float32

    def nrm(k, shape, scale):
        return jax.random.normal(k, shape, f32) * scale

    def gain(k, shape):
        return 1.0 + 0.05 * jax.random.normal(k, shape, f32)

    return {
        'x': nrm(ks[0], (BATCH, SEQ, D_MODEL), 1.0),
        'positions': jnp.broadcast_to(jnp.arange(SEQ, dtype=jnp.int32)[None, :], (BATCH, SEQ)),
        'attn_norm_g': gain(ks[1], (DEPTH, D_MODEL)),
        'w_in': nrm(ks[2], (DEPTH, D_MODEL, IN_TOTAL), D_MODEL ** -0.5),
        'mla_q_norm_g': gain(ks[3], (DEPTH, MLA_Q_RANK)),
        'mla_w_q_up': nrm(ks[4], (DEPTH, MLA_Q_RANK, MLA_HEADS * (MLA_NOPE + MLA_ROPE)), MLA_Q_RANK ** -0.5),
        'mla_kv_norm_g': gain(ks[5], (DEPTH, MLA_KV_RANK)),
        'mla_w_kv_up': nrm(ks[6], (DEPTH, MLA_KV_RANK, MLA_HEADS * (MLA_NOPE + MLA_V)), MLA_KV_RANK ** -0.5),
        'conv_w': nrm(ks[7], (DEPTH, CONV_K, CONV_WIDTH), CONV_K ** -0.5),
        'gla_w_gate_up': nrm(ks[8], (DEPTH, GLA_GATE_RANK, GLA_HEADS * GLA_DK), GLA_GATE_RANK ** -0.5),
        'gla_b_gate': nrm(ks[9], (DEPTH, GLA_HEADS * GLA_DK), 0.1),
        'gla_norm_g': gain(ks[10], (DEPTH, GLA_DV)),
        'w_branch': nrm(ks[11], (DEPTH, N_BRANCH, BRANCH_WIDTH, D_MODEL), BRANCH_WIDTH ** -0.5),
        'w_out': nrm(ks[12], (DEPTH, D_MODEL, D_MODEL), D_MODEL ** -0.5),
        'ffn_norm_g': gain(ks[13], (DEPTH, D_MODEL)),
        'w_ffn_gate': nrm(ks[14], (DEPTH, D_MODEL, D_FF), D_MODEL ** -0.5),
        'w_ffn_up': nrm(ks[15], (DEPTH, D_MODEL, D_FF), D_MODEL ** -0.5),
        'w_ffn_down': nrm(ks[16], (DEPTH, D_FF, D_MODEL), D_FF ** -0.5),
        'final_norm_g': gain(ks[17], (D_MODEL,)),
    }


def reference(x, positions, attn_norm_g, w_in, mla_q_norm_g, mla_w_q_up, mla_kv_norm_g, mla_w_kv_up,
              conv_w, gla_w_gate_up, gla_b_gate, gla_norm_g, w_branch, w_out, ffn_norm_g,
              w_ffn_gate, w_ffn_up, w_ffn_down, final_norm_g):
    b, s, d = x.shape
    split_points = [int(p) for p in np.cumsum(IN_SIZES)[:-1]]
    for l in range(DEPTH):
        h = rmsnorm(x, attn_norm_g[l])
        (cq, ckv, krope, cb, cc, cx, dq, dk, dv, iq, ik, iw,
         gq, gk, gv, glr, gr, gate_pre) = jnp.split(h @ w_in[l], split_points, axis=-1)

        o_mla = mla_branch(cq, ckv, krope, positions, mla_q_norm_g[l], mla_w_q_up[l],
                           mla_kv_norm_g[l], mla_w_kv_up[l])
        o_conv = shortconv_branch(cb, cc, cx, conv_w[l])
        o_dsa = dsa_branch(dq, dk, dv, iq, ik, iw)
        o_gla = gla_branch(gq, gk, gv, glr, gr, gla_w_gate_up[l], gla_b_gate[l], gla_norm_g[l])

        gates = jax.nn.sigmoid(gate_pre).reshape(b, s, N_BRANCH, d)
        branches = (o_mla, o_conv, o_dsa, o_gla)
        merged = gates[:, :, 0, :] * (branches[0] @ w_branch[l, 0])
        for n in range(1, N_BRANCH):
            merged = merged + gates[:, :, n, :] * (branches[n] @ w_branch[l, n])
        x = x + merged @ w_out[l]

        h2 = rmsnorm(x, ffn_norm_g[l])
        x = x + (jax.nn.silu(h2 @ w_ffn_gate[l]) * (h2 @ w_ffn_up[l])) @ w_ffn_down[l]
    return rmsnorm(x, final_norm_g)
```

```python
import functools

import jax
import jax.numpy as jnp
import numpy as np
from jax import lax
from jax.experimental import pallas as pl
from jax.experimental.pallas import tpu as pltpu

F32 = jnp.float32
BF16 = jnp.bfloat16
I32 = jnp.int32

D_MODEL = 1024
MLA_HEADS = 4
MLA_Q_RANK = 256
MLA_KV_RANK = 128
MLA_NOPE = 64
MLA_ROPE = 32
MLA_V = 64
ROPE_THETA = 10000.0
CONV_WIDTH = 256
CONV_K = 3
DSA_HEADS = 4
DSA_HEAD_DIM = 64
IDX_HEADS = 8
IDX_DIM = 32
DSA_TOPK_MAX = 256
GLA_HEADS = 4
GLA_DK = 32
GLA_DV = 64
GLA_GATE_RANK = 16
GLA_TAU = 16.0
GLA_CHUNK = 64
N_BRANCH = 4
BRANCH_WIDTH = 256
D_FF = ((8 * D_MODEL + 3 * 256 - 1) // (3 * 256)) * 256
EPS = 1e-6
NEG = -1e30

LANE = 128
HEAD_PAD = 128

IN_SIZES = (
    MLA_Q_RANK, MLA_KV_RANK, MLA_ROPE,
    CONV_WIDTH, CONV_WIDTH, CONV_WIDTH,
    DSA_HEADS * DSA_HEAD_DIM, DSA_HEAD_DIM, DSA_HEAD_DIM,
    IDX_HEADS * IDX_DIM, IDX_DIM, IDX_HEADS,
    GLA_HEADS * GLA_DK, GLA_HEADS * GLA_DK, GLA_HEADS * GLA_DV,
    GLA_GATE_RANK, GLA_HEADS * GLA_DV,
    N_BRANCH * D_MODEL,
)
_OFF = np.concatenate([[0], np.cumsum(IN_SIZES)]).astype(int)
(O_CQ, O_CKV, O_KR, O_CB, O_CC, O_CX, O_DQ, O_DK, O_DV, O_IQ, O_IK, O_IW,
 O_GQ, O_GK, O_GV, O_GLR, O_GR, O_GATE, O_END) = [int(v) for v in _OFF]

P_CQ, P_CKV, P_KRP, P_KRS = 0, 256, 384, 512
P_CB, P_CC, P_CX = 640, 896, 1152
P_DQ, P_DKK, P_DVV = 1408, 1664, 1792
P_IQ, P_IKR, P_IW = 1920, 2176, 2304
P_GQ, P_GK, P_GV, P_GLR, P_GR = 2432, 2560, 2688, 2944, 3072
P_TOTAL = 3328

VMEM_LIMIT = 56 * 1024 * 1024


def _cparams(sem):
    return pltpu.CompilerParams(dimension_semantics=sem, vmem_limit_bytes=VMEM_LIMIT)


def _rms(x, g):
    return x * lax.rsqrt(jnp.mean(x * x, axis=-1, keepdims=True) + EPS) * g


def _dot(a, b):
    return jnp.dot(a, b, preferred_element_type=F32)


def _dot_nt(a, b):
    return lax.dot_general(a, b, (((1,), (1,)), ((), ())), preferred_element_type=F32)


def _rope_kernel(pos_ref, pat_ref, c_ref, s_ref):
    pos = pos_ref[...].astype(F32)
    ang = pos * pat_ref[0:1, :]
    c_ref[...] = pat_ref[1:2, :] + pat_ref[2:3, :] * jnp.cos(ang)
    s_ref[...] = pat_ref[3:4, :] * jnp.sin(ang)


def _rope_tables(positions, tm):
    t = positions.size
    half = MLA_ROPE // 2
    inv_freq = ROPE_THETA ** (-jnp.arange(half, dtype=F32) / half)
    z64, z32 = jnp.zeros((MLA_NOPE,), F32), jnp.zeros((HEAD_PAD - MLA_NOPE - MLA_ROPE,), F32)
    o16 = jnp.ones((half,), F32)
    pat = jnp.stack([
        jnp.concatenate([z64, inv_freq, inv_freq, z32]),
        jnp.concatenate([z64 + 1.0, 0 * o16, 0 * o16, z32]),
        jnp.concatenate([z64, o16, o16, z32]),
        jnp.concatenate([z64, -o16, o16, z32]),
    ])
    pat = jnp.concatenate([pat, jnp.zeros((4, HEAD_PAD), F32)], axis=0)
    return pl.pallas_call(
        _rope_kernel,
        grid=(t // tm,),
        in_specs=[pl.BlockSpec((tm, 1), lambda i: (i, 0)), pl.BlockSpec((8, HEAD_PAD), lambda i: (0, 0))],
        out_specs=[pl.BlockSpec((tm, HEAD_PAD), lambda i: (i, 0))] * 2,
        out_shape=[jax.ShapeDtypeStruct((t, HEAD_PAD), F32)] * 2,
        compiler_params=_cparams(("arbitrary",)),
        name="rope_tables",
    )(positions.reshape(t, 1), pat)


def _prep_kernel(x_ref, g_ref, w_ref, qg_ref, wq_ref, wqs_ref, kvg_ref, wk_ref, wv_ref, c_ref, s_ref,
                 convw_ref, wgate_ref, bgate_ref,
                 mq_ref, mk_ref, mv_ref, oconv_ref, dq_ref, dkk_ref, dvv_ref, iq_ref, ikr_ref, iw_ref,
                 gq_ref, gk_ref, gv_ref, gg_ref, gr_ref, u_scr, *, tm):
    j = pl.program_id(1)
    h = _rms(x_ref[...], g_ref[...]).astype(BF16)
    p = _dot(h, w_ref[...])

    c1, s1 = c_ref[...], s_ref[...]
    c4 = jnp.concatenate([c1] * MLA_HEADS, axis=1)
    s4 = jnp.concatenate([s1] * MLA_HEADS, axis=1)

    qn = _rms(p[:, P_CQ:P_CQ + MLA_Q_RANK], qg_ref[...]).astype(BF16)
    q = _dot(qn, wq_ref[...]) * c4 + _dot(qn, wqs_ref[...]) * s4
    mq_ref[...] = (q * ((MLA_NOPE + MLA_ROPE) ** -0.5)).astype(BF16)
    kvn = _rms(p[:, P_CKV:P_CKV + MLA_KV_RANK], kvg_ref[...]).astype(BF16)
    kr = p[:, P_KRP:P_KRP + HEAD_PAD] * c1 + p[:, P_KRS:P_KRS + HEAD_PAD] * s1
    mk_ref[...] = (_dot(kvn, wk_ref[...]) + jnp.concatenate([kr] * MLA_HEADS, axis=1)).astype(BF16)
    mv_ref[...] = _dot(kvn, wv_ref[...]).astype(BF16)

    u = p[:, P_CC:P_CC + CONV_WIDTH] * p[:, P_CX:P_CX + CONV_WIDTH]

    @pl.when(j == 0)
    def _():
        u_scr[0:8, :] = jnp.zeros((8, CONV_WIDTH), F32)

    u_scr[8:8 + tm, :] = u
    cw = convw_ref[...]
    y = cw[0:1, :] * u_scr[6:6 + tm, :] + cw[1:2, :] * u_scr[7:7 + tm, :] + cw[2:3, :] * u
    oconv_ref[...] = (p[:, P_CB:P_CB + CONV_WIDTH] * y).astype(BF16)
    u_scr[0:8, :] = u_scr[tm:tm + 8, :]

    dq_ref[...] = (p[:, P_DQ:P_DQ + 256] * (DSA_HEAD_DIM ** -0.5)).astype(BF16)
    dkk_ref[...] = p[:, P_DKK:P_DKK + LANE].astype(BF16)
    dvv_ref[...] = p[:, P_DVV:P_DVV + LANE].astype(BF16)
    iq_ref[...] = p[:, P_IQ:P_IQ + 256].astype(BF16)
    ikr_ref[...] = p[:, P_IKR:P_IKR + LANE].astype(BF16)
    iw_ref[...] = p[:, P_IW:P_IW + LANE] * ((IDX_DIM ** -0.5) * (IDX_HEADS ** -0.5))

    gq_ref[...] = p[:, P_GQ:P_GQ + LANE] * (GLA_DK ** -0.5)
    gk_ref[...] = p[:, P_GK:P_GK + LANE]
    gv_ref[...] = p[:, P_GV:P_GV + 256]
    gate = _dot(p[:, P_GLR:P_GLR + LANE].astype(BF16), wgate_ref[...]) + bgate_ref[...]
    log_sig = jnp.minimum(gate, 0.0) - jnp.log1p(jnp.exp(-jnp.abs(gate)))
    gg_ref[...] = log_sig * (1.0 / GLA_TAU)
    gr = p[:, P_GR:P_GR + 256]
    gr_ref[...] = gr / (1.0 + jnp.exp(-gr))


def _pack_small_weight(w_in):
    d = w_in.shape[0]

    def cols(a, b):
        return w_in[:, a:b]

    def zeros(n):
        return jnp.zeros((d, n), w_in.dtype)

    half = MLA_ROPE // 2
    kr1, kr2 = cols(O_KR, O_KR + half), cols(O_KR + half, O_KR + MLA_ROPE)
    pad_tail = HEAD_PAD - MLA_NOPE - MLA_ROPE
    pieces = [
        cols(O_CQ, O_CKV), cols(O_CKV, O_KR),
        zeros(MLA_NOPE), kr1, kr2, zeros(pad_tail),
        zeros(MLA_NOPE), kr2, kr1, zeros(pad_tail),
        cols(O_CB, O_CC), cols(O_CC, O_CX), cols(O_CX, O_DQ),
        cols(O_DQ, O_DK), cols(O_DK, O_DV), cols(O_DK, O_DV), cols(O_DV, O_IQ), cols(O_DV, O_IQ),
        cols(O_IQ, O_IK), cols(O_IK, O_IW), cols(O_IK, O_IW), cols(O_IK, O_IW), cols(O_IK, O_IW),
        cols(O_IW, O_GQ), zeros(LANE - IDX_HEADS),
        cols(O_GQ, O_GK), cols(O_GK, O_GV), cols(O_GV, O_GLR),
        cols(O_GLR, O_GR), zeros(LANE - GLA_GATE_RANK),
        cols(O_GR, O_GATE),
    ]
    out = jnp.concatenate(pieces, axis=1)
    assert out.shape[1] == P_TOTAL, out.shape
    return out.astype(BF16)


def _pack_mla_weights(w_q_up, w_kv_up):
    half = MLA_ROPE // 2
    rq = w_q_up.shape[0]
    rk = w_kv_up.shape[0]
    wq = w_q_up.reshape(rq, MLA_HEADS, MLA_NOPE + MLA_ROPE)
    nope, x1, x2 = wq[..., :MLA_NOPE], wq[..., MLA_NOPE:MLA_NOPE + half], wq[..., MLA_NOPE + half:]
    zq = jnp.zeros((rq, MLA_HEADS, HEAD_PAD - MLA_NOPE - MLA_ROPE), w_q_up.dtype)
    wq_pad = jnp.concatenate([nope, x1, x2, zq], axis=-1).reshape(rq, MLA_HEADS * HEAD_PAD)
    wq_swap = jnp.concatenate([jnp.zeros_like(nope), x2, x1, zq], axis=-1).reshape(rq, MLA_HEADS * HEAD_PAD)
    wkv = w_kv_up.reshape(rk, MLA_HEADS, MLA_NOPE + MLA_V)
    zk = jnp.zeros((rk, MLA_HEADS, HEAD_PAD - MLA_NOPE), w_kv_up.dtype)
    wk_pad = jnp.concatenate([wkv[..., :MLA_NOPE], zk], axis=-1).reshape(rk, MLA_HEADS * HEAD_PAD)
    wv = wkv[..., MLA_NOPE:].reshape(rk, MLA_HEADS * MLA_V)
    return wq_pad.astype(BF16), wq_swap.astype(BF16), wk_pad.astype(BF16), wv.astype(BF16)


def _prep(x2, nb, g, w_small, qg, wq, wqs, kvg, wk, wv, ctab, stab, conv_w, wgate, bgate, tm):
    t, d = x2.shape
    nj = t // nb // tm

    def tok(width):
        return pl.BlockSpec((tm, width), lambda b, j: (b * nj + j, 0))

    def full(a):
        return pl.BlockSpec(a.shape, lambda b, j: (0,) * a.ndim, pipeline_mode=pl.Buffered(1))

    outs = [(512, BF16), (512, BF16), (256, BF16), (256, BF16), (256, BF16), (LANE, BF16), (LANE, BF16),
            (256, BF16), (LANE, BF16), (LANE, F32), (LANE, F32), (LANE, F32), (256, F32), (LANE, F32), (256, F32)]
    return pl.pallas_call(
        functools.partial(_prep_kernel, tm=tm),
        grid=(nb, nj),
        in_specs=[tok(d), full(g), full(w_small), full(qg), full(wq), full(wqs), full(kvg), full(wk), full(wv),
                  tok(HEAD_PAD), tok(HEAD_PAD), full(conv_w), full(wgate), full(bgate)],
        out_specs=[tok(w) for w, _ in outs],
        out_shape=[jax.ShapeDtypeStruct((t, w), dt) for w, dt in outs],
        scratch_shapes=[pltpu.VMEM((tm + 8, CONV_WIDTH), F32)],
        compiler_params=_cparams(("arbitrary", "arbitrary")),
        name="prep",
    )(x2, g, w_small, qg, wq, wqs, kvg, wk, wv, ctab, stab, conv_w, wgate, bgate)


def _mla_kernel(q_ref, k_ref, v_ref, o_ref, *, tq):
    i = pl.program_id(1)
    row = lax.broadcasted_iota(I32, (tq, tq), 0)
    col = lax.broadcasted_iota(I32, (tq, tq), 1)
    diag_ok = col <= row
    lane = lax.broadcasted_iota(I32, (tq, LANE), 1)

    for pair in range(MLA_HEADS // 2):
        vs = slice(pair * LANE, (pair + 1) * LANE)
        res = []
        for hh in range(2):
            hs = slice((2 * pair + hh) * HEAD_PAD, (2 * pair + hh + 1) * HEAD_PAD)
            q = q_ref[:, hs]

            def step(j, carry, masked, hs=hs, vs=vs, q=q):
                m, l, acc = carry
                off = pl.multiple_of(j * tq, tq)
                s = _dot_nt(q, k_ref[pl.ds(off, tq), hs])
                if masked:
                    s = jnp.where(diag_ok, s, NEG)
                m_new = jnp.maximum(m, jnp.max(s, axis=1, keepdims=True))
                alpha = jnp.exp(m - m_new)
                pexp = jnp.exp(s - m_new)
                l = alpha * l + jnp.sum(pexp, axis=1, keepdims=True)
                acc = alpha * acc + _dot(pexp.astype(BF16), v_ref[pl.ds(off, tq), vs])
                return m_new, l, acc

            init = (jnp.full((tq, 1), -jnp.inf, F32), jnp.zeros((tq, 1), F32), jnp.zeros((tq, LANE), F32))
            carry = lax.fori_loop(0, i, functools.partial(step, masked=False), init)
            m, l, acc = step(i, carry, True)
            res.append(acc / l)
        o_ref[:, vs] = jnp.where(lane < MLA_V, res[0], res[1]).astype(BF16)


def _mla(mq, mk, mv, nb, tq):
    t = mq.shape[0]
    s = t // nb
    nq = s // tq
    return pl.pallas_call(
        functools.partial(_mla_kernel, tq=tq),
        grid=(nb, nq),
        in_specs=[pl.BlockSpec((tq, MLA_HEADS * HEAD_PAD), lambda b, i: (b * nq + i, 0)),
                  pl.BlockSpec((s, MLA_HEADS * HEAD_PAD), lambda b, i: (b, 0)),
                  pl.BlockSpec((s, MLA_HEADS * MLA_V), lambda b, i: (b, 0))],
        out_specs=pl.BlockSpec((tq, MLA_HEADS * MLA_V), lambda b, i: (b * nq + i, 0)),
        out_shape=jax.ShapeDtypeStruct((t, MLA_HEADS * MLA_V), BF16),
        compiler_params=_cparams(("arbitrary", "arbitrary")),
        name="mla_attn",
    )(mq, mk, mv)


def _dsa_kernel(iq_ref, iw_ref, ikr_ref, dq_ref, dkk_ref, dvv_ref, o_ref, key_scr, *, tq, tk, topk, idx_bits):
    i = pl.program_id(1)
    q_end = (i + 1) * tq
    n_chunks = (q_end + tk - 1) // tk
    qpos = i * tq + lax.broadcasted_iota(I32, (tq, 1), 0)
    lane_q = lax.broadcasted_iota(I32, (tq, LANE), 1)
    lane_k = lax.broadcasted_iota(I32, (tq, tk), 1)
    int_min = jnp.int32(-2 ** 31)

    iq = iq_ref[...].astype(F32)
    iw = iw_ref[...]
    q_heads = []
    w_heads = []
    for h in range(IDX_HEADS):
        grp = h // 4
        sub = iq[:, grp * LANE:(grp + 1) * LANE]
        lo = (h % 4) * IDX_DIM
        in_head = (lane_q >= lo) & (lane_q < lo + IDX_DIM)
        q_heads.append(jnp.where(in_head, sub, 0.0).astype(BF16))
        w_heads.append(iw[:, h:h + 1])

    def score_chunk(c, _):
        off = pl.multiple_of(c * tk, tk)
        kc = ikr_ref[pl.ds(off, tk), :]
        acc = jnp.zeros((tq, tk), F32)
        for h in range(IDX_HEADS):
            acc = acc + w_heads[h] * jnp.maximum(_dot_nt(q_heads[h], kc), 0.0)
        acc = jnp.where(c * tk + lane_k <= qpos, acc, NEG)
        bits = pltpu.bitcast(acc, I32)
        key_scr[c] = jnp.where(bits < 0, bits ^ jnp.int32(0x7FFFFFFF), bits)
        return 0

    lax.fori_loop(0, n_chunks, score_chunk, 0)

    def count(pred):
        def body(c, acc):
            hit = jnp.where(pred(key_scr[c], c * tk + lane_k), 1.0, 0.0)
            part = hit[:, 0:LANE]
            for g in range(1, tk // LANE):
                part = part + hit[:, g * LANE:(g + 1) * LANE]
            return acc + part
        acc = lax.fori_loop(0, n_chunks, body, jnp.zeros((tq, LANE), F32))
        return jnp.sum(acc, axis=1, keepdims=True)

    kf = jnp.float32(topk)

    def select():
        def bit_step(it, thr):
            cand = thr ^ lax.shift_left(jnp.int32(1), 31 - it)
            c = count(lambda kk, idx: kk >= cand)
            return jnp.where(c >= kf, cand, thr)
        thr = lax.fori_loop(0, 32, bit_step, jnp.full((tq, 1), int_min, I32))
        c_ge = count(lambda kk, idx: kk >= thr)
        excess = c_ge > kf

        def tie_break():
            need = kf - count(lambda kk, idx: kk > thr)

            def idx_step(it, cut):
                cand = cut + lax.shift_left(jnp.int32(1), idx_bits - 1 - it)
                c = count(lambda kk, idx: (kk == thr) & (idx < cand))
                return jnp.where(c < need, cand, cut)
            cut = lax.fori_loop(0, idx_bits, idx_step, jnp.zeros((tq, 1), I32))
            return jnp.where(excess, cut, jnp.int32(2 ** 30))

        any_excess = jnp.max(jnp.where(excess, 1.0, 0.0)) > 0.0
        cut = lax.cond(any_excess, tie_break, lambda: jnp.full((tq, 1), 2 ** 30, I32))
        return thr, cut

    thr, cut = lax.cond(q_end > topk, select,
                        lambda: (jnp.full((tq, 1), int_min, I32), jnp.full((tq, 1), 2 ** 30, I32)))

    dq = dq_ref[...].astype(F32)
    qm = []
    for h in range(DSA_HEADS):
        sub = dq[:, (h // 2) * LANE:(h // 2 + 1) * LANE]
        lo = (h % 2) * DSA_HEAD_DIM
        in_head = (lane_q >= lo) & (lane_q < lo + DSA_HEAD_DIM)
        qm.append(jnp.where(in_head, sub, 0.0).astype(BF16))

    def att_chunk(c, carry):
        off = pl.multiple_of(c * tk, tk)
        kk = key_scr[c]
        idx = c * tk + lane_k
        sel = ((kk > thr) | ((kk == thr) & (idx <= cut))) & (idx <= qpos)
        kc = dkk_ref[pl.ds(off, tk), :]
        vc = dvv_ref[pl.ds(off, tk), :]
        new = []
        for h in range(DSA_HEADS):
            m, l, acc = carry[h]
            s = jnp.where(sel, _dot_nt(qm[h], kc), NEG)
            m_new = jnp.maximum(m, jnp.max(s, axis=1, keepdims=True))
            alpha = jnp.exp(m - m_new)
            pexp = jnp.exp(s - m_new)
            l = alpha * l + jnp.sum(pexp, axis=1, keepdims=True)
            acc = alpha * acc + _dot(pexp.astype(BF16), vc)
            new.append((m_new, l, acc))
        return tuple(new)

    init = tuple((jnp.full((tq, 1), -jnp.inf, F32), jnp.zeros((tq, 1), F32), jnp.zeros((tq, LANE), F32))
                 for _ in range(DSA_HEADS))
    fin = lax.fori_loop(0, n_chunks, att_chunk, init)
    outs = [acc / l for (_, l, acc) in fin]
    lo_half = lane_q < DSA_HEAD_DIM
    o_ref[...] = jnp.concatenate([jnp.where(lo_half, outs[0], outs[1]),
                                  jnp.where(lo_half, outs[2], outs[3])], axis=1).astype(BF16)


def _dsa(iq, iw, ikr, dq, dkk, dvv, nb, tq, tk):
    t = iq.shape[0]
    s = t // nb
    nq = s // tq
    topk = min(DSA_TOPK_MAX, s // 4)
    idx_bits = max(1, int(np.ceil(np.log2(s))))

    def qspec(w):
        return pl.BlockSpec((tq, w), lambda b, i: (b * nq + i, 0))

    def kspec(w):
        return pl.BlockSpec((s, w), lambda b, i: (b, 0))

    return pl.pallas_call(
        functools.partial(_dsa_kernel, tq=tq, tk=tk, topk=topk, idx_bits=idx_bits),
        grid=(nb, nq),
        in_specs=[qspec(256), qspec(LANE), kspec(LANE), qspec(256), kspec(LANE), kspec(LANE)],
        out_specs=qspec(256),
        out_shape=jax.ShapeDtypeStruct((t, DSA_HEADS * DSA_HEAD_DIM), BF16),
        scratch_shapes=[pltpu.VMEM((s // tk, tq, tk), I32)],
        compiler_params=_cparams(("arbitrary", "arbitrary")),
        name="dsa",
    )(iq, iw, ikr, dq, dkk, dvv)


def _gla_kernel(gq_ref, gk_ref, gv_ref, gg_ref, gr_ref, ng_ref, segv_ref, seg64_ref, bmask_ref, o_ref,
                st_scr, kbuf, cbuf, vbuf, *, tg):
    j = pl.program_id(1)
    ch = GLA_CHUNK
    rowc = lax.broadcasted_iota(I32, (tg, 1), 0) % ch

    @pl.when(j == 0)
    def _():
        st_scr[...] = jnp.zeros_like(st_scr)
        kbuf[0:ch, :] = jnp.zeros((ch, LANE), F32)
        cbuf[0:ch, :] = jnp.zeros((ch, LANE), F32)
        vbuf[0:ch, :] = jnp.zeros((ch, 2 * LANE), F32)

    cbuf[ch:ch + tg, :] = gg_ref[...]
    sh = 1
    while sh < ch:
        cur = cbuf[ch:ch + tg, :]
        prev = cbuf[ch - sh:ch - sh + tg, :]
        cbuf[ch:ch + tg, :] = cur + jnp.where(rowc >= sh, prev, 0.0)
        sh *= 2
    cum = cbuf[ch:ch + tg, :]
    q = gq_ref[...]
    k = gk_ref[...]
    v = gv_ref[...]
    kbuf[ch:ch + tg, :] = k
    vbuf[ch:ch + tg, :] = v

    qd = (q * jnp.exp(cum)).astype(BF16)
    bmask = bmask_ref[...]
    inter = []
    for c in range(tg // ch):
        rs = slice(c * ch, (c + 1) * ch)
        st = st_scr[...]
        inter.append(_dot_nt(qd[rs], st.astype(BF16)))
        last = cum[c * ch + ch - 1:c * ch + ch, :]
        kp = k[rs] * jnp.exp(last - cum[rs])
        upd = lax.dot_general(v[rs].astype(BF16), kp.astype(BF16), (((0,), (0,)), ((), ())),
                              preferred_element_type=F32)
        st_scr[...] = jnp.exp(last) * st + upd * bmask
    o = jnp.concatenate(inter, axis=0)

    segv = segv_ref[...]
    for d in range(ch):
        kd = kbuf[ch - d:ch - d + tg, :]
        cd = cbuf[ch - d:ch - d + tg, :]
        vd = vbuf[ch - d:ch - d + tg, :]
        ok = rowc >= d
        t = jnp.where(ok, q * kd * jnp.exp(jnp.where(ok, cum - cd, 0.0)), 0.0)
        o = o + _dot(t.astype(BF16), segv) * vd

    sq = o * o
    hi = sq.astype(BF16)
    lo = (sq - hi.astype(F32)).astype(BF16)
    var = _dot(hi, seg64_ref[...]) + _dot(lo, seg64_ref[...])
    o_ref[...] = (gr_ref[...] * (o * lax.rsqrt(var + EPS) * ng_ref[...])).astype(BF16)


def _gla(gq, gk, gv, gg, gr, norm_g, nb, tg):
    t = gq.shape[0]
    nj = t // nb // tg
    ng = jnp.tile(norm_g.reshape(1, GLA_DV), (1, GLA_HEADS)).astype(F32)
    hk = np.arange(GLA_HEADS * GLA_DK) // GLA_DK
    hv = np.arange(GLA_HEADS * GLA_DV) // GLA_DV
    same = (hk[:, None] == hv[None, :])
    segv = jnp.asarray(same, BF16)
    bmask = jnp.asarray(same.T, F32)
    seg64 = jnp.asarray((hv[:, None] == hv[None, :]) / GLA_DV, BF16)

    def tok(w):
        return pl.BlockSpec((tg, w), lambda b, j: (b * nj + j, 0))

    def full(a):
        return pl.BlockSpec(a.shape, lambda b, j: (0,) * a.ndim, pipeline_mode=pl.Buffered(1))

    return pl.pallas_call(
        functools.partial(_gla_kernel, tg=tg),
        grid=(nb, nj),
        in_specs=[tok(LANE), tok(LANE), tok(256), tok(LANE), tok(256), full(ng), full(segv), full(seg64), full(bmask)],
        out_specs=tok(256),
        out_shape=jax.ShapeDtypeStruct((t, GLA_HEADS * GLA_DV), BF16),
        scratch_shapes=[pltpu.VMEM((2 * LANE, LANE), F32),
                        pltpu.VMEM((GLA_CHUNK + tg, LANE), F32),
                        pltpu.VMEM((GLA_CHUNK + tg, LANE), F32),
                        pltpu.VMEM((GLA_CHUNK + tg, 2 * LANE), F32)],
        compiler_params=_cparams(("arbitrary", "arbitrary")),
        name="gla",
    )(gq, gk, gv, gg, gr, ng, segv, seg64, bmask)


def _merge_kernel(x_ref, g_ref, wg_ref, o0_ref, o1_ref, o2_ref, o3_ref, wb_ref, wo_ref, out_ref):
    x = x_ref[...]
    h = _rms(x, g_ref[...]).astype(BF16)
    merged = None
    for n, o_ref in enumerate((o0_ref, o1_ref, o2_ref, o3_ref)):
        gate = _dot(h, wg_ref[:, n * D_MODEL:(n + 1) * D_MODEL])
        gate = 1.0 / (1.0 + jnp.exp(-gate))
        term = gate * _dot(o_ref[...], wb_ref[n])
        merged = term if merged is None else merged + term
    out_ref[...] = x + _dot(merged.astype(BF16), wo_ref[...])


def _merge(x2, g, wg, branches, wb, wo, tm):
    t, d = x2.shape

    def tok(w):
        return pl.BlockSpec((tm, w), lambda i: (i, 0))

    def full(a):
        return pl.BlockSpec(a.shape, lambda i: (0,) * a.ndim, pipeline_mode=pl.Buffered(1))

    return pl.pallas_call(
        _merge_kernel,
        grid=(t // tm,),
        in_specs=[tok(d), full(g), full(wg)] + [tok(BRANCH_WIDTH)] * N_BRANCH + [full(wb), full(wo)],
        out_specs=tok(d),
        out_shape=jax.ShapeDtypeStruct((t, d), F32),
        compiler_params=_cparams(("arbitrary",)),
        name="merge",
    )(x2, g, wg, *branches, wb, wo)


def _ffn_kernel(x_ref, g_ref, wgu_ref, wd_ref, fg_ref, out_ref, *, fc, final):
    x = x_ref[...]
    h = _rms(x, g_ref[...]).astype(BF16)
    acc = x
    for c in range(D_FF // fc):
        gate = _dot(h, wgu_ref[0, :, c * fc:(c + 1) * fc])
        up = _dot(h, wgu_ref[1, :, c * fc:(c + 1) * fc])
        act = (gate / (1.0 + jnp.exp(-gate)) * up).astype(BF16)
        acc = acc + _dot(act, wd_ref[c * fc:(c + 1) * fc, :])
    if final:
        acc = _rms(acc, fg_ref[...])
    out_ref[...] = acc


def _ffn(x2, g, wgu, wd, fg, tm, final):
    t, d = x2.shape

    def tok(w):
        return pl.BlockSpec((tm, w), lambda i: (i, 0))

    def full(a):
        return pl.BlockSpec(a.shape, lambda i: (0,) * a.ndim, pipeline_mode=pl.Buffered(1))

    return pl.pallas_call(
        functools.partial(_ffn_kernel, fc=256, final=final),
        grid=(t // tm,),
        in_specs=[tok(d), full(g), full(wgu), full(wd), full(fg)],
        out_specs=tok(d),
        out_shape=jax.ShapeDtypeStruct((t, d), F32),
        compiler_params=_cparams(("arbitrary",)),
        name="ffn",
    )(x2, g, wgu, wd, fg)


def _tile(n, pref):
    while n % pref:
        pref //= 2
    return pref


def kernel(x, positions, attn_norm_g, w_in, mla_q_norm_g, mla_w_q_up, mla_kv_norm_g, mla_w_kv_up, conv_w, gla_w_gate_up, gla_b_gate, gla_norm_g, w_branch, w_out, ffn_norm_g, w_ffn_gate, w_ffn_up, w_ffn_down, final_norm_g):
    b, s, d = x.shape
    depth = w_in.shape[0]
    t = b * s
    assert d == D_MODEL and s % GLA_CHUNK == 0
    tm_prep = _tile(s, 256)
    tm_wide = _tile(t, 512)
    tq_mla = _tile(s, 256)
    tq_dsa = _tile(s, 128)
    tk_dsa = _tile(s, 256)
    tg = _tile(s, 256)

    x2 = x.reshape(t, d)
    ctab, stab = _rope_tables(positions, _tile(t, 512))
    row = lambda a: a.reshape(1, -1).astype(F32)

    for l in range(depth):
        w_small = _pack_small_weight(w_in[l])
        wq, wqs, wk, wv = _pack_mla_weights(mla_w_q_up[l], mla_w_kv_up[l])
        wgate = jnp.concatenate([gla_w_gate_up[l], jnp.zeros((LANE - GLA_GATE_RANK, GLA_HEADS * GLA_DK), F32)],
                                axis=0).astype(BF16)
        (mq, mk, mv, o_conv, dq, dkk, dvv, iq, ikr, iw, gq, gk, gv, gg, gr) = _prep(
            x2, b, row(attn_norm_g[l]), w_small, row(mla_q_norm_g[l]), wq, wqs, row(mla_kv_norm_g[l]), wk, wv,
            ctab, stab, conv_w[l].astype(F32), wgate, row(gla_b_gate[l]), tm_prep)
        o_mla = _mla(mq, mk, mv, b, tq_mla)
        o_dsa = _dsa(iq, iw, ikr, dq, dkk, dvv, b, tq_dsa, tk_dsa)
        o_gla = _gla(gq, gk, gv, gg, gr, gla_norm_g[l], b, tg)
        x2 = _merge(x2, row(attn_norm_g[l]), w_in[l][:, O_GATE:].astype(BF16), (o_mla, o_conv, o_dsa, o_gla),
                    w_branch[l].astype(BF16), w_out[l].astype(BF16), tm_wide)
        wgu = jnp.stack([w_ffn_gate[l], w_ffn_up[l]]).astype(BF16)
        x2 = _ffn(x2, row(ffn_norm_g[l]), wgu, w_ffn_down[l].astype(BF16), row(final_norm_g), tm_wide,
                  final=(l == depth - 1))
    return x2.reshape(b, s, d)
```

```python
import functools

import jax
import jax.numpy as jnp
import numpy as np
from jax import lax
from jax.experimental import pallas as pl
from jax.experimental.pallas import tpu as pltpu

F32 = jnp.float32
BF16 = jnp.bfloat16
I32 = jnp.int32
I16 = jnp.int16

D_MODEL = 1024
MLA_HEADS = 4
MLA_Q_RANK = 256
MLA_KV_RANK = 128
MLA_NOPE = 64
MLA_ROPE = 32
MLA_V = 64
ROPE_THETA = 10000.0
CONV_WIDTH = 256
CONV_K = 3
DSA_HEADS = 4
DSA_HEAD_DIM = 64
IDX_HEADS = 8
IDX_DIM = 32
DSA_TOPK_MAX = 256
GLA_HEADS = 4
GLA_DK = 32
GLA_DV = 64
GLA_GATE_RANK = 16
GLA_TAU = 16.0
GLA_CHUNK = 64
N_BRANCH = 4
BRANCH_WIDTH = 256
D_FF = ((8 * D_MODEL + 3 * 256 - 1) // (3 * 256)) * 256
EPS = 1e-6
NEG = -1e30
LOG2E = 1.4426950408889634

LANE = 128
HEAD_PAD = 128

IN_SIZES = (
    MLA_Q_RANK, MLA_KV_RANK, MLA_ROPE,
    CONV_WIDTH, CONV_WIDTH, CONV_WIDTH,
    DSA_HEADS * DSA_HEAD_DIM, DSA_HEAD_DIM, DSA_HEAD_DIM,
    IDX_HEADS * IDX_DIM, IDX_DIM, IDX_HEADS,
    GLA_HEADS * GLA_DK, GLA_HEADS * GLA_DK, GLA_HEADS * GLA_DV,
    GLA_GATE_RANK, GLA_HEADS * GLA_DV,
    N_BRANCH * D_MODEL,
)
_OFF = np.concatenate([[0], np.cumsum(IN_SIZES)]).astype(int)
(O_CQ, O_CKV, O_KR, O_CB, O_CC, O_CX, O_DQ, O_DK, O_DV, O_IQ, O_IK, O_IW,
 O_GQ, O_GK, O_GV, O_GLR, O_GR, O_GATE, O_END) = [int(v) for v in _OFF]

P_CQ, P_CKV, P_KRP, P_KRS = 0, 256, 384, 512
P_CB, P_CC, P_CX = 640, 896, 1152
P_DQ, P_DKK, P_DVV = 1408, 1664, 1792
P_IQ, P_IKR, P_IW = 1920, 2176, 2304
P_GQ, P_GK, P_GV, P_GLR, P_GR = 2432, 2560, 2688, 2944, 3072
P_TOTAL = 3328

VMEM_LIMIT = 56 * 1024 * 1024


def _cparams(sem):
    return pltpu.CompilerParams(dimension_semantics=sem, vmem_limit_bytes=VMEM_LIMIT)


def _rms(x, g):
    return x * lax.rsqrt(jnp.mean(x * x, axis=-1, keepdims=True) + EPS) * g


def _dot(a, b):
    return jnp.dot(a, b, preferred_element_type=F32)


def _dot_nt(a, b):
    return lax.dot_general(a, b, (((1,), (1,)), ((), ())), preferred_element_type=F32)


def _rope_kernel(pos_ref, pat_ref, c_ref, s_ref):
    pos = pos_ref[...].astype(F32)
    ang = pos * pat_ref[0:1, :]
    c_ref[...] = pat_ref[1:2, :] + pat_ref[2:3, :] * jnp.cos(ang)
    s_ref[...] = pat_ref[3:4, :] * jnp.sin(ang)


def _rope_tables(positions, tm):
    t = positions.size
    half = MLA_ROPE // 2
    inv_freq = ROPE_THETA ** (-jnp.arange(half, dtype=F32) / half)
    z64, z32 = jnp.zeros((MLA_NOPE,), F32), jnp.zeros((HEAD_PAD - MLA_NOPE - MLA_ROPE,), F32)
    o16 = jnp.ones((half,), F32)
    pat = jnp.stack([
        jnp.concatenate([z64, inv_freq, inv_freq, z32]),
        jnp.concatenate([z64 + 1.0, 0 * o16, 0 * o16, z32]),
        jnp.concatenate([z64, o16, o16, z32]),
        jnp.concatenate([z64, -o16, o16, z32]),
    ])
    pat = jnp.concatenate([pat, jnp.zeros((4, HEAD_PAD), F32)], axis=0)
    return pl.pallas_call(
        _rope_kernel,
        grid=(t // tm,),
        in_specs=[pl.BlockSpec((tm, 1), lambda i: (i, 0)), pl.BlockSpec((8, HEAD_PAD), lambda i: (0, 0))],
        out_specs=[pl.BlockSpec((tm, HEAD_PAD), lambda i: (i, 0))] * 2,
        out_shape=[jax.ShapeDtypeStruct((t, HEAD_PAD), F32)] * 2,
        compiler_params=_cparams(("arbitrary",)),
        name="rope_tables",
    )(positions.reshape(t, 1), pat)


def _prep_kernel(x_ref, g_ref, w_ref, qg_ref, wq_ref, wqs_ref, kvg_ref, wk_ref, wv_ref, c_ref, s_ref,
                 convw_ref, wgate_ref, bgate_ref,
                 mq_ref, mk_ref, mv_ref, oconv_ref, dq_ref, dkk_ref, dvv_ref, iq_ref, ikr_ref, iw_ref,
                 gq_ref, gk_ref, gv_ref, gg_ref, gr_ref, u_scr, *, tm):
    j = pl.program_id(1)
    h = _rms(x_ref[...], g_ref[...]).astype(BF16)
    p = _dot(h, w_ref[...])

    c1, s1 = c_ref[...], s_ref[...]
    c4 = jnp.concatenate([c1] * MLA_HEADS, axis=1)
    s4 = jnp.concatenate([s1] * MLA_HEADS, axis=1)

    qn = _rms(p[:, P_CQ:P_CQ + MLA_Q_RANK], qg_ref[...]).astype(BF16)
    q = _dot(qn, wq_ref[...]) * c4 + _dot(qn, wqs_ref[...]) * s4
    mq_ref[...] = (q * ((MLA_NOPE + MLA_ROPE) ** -0.5 * LOG2E)).astype(BF16)
    kvn = _rms(p[:, P_CKV:P_CKV + MLA_KV_RANK], kvg_ref[...]).astype(BF16)
    kr = p[:, P_KRP:P_KRP + HEAD_PAD] * c1 + p[:, P_KRS:P_KRS + HEAD_PAD] * s1
    mk_ref[...] = (_dot(kvn, wk_ref[...]) + jnp.concatenate([kr] * MLA_HEADS, axis=1)).astype(BF16)
    mv_ref[...] = _dot(kvn, wv_ref[...]).astype(BF16)

    u = p[:, P_CC:P_CC + CONV_WIDTH] * p[:, P_CX:P_CX + CONV_WIDTH]

    @pl.when(j == 0)
    def _():
        u_scr[0:8, :] = jnp.zeros((8, CONV_WIDTH), F32)

    u_scr[8:8 + tm, :] = u
    cw = convw_ref[...]
    y = cw[0:1, :] * u_scr[6:6 + tm, :] + cw[1:2, :] * u_scr[7:7 + tm, :] + cw[2:3, :] * u
    oconv_ref[...] = (p[:, P_CB:P_CB + CONV_WIDTH] * y).astype(BF16)
    u_scr[0:8, :] = u_scr[tm:tm + 8, :]

    dq_ref[...] = (p[:, P_DQ:P_DQ + 256] * (DSA_HEAD_DIM ** -0.5 * LOG2E)).astype(BF16)
    dkk_ref[...] = p[:, P_DKK:P_DKK + LANE].astype(BF16)
    dvv_ref[...] = p[:, P_DVV:P_DVV + LANE].astype(BF16)
    iq_ref[...] = p[:, P_IQ:P_IQ + 256].astype(BF16)
    ikr_ref[...] = p[:, P_IKR:P_IKR + LANE].astype(BF16)
    iw_ref[...] = p[:, P_IW:P_IW + LANE] * ((IDX_DIM ** -0.5) * (IDX_HEADS ** -0.5))

    gq_ref[...] = p[:, P_GQ:P_GQ + LANE] * (GLA_DK ** -0.5)
    gk_ref[...] = p[:, P_GK:P_GK + LANE]
    gv_ref[...] = p[:, P_GV:P_GV + 256]
    gate = _dot(p[:, P_GLR:P_GLR + LANE].astype(BF16), wgate_ref[...]) + bgate_ref[...]
    log_sig = jnp.minimum(gate, 0.0) - jnp.log1p(jnp.exp(-jnp.abs(gate)))
    gg_ref[...] = log_sig * (1.0 / GLA_TAU)
    gr = p[:, P_GR:P_GR + 256]
    gr_ref[...] = gr / (1.0 + jnp.exp(-gr))


def _pack_small_weight(w_in):
    d = w_in.shape[0]

    def cols(a, b):
        return w_in[:, a:b]

    def zeros(n):
        return jnp.zeros((d, n), w_in.dtype)

    half = MLA_ROPE // 2
    kr1, kr2 = cols(O_KR, O_KR + half), cols(O_KR + half, O_KR + MLA_ROPE)
    pad_tail = HEAD_PAD - MLA_NOPE - MLA_ROPE
    pieces = [
        cols(O_CQ, O_CKV), cols(O_CKV, O_KR),
        zeros(MLA_NOPE), kr1, kr2, zeros(pad_tail),
        zeros(MLA_NOPE), kr2, kr1, zeros(pad_tail),
        cols(O_CB, O_CC), cols(O_CC, O_CX), cols(O_CX, O_DQ),
        cols(O_DQ, O_DK), cols(O_DK, O_DV), cols(O_DK, O_DV), cols(O_DV, O_IQ), cols(O_DV, O_IQ),
        cols(O_IQ, O_IK), cols(O_IK, O_IW), cols(O_IK, O_IW), cols(O_IK, O_IW), cols(O_IK, O_IW),
        cols(O_IW, O_GQ), zeros(LANE - IDX_HEADS),
        cols(O_GQ, O_GK), cols(O_GK, O_GV), cols(O_GV, O_GLR),
        cols(O_GLR, O_GR), zeros(LANE - GLA_GATE_RANK),
        cols(O_GR, O_GATE),
    ]
    out = jnp.concatenate(pieces, axis=1)
    assert out.shape[1] == P_TOTAL, out.shape
    return out.astype(BF16)


def _pack_mla_weights(w_q_up, w_kv_up):
    half = MLA_ROPE // 2
    rq = w_q_up.shape[0]
    rk = w_kv_up.shape[0]
    wq = w_q_up.reshape(rq, MLA_HEADS, MLA_NOPE + MLA_ROPE)
    nope, x1, x2 = wq[..., :MLA_NOPE], wq[..., MLA_NOPE:MLA_NOPE + half], wq[..., MLA_NOPE + half:]
    zq = jnp.zeros((rq, MLA_HEADS, HEAD_PAD - MLA_NOPE - MLA_ROPE), w_q_up.dtype)
    wq_pad = jnp.concatenate([nope, x1, x2, zq], axis=-1).reshape(rq, MLA_HEADS * HEAD_PAD)
    wq_swap = jnp.concatenate([jnp.zeros_like(nope), x2, x1, zq], axis=-1).reshape(rq, MLA_HEADS * HEAD_PAD)
    wkv = w_kv_up.reshape(rk, MLA_HEADS, MLA_NOPE + MLA_V)
    zk = jnp.zeros((rk, MLA_HEADS, HEAD_PAD - MLA_NOPE), w_kv_up.dtype)
    wk_pad = jnp.concatenate([wkv[..., :MLA_NOPE], zk], axis=-1).reshape(rk, MLA_HEADS * HEAD_PAD)
    wv = wkv[..., MLA_NOPE:].reshape(rk, MLA_HEADS * MLA_V)
    return wq_pad.astype(BF16), wq_swap.astype(BF16), wk_pad.astype(BF16), wv.astype(BF16)


def _prep(x2, nb, g, w_small, qg, wq, wqs, kvg, wk, wv, ctab, stab, conv_w, wgate, bgate, tm):
    t, d = x2.shape
    nj = t // nb // tm

    def tok(width):
        return pl.BlockSpec((tm, width), lambda b, j: (b * nj + j, 0))

    def full(a):
        return pl.BlockSpec(a.shape, lambda b, j: (0,) * a.ndim, pipeline_mode=pl.Buffered(1))

    outs = [(512, BF16), (512, BF16), (256, BF16), (256, BF16), (256, BF16), (LANE, BF16), (LANE, BF16),
            (256, BF16), (LANE, BF16), (LANE, F32), (LANE, F32), (LANE, F32), (256, F32), (LANE, F32), (256, F32)]
    return pl.pallas_call(
        functools.partial(_prep_kernel, tm=tm),
        grid=(nb, nj),
        in_specs=[tok(d), full(g), full(w_small), full(qg), full(wq), full(wqs), full(kvg), full(wk), full(wv),
                  tok(HEAD_PAD), tok(HEAD_PAD), full(conv_w), full(wgate), full(bgate)],
        out_specs=[tok(w) for w, _ in outs],
        out_shape=[jax.ShapeDtypeStruct((t, w), dt) for w, dt in outs],
        scratch_shapes=[pltpu.VMEM((tm + 8, CONV_WIDTH), F32)],
        compiler_params=_cparams(("arbitrary", "arbitrary")),
        name="prep",
    )(x2, g, w_small, qg, wq, wqs, kvg, wk, wv, ctab, stab, conv_w, wgate, bgate)


def _fold_lanes(x, op):
    out = x[:, 0:LANE]
    for g in range(1, x.shape[1] // LANE):
        out = op(out, x[:, g * LANE:(g + 1) * LANE])
    return out


def _mla_kernel(q_ref, k_ref, v_ref, o_ref, s_scr, m_scr, l_scr, acc_scr, *, tq, tk):
    i = pl.program_id(1)
    n_chunks = ((i + 1) * tq + tk - 1) // tk
    row = i * tq + lax.broadcasted_iota(I32, (tq, tk), 0)
    col = lax.broadcasted_iota(I32, (tq, tk), 1)
    lane = lax.broadcasted_iota(I32, (tq, LANE), 1)
    heads = range(MLA_HEADS)
    hs = [slice(h * HEAD_PAD, (h + 1) * HEAD_PAD) for h in heads]
    vs = [slice((h // 2) * LANE, (h // 2 + 1) * LANE) for h in heads]

    m_scr[...] = jnp.full(m_scr.shape, NEG, F32)
    l_scr[...] = jnp.zeros(l_scr.shape, F32)
    acc_scr[...] = jnp.zeros(acc_scr.shape, F32)

    def phase1(c, carry, masked):
        off = pl.multiple_of(c * tk, tk)
        for h in heads:
            s = _dot_nt(q_ref[:, hs[h]], k_ref[pl.ds(off, tk), hs[h]])
            if masked:
                s = jnp.where(c * tk + col <= row, s, NEG)
            s_scr[h, c] = s
            m_scr[h] = jnp.maximum(m_scr[h], _fold_lanes(s, jnp.maximum))
        return carry

    lax.fori_loop(0, n_chunks - 1, functools.partial(phase1, masked=False), 0)
    phase1(n_chunks - 1, 0, True)
    for h in heads:
        m_scr[h] = jnp.broadcast_to(jnp.max(m_scr[h], axis=1, keepdims=True), (tq, LANE))

    def phase2(c, carry):
        off = pl.multiple_of(c * tk, tk)
        for h in heads:
            p = jnp.exp2(s_scr[h, c] - jnp.concatenate([m_scr[h]] * (tk // LANE), axis=1))
            l_scr[h] = l_scr[h] + _fold_lanes(p, jnp.add)
            acc_scr[h] = acc_scr[h] + _dot(p.astype(BF16), v_ref[pl.ds(off, tk), vs[h]])
        return carry

    lax.fori_loop(0, n_chunks, phase2, 0)
    res = [acc_scr[h] / jnp.sum(l_scr[h], axis=1, keepdims=True) for h in heads]
    for pair in range(MLA_HEADS // 2):
        o_ref[:, pair * LANE:(pair + 1) * LANE] = jnp.where(lane < MLA_V, res[2 * pair], res[2 * pair + 1]).astype(BF16)


def _mla(mq, mk, mv, nb, tq, tk):
    t = mq.shape[0]
    s = t // nb
    nq = s // tq
    return pl.pallas_call(
        functools.partial(_mla_kernel, tq=tq, tk=tk),
        grid=(nb, nq),
        in_specs=[pl.BlockSpec((tq, MLA_HEADS * HEAD_PAD), lambda b, i: (b * nq + i, 0)),
                  pl.BlockSpec((s, MLA_HEADS * HEAD_PAD), lambda b, i: (b, 0)),
                  pl.BlockSpec((s, MLA_HEADS * MLA_V), lambda b, i: (b, 0))],
        out_specs=pl.BlockSpec((tq, MLA_HEADS * MLA_V), lambda b, i: (b * nq + i, 0)),
        out_shape=jax.ShapeDtypeStruct((t, MLA_HEADS * MLA_V), BF16),
        scratch_shapes=[pltpu.VMEM((MLA_HEADS, s // tk, tq, tk), F32)] + [pltpu.VMEM((MLA_HEADS, tq, LANE), F32)] * 3,
        compiler_params=_cparams(("arbitrary", "arbitrary")),
        name="mla_attn",
    )(mq, mk, mv)


def _dsa_kernel(iq_ref, iw_ref, ikr_ref, dq_ref, dkk_ref, dvv_ref, o_ref,
                key_scr, hi_scr, lo_scr, z_scr, s_scr, m_scr, l_scr, acc_scr, *, tq, tk, topk, idx_bits):
    i = pl.program_id(1)
    q_end = (i + 1) * tq
    n_chunks = (q_end + tk - 1) // tk
    qpos = i * tq + lax.broadcasted_iota(I32, (tq, 1), 0)
    lane_q = lax.broadcasted_iota(I32, (tq, LANE), 1)
    lane_k = lax.broadcasted_iota(I32, (tq, tk), 1)
    int_min = jnp.int32(-2 ** 31)

    iq = iq_ref[...].astype(F32)
    iw = iw_ref[...]
    q_heads = []
    w_heads = []
    for h in range(IDX_HEADS):
        grp = h // 4
        sub = iq[:, grp * LANE:(grp + 1) * LANE]
        lo = (h % 4) * IDX_DIM
        in_head = (lane_q >= lo) & (lane_q < lo + IDX_DIM)
        q_heads.append(jnp.where(in_head, sub, 0.0).astype(BF16))
        w_heads.append(iw[:, h:h + 1])
    q_stack = jnp.concatenate(q_heads, axis=0)
    rb = 64
    lane_rb = lax.broadcasted_iota(I32, (rb, tk), 1)
    row_rb = lax.broadcasted_iota(I32, (rb, 1), 0)

    def score_chunk(c, _):
        off = pl.multiple_of(c * tk, tk)
        z_scr[...] = _dot_nt(q_stack, ikr_ref[pl.ds(off, tk), :])
        for r in range(0, tq, rb):
            acc = w_heads[0][r:r + rb] * jnp.maximum(z_scr[r:r + rb, :], 0.0)
            for h in range(1, IDX_HEADS):
                acc = acc + w_heads[h][r:r + rb] * jnp.maximum(z_scr[h * tq + r:h * tq + r + rb, :], 0.0)
            bits = pltpu.bitcast(acc, I32)
            key = jnp.where(bits < 0, bits ^ jnp.int32(0x7FFFFFFF), bits)
            key = jnp.where(c * tk + lane_rb <= i * tq + r + row_rb, key, int_min)
            key_scr[c, r:r + rb, :] = key
            hi_scr[c, r:r + rb, :] = lax.shift_right_arithmetic(key, 16).astype(I16)
        return 0

    lax.fori_loop(0, n_chunks, score_chunk, 0)

    def count(pred):
        def body(c, acc):
            hit = jnp.where(pred(key_scr[c], c * tk + lane_k), 1.0, 0.0)
            return acc + _fold_lanes(hit, jnp.add)
        acc = lax.fori_loop(0, n_chunks, body, jnp.zeros((tq, LANE), F32))
        return jnp.sum(acc, axis=1, keepdims=True)

    def count16(ref, pred):
        def body(c, acc):
            hit = jnp.where(pred(ref[c]), jnp.int16(1), jnp.int16(0))
            return acc + _fold_lanes(hit, jnp.add)
        acc = lax.fori_loop(0, n_chunks, body, jnp.zeros((tq, LANE), I16))
        return jnp.sum(acc.astype(F32), axis=1, keepdims=True)

    def bisect16(ref, need):
        def bit_step(it, v):
            cand = v + lax.shift_left(jnp.int32(1), 15 - it)
            cand16 = cand.astype(I16)
            c = count16(ref, lambda a: a >= cand16)
            return jnp.where(c >= need, cand, v)
        return lax.fori_loop(0, 16, bit_step, jnp.full((tq, 1), -32768, I32))

    kf = jnp.float32(topk)

    def select():
        t_hi = bisect16(hi_scr, kf)
        t_hi16 = t_hi.astype(I16)
        need_lo = kf - count16(hi_scr, lambda a: a > t_hi16)

        def build_lo(c, _):
            kk = key_scr[c]
            in_bucket = lax.shift_right_arithmetic(kk, 16) == t_hi
            lo = (kk & jnp.int32(0xFFFF)) - 32768
            lo_scr[c] = jnp.where(in_bucket, lo, -32768).astype(I16)
            return 0
        lax.fori_loop(0, n_chunks, build_lo, 0)
        t_lo = bisect16(lo_scr, need_lo)
        thr = lax.shift_left(t_hi, 16) | (t_lo + 32768)
        thr = jnp.maximum(thr, int_min + 1)
        c_ge = count(lambda kk, idx: kk >= thr)
        excess = c_ge > kf

        def tie_break():
            need = kf - count(lambda kk, idx: kk > thr)

            def idx_step(it, cut):
                cand = cut + lax.shift_left(jnp.int32(1), idx_bits - 1 - it)
                c = count(lambda kk, idx: (kk == thr) & (idx < cand))
                return jnp.where(c < need, cand, cut)
            cut = lax.fori_loop(0, idx_bits, idx_step, jnp.zeros((tq, 1), I32))
            cut = jnp.where(excess, cut, jnp.int32(2 ** 30))

            def demote(c, _):
                kk = key_scr[c]
                key_scr[c] = jnp.where((kk == thr) & (c * tk + lane_k > cut), thr - 1, kk)
                return 0
            lax.fori_loop(0, n_chunks, demote, 0)

        any_excess = jnp.max(jnp.where(excess, 1.0, 0.0)) > 0.0
        lax.cond(any_excess, tie_break, lambda: None)
        return thr

    thr = lax.cond(q_end > topk, select, lambda: jnp.full((tq, 1), int_min + 1, I32))

    dq = dq_ref[...].astype(F32)
    qm = []
    for h in range(DSA_HEADS):
        sub = dq[:, (h // 2) * LANE:(h // 2 + 1) * LANE]
        lo = (h % 2) * DSA_HEAD_DIM
        in_head = (lane_q >= lo) & (lane_q < lo + DSA_HEAD_DIM)
        qm.append(jnp.where(in_head, sub, 0.0).astype(BF16))
    qm_stack = jnp.concatenate(qm, axis=0)
    heads = range(DSA_HEADS)

    m_scr[...] = jnp.full(m_scr.shape, NEG, F32)
    l_scr[...] = jnp.zeros(l_scr.shape, F32)
    acc_scr[...] = jnp.zeros(acc_scr.shape, F32)

    def phase1(c, carry):
        off = pl.multiple_of(c * tk, tk)
        sel = key_scr[c] >= thr
        s = _dot_nt(qm_stack, dkk_ref[pl.ds(off, tk), :])
        for h in heads:
            sm = jnp.where(sel, s[h * tq:(h + 1) * tq], NEG)
            s_scr[h, c] = sm
            m_scr[h] = jnp.maximum(m_scr[h], _fold_lanes(sm, jnp.maximum))
        return carry

    lax.fori_loop(0, n_chunks, phase1, 0)
    for h in heads:
        m_scr[h] = jnp.broadcast_to(jnp.max(m_scr[h], axis=1, keepdims=True), (tq, LANE))

    def phase2(c, carry):
        off = pl.multiple_of(c * tk, tk)
        ps = []
        for h in heads:
            p = jnp.exp2(s_scr[h, c] - jnp.concatenate([m_scr[h]] * (tk // LANE), axis=1))
            l_scr[h] = l_scr[h] + _fold_lanes(p, jnp.add)
            ps.append(p.astype(BF16))
        pv = _dot(jnp.concatenate(ps, axis=0), dvv_ref[pl.ds(off, tk), :])
        for h in heads:
            acc_scr[h] = acc_scr[h] + pv[h * tq:(h + 1) * tq]
        return carry

    lax.fori_loop(0, n_chunks, phase2, 0)
    outs = [acc_scr[h] / jnp.sum(l_scr[h], axis=1, keepdims=True) for h in heads]
    lo_half = lane_q < DSA_HEAD_DIM
    o_ref[...] = jnp.concatenate([jnp.where(lo_half, outs[0], outs[1]),
                                  jnp.where(lo_half, outs[2], outs[3])], axis=1).astype(BF16)


def _dsa(iq, iw, ikr, dq, dkk, dvv, nb, tq, tk):
    t = iq.shape[0]
    s = t // nb
    nq = s // tq
    topk = min(DSA_TOPK_MAX, s // 4)
    idx_bits = max(1, int(np.ceil(np.log2(s))))

    def qspec(w):
        return pl.BlockSpec((tq, w), lambda b, i: (b * nq + i, 0))

    def kspec(w):
        return pl.BlockSpec((s, w), lambda b, i: (b, 0))

    return pl.pallas_call(
        functools.partial(_dsa_kernel, tq=tq, tk=tk, topk=topk, idx_bits=idx_bits),
        grid=(nb, nq),
        in_specs=[qspec(256), qspec(LANE), kspec(LANE), qspec(256), kspec(LANE), kspec(LANE)],
        out_specs=qspec(256),
        out_shape=jax.ShapeDtypeStruct((t, DSA_HEADS * DSA_HEAD_DIM), BF16),
        scratch_shapes=[pltpu.VMEM((s // tk, tq, tk), I32),
                        pltpu.VMEM((s // tk, tq, tk), I16),
                        pltpu.VMEM((s // tk, tq, tk), I16),
                        pltpu.VMEM((IDX_HEADS * tq, tk), F32),
                        pltpu.VMEM((DSA_HEADS, s // tk, tq, tk), F32)] + [pltpu.VMEM((DSA_HEADS, tq, LANE), F32)] * 3,
        compiler_params=_cparams(("arbitrary", "arbitrary")),
        name="dsa",
    )(iq, iw, ikr, dq, dkk, dvv)


def _gla_kernel(gq_ref, gk_ref, gv_ref, gg_ref, gr_ref, ng_ref, segv_ref, seg64_ref, bmask_ref, o_ref,
                st_scr, kbuf, cbuf, vbuf, *, tg):
    j = pl.program_id(1)
    ch = GLA_CHUNK
    rowc = lax.broadcasted_iota(I32, (tg, 1), 0) % ch

    @pl.when(j == 0)
    def _():
        st_scr[...] = jnp.zeros_like(st_scr)
        kbuf[0:ch, :] = jnp.zeros((ch, LANE), F32)
        cbuf[0:ch, :] = jnp.zeros((ch, LANE), F32)
        vbuf[0:ch, :] = jnp.zeros((ch, 2 * LANE), F32)

    cbuf[ch:ch + tg, :] = gg_ref[...]
    sh = 1
    while sh < ch:
        cur = cbuf[ch:ch + tg, :]
        prev = cbuf[ch - sh:ch - sh + tg, :]
        cbuf[ch:ch + tg, :] = cur + jnp.where(rowc >= sh, prev, 0.0)
        sh *= 2
    cum = cbuf[ch:ch + tg, :]
    q = gq_ref[...]
    k = gk_ref[...]
    v = gv_ref[...]
    kbuf[ch:ch + tg, :] = k
    vbuf[ch:ch + tg, :] = v

    qd = (q * jnp.exp(cum)).astype(BF16)
    bmask = bmask_ref[...]
    inter = []
    for c in range(tg // ch):
        rs = slice(c * ch, (c + 1) * ch)
        st = st_scr[...]
        inter.append(_dot_nt(qd[rs], st.astype(BF16)))
        last = cum[c * ch + ch - 1:c * ch + ch, :]
        kp = k[rs] * jnp.exp(last - cum[rs])
        upd = lax.dot_general(v[rs].astype(BF16), kp.astype(BF16), (((0,), (0,)), ((), ())),
                              preferred_element_type=F32)
        st_scr[...] = jnp.exp(last) * st + upd * bmask
    o = jnp.concatenate(inter, axis=0)

    segv = segv_ref[...]
    for d in range(ch):
        kd = kbuf[ch - d:ch - d + tg, :]
        cd = cbuf[ch - d:ch - d + tg, :]
        vd = vbuf[ch - d:ch - d + tg, :]
        ok = rowc >= d
        t = jnp.where(ok, q * kd * jnp.exp(jnp.where(ok, cum - cd, 0.0)), 0.0)
        o = o + _dot(t.astype(BF16), segv) * vd

    sq = o * o
    hi = sq.astype(BF16)
    lo = (sq - hi.astype(F32)).astype(BF16)
    var = _dot(hi, seg64_ref[...]) + _dot(lo, seg64_ref[...])
    o_ref[...] = (gr_ref[...] * (o * lax.rsqrt(var + EPS) * ng_ref[...])).astype(BF16)


def _gla(gq, gk, gv, gg, gr, norm_g, nb, tg):
    t = gq.shape[0]
    nj = t // nb // tg
    ng = jnp.tile(norm_g.reshape(1, GLA_DV), (1, GLA_HEADS)).astype(F32)
    hk = np.arange(GLA_HEADS * GLA_DK) // GLA_DK
    hv = np.arange(GLA_HEADS * GLA_DV) // GLA_DV
    same = (hk[:, None] == hv[None, :])
    segv = jnp.asarray(same, BF16)
    bmask = jnp.asarray(same.T, F32)
    seg64 = jnp.asarray((hv[:, None] == hv[None, :]) / GLA_DV, BF16)

    def tok(w):
        return pl.BlockSpec((tg, w), lambda b, j: (b * nj + j, 0))

    def full(a):
        return pl.BlockSpec(a.shape, lambda b, j: (0,) * a.ndim, pipeline_mode=pl.Buffered(1))

    return pl.pallas_call(
        functools.partial(_gla_kernel, tg=tg),
        grid=(nb, nj),
        in_specs=[tok(LANE), tok(LANE), tok(256), tok(LANE), tok(256), full(ng), full(segv), full(seg64), full(bmask)],
        out_specs=tok(256),
        out_shape=jax.ShapeDtypeStruct((t, GLA_HEADS * GLA_DV), BF16),
        scratch_shapes=[pltpu.VMEM((2 * LANE, LANE), F32),
                        pltpu.VMEM((GLA_CHUNK + tg, LANE), F32),
                        pltpu.VMEM((GLA_CHUNK + tg, LANE), F32),
                        pltpu.VMEM((GLA_CHUNK + tg, 2 * LANE), F32)],
        compiler_params=_cparams(("arbitrary", "arbitrary")),
        name="gla",
    )(gq, gk, gv, gg, gr, ng, segv, seg64, bmask)


def _merge_kernel(x_ref, g_ref, wg_ref, o0_ref, o1_ref, o2_ref, o3_ref, wb_ref, wo_ref, out_ref):
    x = x_ref[...]
    h = _rms(x, g_ref[...]).astype(BF16)
    merged = None
    for n, o_ref in enumerate((o0_ref, o1_ref, o2_ref, o3_ref)):
        gate = _dot(h, wg_ref[:, n * D_MODEL:(n + 1) * D_MODEL])
        gate = 1.0 / (1.0 + jnp.exp(-gate))
        term = gate * _dot(o_ref[...], wb_ref[n])
        merged = term if merged is None else merged + term
    out_ref[...] = x + _dot(merged.astype(BF16), wo_ref[...])


def _merge(x2, g, wg, branches, wb, wo, tm):
    t, d = x2.shape

    def tok(w):
        return pl.BlockSpec((tm, w), lambda i: (i, 0))

    def full(a):
        return pl.BlockSpec(a.shape, lambda i: (0,) * a.ndim, pipeline_mode=pl.Buffered(1))

    return pl.pallas_call(
        _merge_kernel,
        grid=(t // tm,),
        in_specs=[tok(d), full(g), full(wg)] + [tok(BRANCH_WIDTH)] * N_BRANCH + [full(wb), full(wo)],
        out_specs=tok(d),
        out_shape=jax.ShapeDtypeStruct((t, d), F32),
        compiler_params=_cparams(("arbitrary",)),
        name="merge",
    )(x2, g, wg, *branches, wb, wo)


def _ffn_kernel(x_ref, g_ref, wgu_ref, wd_ref, fg_ref, out_ref, *, fc, final):
    x = x_ref[...]
    h = _rms(x, g_ref[...]).astype(BF16)
    acc = x
    for c in range(D_FF // fc):
        gate = _dot(h, wgu_ref[0, :, c * fc:(c + 1) * fc])
        up = _dot(h, wgu_ref[1, :, c * fc:(c + 1) * fc])
        act = (gate / (1.0 + jnp.exp(-gate)) * up).astype(BF16)
        acc = acc + _dot(act, wd_ref[c * fc:(c + 1) * fc, :])
    if final:
        acc = _rms(acc, fg_ref[...])
    out_ref[...] = acc


def _ffn(x2, g, wgu, wd, fg, tm, final):
    t, d = x2.shape

    def tok(w):
        return pl.BlockSpec((tm, w), lambda i: (i, 0))

    def full(a):
        return pl.BlockSpec(a.shape, lambda i: (0,) * a.ndim, pipeline_mode=pl.Buffered(1))

    return pl.pallas_call(
        functools.partial(_ffn_kernel, fc=256, final=final),
        grid=(t // tm,),
        in_specs=[tok(d), full(g), full(wgu), full(wd), full(fg)],
        out_specs=tok(d),
        out_shape=jax.ShapeDtypeStruct((t, d), F32),
        compiler_params=_cparams(("arbitrary",)),
        name="ffn",
    )(x2, g, wgu, wd, fg)


def _tile(n, pref):
    while n % pref:
        pref //= 2
    return pref


def kernel(x, positions, attn_norm_g, w_in, mla_q_norm_g, mla_w_q_up, mla_kv_norm_g, mla_w_kv_up, conv_w, gla_w_gate_up, gla_b_gate, gla_norm_g, w_branch, w_out, ffn_norm_g, w_ffn_gate, w_ffn_up, w_ffn_down, final_norm_g):
    b, s, d = x.shape
    depth = w_in.shape[0]
    t = b * s
    assert d == D_MODEL and s % GLA_CHUNK == 0
    tm_prep = _tile(s, 256)
    tm_wide = _tile(t, 512)
    tq_mla = _tile(s, 256)
    tk_att = _tile(s, 512)
    tq_dsa = _tile(s, 256)
    tg = _tile(s, 256)

    x2 = x.reshape(t, d)
    ctab, stab = _rope_tables(positions, _tile(t, 512))
    row = lambda a: a.reshape(1, -1).astype(F32)

    for l in range(depth):
        w_small = _pack_small_weight(w_in[l])
        wq, wqs, wk, wv = _pack_mla_weights(mla_w_q_up[l], mla_w_kv_up[l])
        wgate = jnp.concatenate([gla_w_gate_up[l], jnp.zeros((LANE - GLA_GATE_RANK, GLA_HEADS * GLA_DK), F32)],
                                axis=0).astype(BF16)
        (mq, mk, mv, o_conv, dq, dkk, dvv, iq, ikr, iw, gq, gk, gv, gg, gr) = _prep(
            x2, b, row(attn_norm_g[l]), w_small, row(mla_q_norm_g[l]), wq, wqs, row(mla_kv_norm_g[l]), wk, wv,
            ctab, stab, conv_w[l].astype(F32), wgate, row(gla_b_gate[l]), tm_prep)
        o_mla = _mla(mq, mk, mv, b, tq_mla, tk_att)
        o_dsa = _dsa(iq, iw, ikr, dq, dkk, dvv, b, tq_dsa, tk_att)
        o_gla = _gla(gq, gk, gv, gg, gr, gla_norm_g[l], b, tg)
        x2 = _merge(x2, row(attn_norm_g[l]), w_in[l][:, O_GATE:].astype(BF16), (o_mla, o_conv, o_dsa, o_gla),
                    w_branch[l].astype(BF16), w_out[l].astype(BF16), tm_wide)
        wgu = jnp.stack([w_ffn_gate[l], w_ffn_up[l]]).astype(BF16)
        x2 = _ffn(x2, row(ffn_norm_g[l]), wgu, w_ffn_down[l].astype(BF16), row(final_norm_g), tm_wide,
                  final=(l == depth - 1))
    return x2.reshape(b, s, d)
```

```python
import functools

import jax
import jax.numpy as jnp
import numpy as np
from jax import lax
from jax.experimental import pallas as pl
from jax.experimental.pallas import tpu as pltpu

F32 = jnp.float32
BF16 = jnp.bfloat16
I32 = jnp.int32
I16 = jnp.int16

D_MODEL = 1024
MLA_HEADS = 4
MLA_Q_RANK = 256
MLA_KV_RANK = 128
MLA_NOPE = 64
MLA_ROPE = 32
MLA_V = 64
ROPE_THETA = 10000.0
CONV_WIDTH = 256
CONV_K = 3
DSA_HEADS = 4
DSA_HEAD_DIM = 64
IDX_HEADS = 8
IDX_DIM = 32
DSA_TOPK_MAX = 256
GLA_HEADS = 4
GLA_DK = 32
GLA_DV = 64
GLA_GATE_RANK = 16
GLA_TAU = 16.0
GLA_CHUNK = 64
N_BRANCH = 4
BRANCH_WIDTH = 256
D_FF = ((8 * D_MODEL + 3 * 256 - 1) // (3 * 256)) * 256
EPS = 1e-6
NEG = -1e30
LOG2E = 1.4426950408889634

LANE = 128
HEAD_PAD = 128

IN_SIZES = (
    MLA_Q_RANK, MLA_KV_RANK, MLA_ROPE,
    CONV_WIDTH, CONV_WIDTH, CONV_WIDTH,
    DSA_HEADS * DSA_HEAD_DIM, DSA_HEAD_DIM, DSA_HEAD_DIM,
    IDX_HEADS * IDX_DIM, IDX_DIM, IDX_HEADS,
    GLA_HEADS * GLA_DK, GLA_HEADS * GLA_DK, GLA_HEADS * GLA_DV,
    GLA_GATE_RANK, GLA_HEADS * GLA_DV,
    N_BRANCH * D_MODEL,
)
_OFF = np.concatenate([[0], np.cumsum(IN_SIZES)]).astype(int)
(O_CQ, O_CKV, O_KR, O_CB, O_CC, O_CX, O_DQ, O_DK, O_DV, O_IQ, O_IK, O_IW,
 O_GQ, O_GK, O_GV, O_GLR, O_GR, O_GATE, O_END) = [int(v) for v in _OFF]

P_CQ, P_CKV, P_KRP, P_KRS = 0, 256, 384, 512
P_CB, P_CC, P_CX = 640, 896, 1152
P_DQ, P_DKK, P_DVV = 1408, 1664, 1792
P_IQ, P_IKR, P_IW = 1920, 2176, 2304
P_GQ, P_GK, P_GV, P_GLR, P_GR = 2432, 2560, 2688, 2944, 3072
P_TOTAL = 3328

VMEM_LIMIT = 56 * 1024 * 1024


def _cparams(sem):
    return pltpu.CompilerParams(dimension_semantics=sem, vmem_limit_bytes=VMEM_LIMIT)


def _rms(x, g):
    return x * lax.rsqrt(jnp.mean(x * x, axis=-1, keepdims=True) + EPS) * g


def _dot(a, b):
    return jnp.dot(a, b, preferred_element_type=F32)


def _dot_nt(a, b):
    return lax.dot_general(a, b, (((1,), (1,)), ((), ())), preferred_element_type=F32)


def _rope_kernel(pos_ref, pat_ref, c_ref, s_ref):
    pos = pos_ref[...].astype(F32)
    ang = pos * pat_ref[0:1, :]
    c_ref[...] = pat_ref[1:2, :] + pat_ref[2:3, :] * jnp.cos(ang)
    s_ref[...] = pat_ref[3:4, :] * jnp.sin(ang)


def _rope_tables(positions, tm):
    t = positions.size
    half = MLA_ROPE // 2
    inv_freq = ROPE_THETA ** (-jnp.arange(half, dtype=F32) / half)
    z64, z32 = jnp.zeros((MLA_NOPE,), F32), jnp.zeros((HEAD_PAD - MLA_NOPE - MLA_ROPE,), F32)
    o16 = jnp.ones((half,), F32)
    pat = jnp.stack([
        jnp.concatenate([z64, inv_freq, inv_freq, z32]),
        jnp.concatenate([z64 + 1.0, 0 * o16, 0 * o16, z32]),
        jnp.concatenate([z64, o16, o16, z32]),
        jnp.concatenate([z64, -o16, o16, z32]),
    ])
    pat = jnp.concatenate([pat, jnp.zeros((4, HEAD_PAD), F32)], axis=0)
    return pl.pallas_call(
        _rope_kernel,
        grid=(t // tm,),
        in_specs=[pl.BlockSpec((tm, 1), lambda i: (i, 0)), pl.BlockSpec((8, HEAD_PAD), lambda i: (0, 0))],
        out_specs=[pl.BlockSpec((tm, HEAD_PAD), lambda i: (i, 0))] * 2,
        out_shape=[jax.ShapeDtypeStruct((t, HEAD_PAD), F32)] * 2,
        compiler_params=_cparams(("arbitrary",)),
        name="rope_tables",
    )(positions.reshape(t, 1), pat)


def _prep_kernel(x_ref, g_ref, w_ref, qg_ref, wq_ref, wqs_ref, kvg_ref, wk_ref, wv_ref, c_ref, s_ref,
                 convw_ref, wgate_ref, bgate_ref,
                 mq_ref, mk_ref, mv_ref, oconv_ref, dq_ref, dkk_ref, dvv_ref, iq_ref, ikr_ref, iw_ref,
                 gq_ref, gk_ref, gv_ref, gg_ref, gr_ref, u_scr, *, tm):
    j = pl.program_id(1)
    h = _rms(x_ref[...], g_ref[...]).astype(BF16)
    p = _dot(h, w_ref[...])

    c1, s1 = c_ref[...], s_ref[...]
    c4 = jnp.concatenate([c1] * MLA_HEADS, axis=1)
    s4 = jnp.concatenate([s1] * MLA_HEADS, axis=1)

    qn = _rms(p[:, P_CQ:P_CQ + MLA_Q_RANK], qg_ref[...]).astype(BF16)
    q = _dot(qn, wq_ref[...]) * c4 + _dot(qn, wqs_ref[...]) * s4
    mq_ref[...] = (q * ((MLA_NOPE + MLA_ROPE) ** -0.5 * LOG2E)).astype(BF16)
    kvn = _rms(p[:, P_CKV:P_CKV + MLA_KV_RANK], kvg_ref[...]).astype(BF16)
    kr = p[:, P_KRP:P_KRP + HEAD_PAD] * c1 + p[:, P_KRS:P_KRS + HEAD_PAD] * s1
    mk_ref[...] = (_dot(kvn, wk_ref[...]) + jnp.concatenate([kr] * MLA_HEADS, axis=1)).astype(BF16)
    mv_ref[...] = _dot(kvn, wv_ref[...]).astype(BF16)

    u = p[:, P_CC:P_CC + CONV_WIDTH] * p[:, P_CX:P_CX + CONV_WIDTH]

    @pl.when(j == 0)
    def _():
        u_scr[0:8, :] = jnp.zeros((8, CONV_WIDTH), F32)

    u_scr[8:8 + tm, :] = u
    cw = convw_ref[...]
    y = cw[0:1, :] * u_scr[6:6 + tm, :] + cw[1:2, :] * u_scr[7:7 + tm, :] + cw[2:3, :] * u
    oconv_ref[...] = (p[:, P_CB:P_CB + CONV_WIDTH] * y).astype(BF16)
    u_scr[0:8, :] = u_scr[tm:tm + 8, :]

    dq_ref[...] = (p[:, P_DQ:P_DQ + 256] * (DSA_HEAD_DIM ** -0.5 * LOG2E)).astype(BF16)
    dkk_ref[...] = p[:, P_DKK:P_DKK + LANE].astype(BF16)
    dvv_ref[...] = p[:, P_DVV:P_DVV + LANE].astype(BF16)
    iq_ref[...] = p[:, P_IQ:P_IQ + 256].astype(BF16)
    ikr_ref[...] = p[:, P_IKR:P_IKR + LANE].astype(BF16)
    iw_ref[...] = p[:, P_IW:P_IW + LANE] * ((IDX_DIM ** -0.5) * (IDX_HEADS ** -0.5))

    gq_ref[...] = p[:, P_GQ:P_GQ + LANE] * (GLA_DK ** -0.5)
    gk_ref[...] = p[:, P_GK:P_GK + LANE]
    gv_ref[...] = p[:, P_GV:P_GV + 256]
    gate = _dot(p[:, P_GLR:P_GLR + LANE].astype(BF16), wgate_ref[...]) + bgate_ref[...]
    log_sig = jnp.minimum(gate, 0.0) - jnp.log1p(jnp.exp(-jnp.abs(gate)))
    gg_ref[...] = log_sig * (1.0 / GLA_TAU)
    gr = p[:, P_GR:P_GR + 256]
    gr_ref[...] = gr / (1.0 + jnp.exp(-gr))


def _pack_small_weight(w_in):
    d = w_in.shape[0]

    def cols(a, b):
        return w_in[:, a:b]

    def zeros(n):
        return jnp.zeros((d, n), w_in.dtype)

    half = MLA_ROPE // 2
    kr1, kr2 = cols(O_KR, O_KR + half), cols(O_KR + half, O_KR + MLA_ROPE)
    pad_tail = HEAD_PAD - MLA_NOPE - MLA_ROPE
    pieces = [
        cols(O_CQ, O_CKV), cols(O_CKV, O_KR),
        zeros(MLA_NOPE), kr1, kr2, zeros(pad_tail),
        zeros(MLA_NOPE), kr2, kr1, zeros(pad_tail),
        cols(O_CB, O_CC), cols(O_CC, O_CX), cols(O_CX, O_DQ),
        cols(O_DQ, O_DK), cols(O_DK, O_DV), cols(O_DK, O_DV), cols(O_DV, O_IQ), cols(O_DV, O_IQ),
        cols(O_IQ, O_IK), cols(O_IK, O_IW), cols(O_IK, O_IW), cols(O_IK, O_IW), cols(O_IK, O_IW),
        cols(O_IW, O_GQ), zeros(LANE - IDX_HEADS),
        cols(O_GQ, O_GK), cols(O_GK, O_GV), cols(O_GV, O_GLR),
        cols(O_GLR, O_GR), zeros(LANE - GLA_GATE_RANK),
        cols(O_GR, O_GATE),
    ]
    out = jnp.concatenate(pieces, axis=1)
    assert out.shape[1] == P_TOTAL, out.shape
    return out.astype(BF16)


def _pack_mla_weights(w_q_up, w_kv_up):
    half = MLA_ROPE // 2
    rq = w_q_up.shape[0]
    rk = w_kv_up.shape[0]
    wq = w_q_up.reshape(rq, MLA_HEADS, MLA_NOPE + MLA_ROPE)
    nope, x1, x2 = wq[..., :MLA_NOPE], wq[..., MLA_NOPE:MLA_NOPE + half], wq[..., MLA_NOPE + half:]
    zq = jnp.zeros((rq, MLA_HEADS, HEAD_PAD - MLA_NOPE - MLA_ROPE), w_q_up.dtype)
    wq_pad = jnp.concatenate([nope, x1, x2, zq], axis=-1).reshape(rq, MLA_HEADS * HEAD_PAD)
    wq_swap = jnp.concatenate([jnp.zeros_like(nope), x2, x1, zq], axis=-1).reshape(rq, MLA_HEADS * HEAD_PAD)
    wkv = w_kv_up.reshape(rk, MLA_HEADS, MLA_NOPE + MLA_V)
    zk = jnp.zeros((rk, MLA_HEADS, HEAD_PAD - MLA_NOPE), w_kv_up.dtype)
    wk_pad = jnp.concatenate([wkv[..., :MLA_NOPE], zk], axis=-1).reshape(rk, MLA_HEADS * HEAD_PAD)
    wv = wkv[..., MLA_NOPE:].reshape(rk, MLA_HEADS * MLA_V)
    return wq_pad.astype(BF16), wq_swap.astype(BF16), wk_pad.astype(BF16), wv.astype(BF16)


def _prep(x2, nb, g, w_small, qg, wq, wqs, kvg, wk, wv, ctab, stab, conv_w, wgate, bgate, tm):
    t, d = x2.shape
    nj = t // nb // tm

    def tok(width):
        return pl.BlockSpec((tm, width), lambda b, j: (b * nj + j, 0))

    def full(a):
        return pl.BlockSpec(a.shape, lambda b, j: (0,) * a.ndim, pipeline_mode=pl.Buffered(1))

    outs = [(512, BF16), (512, BF16), (256, BF16), (256, BF16), (256, BF16), (LANE, BF16), (LANE, BF16),
            (256, BF16), (LANE, BF16), (LANE, F32), (LANE, F32), (LANE, F32), (256, F32), (LANE, F32), (256, F32)]
    return pl.pallas_call(
        functools.partial(_prep_kernel, tm=tm),
        grid=(nb, nj),
        in_specs=[tok(d), full(g), full(w_small), full(qg), full(wq), full(wqs), full(kvg), full(wk), full(wv),
                  tok(HEAD_PAD), tok(HEAD_PAD), full(conv_w), full(wgate), full(bgate)],
        out_specs=[tok(w) for w, _ in outs],
        out_shape=[jax.ShapeDtypeStruct((t, w), dt) for w, dt in outs],
        scratch_shapes=[pltpu.VMEM((tm + 8, CONV_WIDTH), F32)],
        compiler_params=_cparams(("arbitrary", "arbitrary")),
        name="prep",
    )(x2, g, w_small, qg, wq, wqs, kvg, wk, wv, ctab, stab, conv_w, wgate, bgate)


def _fold_lanes(x, op):
    out = x[:, 0:LANE]
    for g in range(1, x.shape[1] // LANE):
        out = op(out, x[:, g * LANE:(g + 1) * LANE])
    return out


def _mla_kernel(q_ref, k_ref, v_ref, o_ref, s_scr, m_scr, l_scr, acc_scr, *, tq, tk):
    i = pl.program_id(1)
    n_chunks = ((i + 1) * tq + tk - 1) // tk
    row = i * tq + lax.broadcasted_iota(I32, (tq, tk), 0)
    col = lax.broadcasted_iota(I32, (tq, tk), 1)
    lane = lax.broadcasted_iota(I32, (tq, LANE), 1)
    heads = range(MLA_HEADS)
    hs = [slice(h * HEAD_PAD, (h + 1) * HEAD_PAD) for h in heads]
    vs = [slice((h // 2) * LANE, (h // 2 + 1) * LANE) for h in heads]

    m_scr[...] = jnp.full(m_scr.shape, NEG, F32)
    l_scr[...] = jnp.zeros(l_scr.shape, F32)
    acc_scr[...] = jnp.zeros(acc_scr.shape, F32)

    def phase1(c, carry, masked):
        off = pl.multiple_of(c * tk, tk)
        for h in heads:
            s = _dot_nt(q_ref[:, hs[h]], k_ref[pl.ds(off, tk), hs[h]])
            if masked:
                s = jnp.where(c * tk + col <= row, s, NEG)
            s_scr[h, c] = s
            m_scr[h] = jnp.maximum(m_scr[h], _fold_lanes(s, jnp.maximum))
        return carry

    lax.fori_loop(0, n_chunks - 1, functools.partial(phase1, masked=False), 0)
    phase1(n_chunks - 1, 0, True)
    for h in heads:
        m_scr[h] = jnp.broadcast_to(jnp.max(m_scr[h], axis=1, keepdims=True), (tq, LANE))

    def phase2(c, carry):
        off = pl.multiple_of(c * tk, tk)
        for h in heads:
            p = jnp.exp2(s_scr[h, c] - jnp.concatenate([m_scr[h]] * (tk // LANE), axis=1))
            l_scr[h] = l_scr[h] + _fold_lanes(p, jnp.add)
            acc_scr[h] = acc_scr[h] + _dot(p.astype(BF16), v_ref[pl.ds(off, tk), vs[h]])
        return carry

    lax.fori_loop(0, n_chunks, phase2, 0)
    res = [acc_scr[h] / jnp.sum(l_scr[h], axis=1, keepdims=True) for h in heads]
    for pair in range(MLA_HEADS // 2):
        o_ref[:, pair * LANE:(pair + 1) * LANE] = jnp.where(lane < MLA_V, res[2 * pair], res[2 * pair + 1]).astype(BF16)


def _mla(mq, mk, mv, nb, tq, tk):
    t = mq.shape[0]
    s = t // nb
    nq = s // tq
    return pl.pallas_call(
        functools.partial(_mla_kernel, tq=tq, tk=tk),
        grid=(nb, nq),
        in_specs=[pl.BlockSpec((tq, MLA_HEADS * HEAD_PAD), lambda b, i: (b * nq + i, 0)),
                  pl.BlockSpec((s, MLA_HEADS * HEAD_PAD), lambda b, i: (b, 0)),
                  pl.BlockSpec((s, MLA_HEADS * MLA_V), lambda b, i: (b, 0))],
        out_specs=pl.BlockSpec((tq, MLA_HEADS * MLA_V), lambda b, i: (b * nq + i, 0)),
        out_shape=jax.ShapeDtypeStruct((t, MLA_HEADS * MLA_V), BF16),
        scratch_shapes=[pltpu.VMEM((MLA_HEADS, s // tk, tq, tk), F32)] + [pltpu.VMEM((MLA_HEADS, tq, LANE), F32)] * 3,
        compiler_params=_cparams(("arbitrary", "arbitrary")),
        name="mla_attn",
    )(mq, mk, mv)


def _dsa_kernel(iq_ref, iw_ref, ikr_ref, dq_ref, dkk_ref, dvv_ref, tri_ref, o_ref,
                key_scr, hi_scr, lo_scr, z_scr, s_scr, m_scr, l_scr, acc_scr, *, tq, th, tk, topk):
    i = pl.program_id(1)
    q_end = (i + 1) * tq
    n_chunks = (q_end + tk - 1) // tk
    lane_q = lax.broadcasted_iota(I32, (th, LANE), 1)
    lane_k = lax.broadcasted_iota(I32, (tq, tk), 1)
    int_min = jnp.int32(-2 ** 31)
    rb = 64
    lane_rb = lax.broadcasted_iota(I32, (rb, tk), 1)
    row_rb = lax.broadcasted_iota(I32, (rb, 1), 0)

    for r0 in range(0, tq, th):
        iq = iq_ref[r0:r0 + th, :].astype(F32)
        iw = iw_ref[r0:r0 + th, :]
        q_heads = []
        w_heads = []
        for h in range(IDX_HEADS):
            grp = h // 4
            sub = iq[:, grp * LANE:(grp + 1) * LANE]
            lo = (h % 4) * IDX_DIM
            in_head = (lane_q >= lo) & (lane_q < lo + IDX_DIM)
            q_heads.append(jnp.where(in_head, sub, 0.0).astype(BF16))
            w_heads.append(iw[:, h:h + 1])
        q_stack = jnp.concatenate(q_heads, axis=0)

        def score_chunk(c, _, r0=r0, q_stack=q_stack, w_heads=w_heads):
            off = pl.multiple_of(c * tk, tk)
            z_scr[...] = _dot_nt(q_stack, ikr_ref[pl.ds(off, tk), :])
            for r in range(0, th, rb):
                acc = w_heads[0][r:r + rb] * jnp.maximum(z_scr[r:r + rb, :], 0.0)
                for h in range(1, IDX_HEADS):
                    acc = acc + w_heads[h][r:r + rb] * jnp.maximum(z_scr[h * th + r:h * th + r + rb, :], 0.0)
                bits = pltpu.bitcast(acc, I32)
                key = jnp.where(bits < 0, bits ^ jnp.int32(0x7FFFFFFF), bits)
                key = jnp.where(c * tk + lane_rb <= i * tq + r0 + r + row_rb, key, int_min)
                key_scr[c, r0 + r:r0 + r + rb, :] = key
                hi_scr[c, r0 + r:r0 + r + rb, :] = lax.shift_right_arithmetic(key, 16).astype(I16)
            return 0

        lax.fori_loop(0, n_chunks, score_chunk, 0)

    def count(pred):
        def body(c, acc):
            hit = jnp.where(pred(key_scr[c], c * tk + lane_k), 1.0, 0.0)
            return acc + _fold_lanes(hit, jnp.add)
        acc = lax.fori_loop(0, n_chunks, body, jnp.zeros((tq, LANE), F32))
        return jnp.sum(acc, axis=1, keepdims=True)

    def count16(ref, pred):
        def body(c, acc):
            hit = jnp.where(pred(ref[c]), jnp.int16(1), jnp.int16(0))
            return acc + _fold_lanes(hit, jnp.add)
        acc = lax.fori_loop(0, n_chunks, body, jnp.zeros((tq, LANE), I16))
        return jnp.sum(acc.astype(F32), axis=1, keepdims=True)

    def bisect16(ref, need):
        def bit_step(it, v):
            cand = v + lax.shift_left(jnp.int32(1), 15 - it)
            cand16 = cand.astype(I16)
            c = count16(ref, lambda a: a >= cand16)
            return jnp.where(c >= need, cand, v)
        return lax.fori_loop(0, 16, bit_step, jnp.full((tq, 1), -32768, I32))

    kf = jnp.float32(topk)

    def select():
        t_hi = bisect16(hi_scr, kf)
        t_hi16 = t_hi.astype(I16)
        need_lo = kf - count16(hi_scr, lambda a: a > t_hi16)

        def build_lo(c, _):
            kk = key_scr[c]
            in_bucket = lax.shift_right_arithmetic(kk, 16) == t_hi
            lo = (kk & jnp.int32(0xFFFF)) - 32768
            lo_scr[c] = jnp.where(in_bucket, lo, -32768).astype(I16)
            return 0
        lax.fori_loop(0, n_chunks, build_lo, 0)
        t_lo = bisect16(lo_scr, need_lo)
        thr = lax.shift_left(t_hi, 16) | (t_lo + 32768)
        thr = jnp.maximum(thr, int_min + 1)
        c_ge = count(lambda kk, idx: kk >= thr)
        excess = c_ge > kf

        def tie_break():
            need = kf - count(lambda kk, idx: kk > thr)

            def demote(c, seen):
                kk = key_scr[c]
                tie = kk == thr
                rank = seen + _dot(jnp.where(tie, 1.0, 0.0).astype(BF16), tri_ref[...])
                key_scr[c] = jnp.where(tie & (rank > need), thr - 1, kk)
                return rank[:, tk - 1:tk]
            lax.fori_loop(0, n_chunks, demote, jnp.zeros((tq, 1), F32))

        any_excess = jnp.max(jnp.where(excess, 1.0, 0.0)) > 0.0
        lax.cond(any_excess, tie_break, lambda: None)
        return thr

    thr = lax.cond(q_end > topk, select, lambda: jnp.full((tq, 1), int_min + 1, I32))

    heads = range(DSA_HEADS)
    for r0 in range(0, tq, th):
        rows = slice(r0, r0 + th)
        thr_h = thr[rows]
        dq = dq_ref[rows, :].astype(F32)
        qm = []
        for h in heads:
            sub = dq[:, (h // 2) * LANE:(h // 2 + 1) * LANE]
            lo = (h % 2) * DSA_HEAD_DIM
            in_head = (lane_q >= lo) & (lane_q < lo + DSA_HEAD_DIM)
            qm.append(jnp.where(in_head, sub, 0.0).astype(BF16))
        qm_stack = jnp.concatenate(qm, axis=0)

        m_scr[...] = jnp.full(m_scr.shape, NEG, F32)
        l_scr[...] = jnp.zeros(l_scr.shape, F32)
        acc_scr[...] = jnp.zeros(acc_scr.shape, F32)

        def phase1(c, carry, rows=rows, thr_h=thr_h, qm_stack=qm_stack):
            off = pl.multiple_of(c * tk, tk)
            sel = key_scr[c, rows, :] >= thr_h
            s = _dot_nt(qm_stack, dkk_ref[pl.ds(off, tk), :])
            for h in heads:
                sm = jnp.where(sel, s[h * th:(h + 1) * th], NEG)
                s_scr[h, c] = sm
                m_scr[h] = jnp.maximum(m_scr[h], _fold_lanes(sm, jnp.maximum))
            return carry

        lax.fori_loop(0, n_chunks, phase1, 0)
        for h in heads:
            m_scr[h] = jnp.broadcast_to(jnp.max(m_scr[h], axis=1, keepdims=True), (th, LANE))

        def phase2(c, carry):
            off = pl.multiple_of(c * tk, tk)
            ps = []
            for h in heads:
                p = jnp.exp2(s_scr[h, c] - jnp.concatenate([m_scr[h]] * (tk // LANE), axis=1))
                l_scr[h] = l_scr[h] + _fold_lanes(p, jnp.add)
                ps.append(p.astype(BF16))
            pv = _dot(jnp.concatenate(ps, axis=0), dvv_ref[pl.ds(off, tk), :])
            for h in heads:
                acc_scr[h] = acc_scr[h] + pv[h * th:(h + 1) * th]
            return carry

        lax.fori_loop(0, n_chunks, phase2, 0)
        outs = [acc_scr[h] / jnp.sum(l_scr[h], axis=1, keepdims=True) for h in heads]
        lo_half = lane_q < DSA_HEAD_DIM
        o_ref[rows, :] = jnp.concatenate([jnp.where(lo_half, outs[0], outs[1]),
                                          jnp.where(lo_half, outs[2], outs[3])], axis=1).astype(BF16)


def _dsa(iq, iw, ikr, dq, dkk, dvv, nb, tq, th, tk):
    t = iq.shape[0]
    s = t // nb
    nq = s // tq
    topk = min(DSA_TOPK_MAX, s // 4)
    tri = jnp.asarray(np.triu(np.ones((tk, tk), np.float32)), BF16)

    def qspec(w):
        return pl.BlockSpec((tq, w), lambda b, i: (b * nq + i, 0))

    def kspec(w):
        return pl.BlockSpec((s, w), lambda b, i: (b, 0))

    return pl.pallas_call(
        functools.partial(_dsa_kernel, tq=tq, th=th, tk=tk, topk=topk),
        grid=(nb, nq),
        in_specs=[qspec(256), qspec(LANE), kspec(LANE), qspec(256), kspec(LANE), kspec(LANE),
                  pl.BlockSpec((tk, tk), lambda b, i: (0, 0), pipeline_mode=pl.Buffered(1))],
        out_specs=qspec(256),
        out_shape=jax.ShapeDtypeStruct((t, DSA_HEADS * DSA_HEAD_DIM), BF16),
        scratch_shapes=[pltpu.VMEM((s // tk, tq, tk), I32),
                        pltpu.VMEM((s // tk, tq, tk), I16),
                        pltpu.VMEM((s // tk, tq, tk), I16),
                        pltpu.VMEM((IDX_HEADS * th, tk), F32),
                        pltpu.VMEM((DSA_HEADS, s // tk, th, tk), F32)] + [pltpu.VMEM((DSA_HEADS, th, LANE), F32)] * 3,
        compiler_params=_cparams(("arbitrary", "arbitrary")),
        name="dsa",
    )(iq, iw, ikr, dq, dkk, dvv, tri)


def _gla_kernel(gq_ref, gk_ref, gv_ref, gg_ref, gr_ref, ng_ref, segv_ref, seg64_ref, bmask_ref, o_ref,
                st_scr, kbuf, cbuf, vbuf, *, tg):
    j = pl.program_id(1)
    ch = GLA_CHUNK
    rowc = lax.broadcasted_iota(I32, (tg, 1), 0) % ch

    @pl.when(j == 0)
    def _():
        st_scr[...] = jnp.zeros_like(st_scr)
        kbuf[0:ch, :] = jnp.zeros((ch, LANE), F32)
        cbuf[0:ch, :] = jnp.zeros((ch, LANE), F32)
        vbuf[0:ch, :] = jnp.zeros((ch, 2 * LANE), F32)

    cbuf[ch:ch + tg, :] = gg_ref[...]
    sh = 1
    while sh < ch:
        cur = cbuf[ch:ch + tg, :]
        prev = cbuf[ch - sh:ch - sh + tg, :]
        cbuf[ch:ch + tg, :] = cur + jnp.where(rowc >= sh, prev, 0.0)
        sh *= 2
    cum = cbuf[ch:ch + tg, :]
    q = gq_ref[...]
    k = gk_ref[...]
    v = gv_ref[...]
    kbuf[ch:ch + tg, :] = k
    vbuf[ch:ch + tg, :] = v

    qd = (q * jnp.exp(cum)).astype(BF16)
    bmask = bmask_ref[...]
    inter = []
    for c in range(tg // ch):
        rs = slice(c * ch, (c + 1) * ch)
        st = st_scr[...]
        inter.append(_dot_nt(qd[rs], st.astype(BF16)))
        last = cum[c * ch + ch - 1:c * ch + ch, :]
        kp = k[rs] * jnp.exp(last - cum[rs])
        upd = lax.dot_general(v[rs].astype(BF16), kp.astype(BF16), (((0,), (0,)), ((), ())),
                              preferred_element_type=F32)
        st_scr[...] = jnp.exp(last) * st + upd * bmask
    o = jnp.concatenate(inter, axis=0)

    segv = segv_ref[...]
    for d in range(ch):
        kd = kbuf[ch - d:ch - d + tg, :]
        cd = cbuf[ch - d:ch - d + tg, :]
        vd = vbuf[ch - d:ch - d + tg, :]
        ok = rowc >= d
        t = jnp.where(ok, q * kd * jnp.exp(jnp.where(ok, cum - cd, 0.0)), 0.0)
        o = o + _dot(t.astype(BF16), segv) * vd

    sq = o * o
    hi = sq.astype(BF16)
    lo = (sq - hi.astype(F32)).astype(BF16)
    var = _dot(hi, seg64_ref[...]) + _dot(lo, seg64_ref[...])
    o_ref[...] = (gr_ref[...] * (o * lax.rsqrt(var + EPS) * ng_ref[...])).astype(BF16)


def _gla(gq, gk, gv, gg, gr, norm_g, nb, tg):
    t = gq.shape[0]
    nj = t // nb // tg
    ng = jnp.tile(norm_g.reshape(1, GLA_DV), (1, GLA_HEADS)).astype(F32)
    hk = np.arange(GLA_HEADS * GLA_DK) // GLA_DK
    hv = np.arange(GLA_HEADS * GLA_DV) // GLA_DV
    same = (hk[:, None] == hv[None, :])
    segv = jnp.asarray(same, BF16)
    bmask = jnp.asarray(same.T, F32)
    seg64 = jnp.asarray((hv[:, None] == hv[None, :]) / GLA_DV, BF16)

    def tok(w):
        return pl.BlockSpec((tg, w), lambda b, j: (b * nj + j, 0))

    def full(a):
        return pl.BlockSpec(a.shape, lambda b, j: (0,) * a.ndim, pipeline_mode=pl.Buffered(1))

    return pl.pallas_call(
        functools.partial(_gla_kernel, tg=tg),
        grid=(nb, nj),
        in_specs=[tok(LANE), tok(LANE), tok(256), tok(LANE), tok(256), full(ng), full(segv), full(seg64), full(bmask)],
        out_specs=tok(256),
        out_shape=jax.ShapeDtypeStruct((t, GLA_HEADS * GLA_DV), BF16),
        scratch_shapes=[pltpu.VMEM((2 * LANE, LANE), F32),
                        pltpu.VMEM((GLA_CHUNK + tg, LANE), F32),
                        pltpu.VMEM((GLA_CHUNK + tg, LANE), F32),
                        pltpu.VMEM((GLA_CHUNK + tg, 2 * LANE), F32)],
        compiler_params=_cparams(("arbitrary", "arbitrary")),
        name="gla",
    )(gq, gk, gv, gg, gr, ng, segv, seg64, bmask)


def _merge_kernel(x_ref, g_ref, wg_ref, o0_ref, o1_ref, o2_ref, o3_ref, wb_ref, wo_ref, out_ref):
    x = x_ref[...]
    h = _rms(x, g_ref[...]).astype(BF16)
    merged = None
    for n, o_ref in enumerate((o0_ref, o1_ref, o2_ref, o3_ref)):
        gate = _dot(h, wg_ref[:, n * D_MODEL:(n + 1) * D_MODEL])
        gate = 1.0 / (1.0 + jnp.exp(-gate))
        term = gate * _dot(o_ref[...], wb_ref[n])
        merged = term if merged is None else merged + term
    out_ref[...] = x + _dot(merged.astype(BF16), wo_ref[...])


def _merge(x2, g, wg, branches, wb, wo, tm):
    t, d = x2.shape

    def tok(w):
        return pl.BlockSpec((tm, w), lambda i: (i, 0))

    def full(a):
        return pl.BlockSpec(a.shape, lambda i: (0,) * a.ndim, pipeline_mode=pl.Buffered(1))

    return pl.pallas_call(
        _merge_kernel,
        grid=(t // tm,),
        in_specs=[tok(d), full(g), full(wg)] + [tok(BRANCH_WIDTH)] * N_BRANCH + [full(wb), full(wo)],
        out_specs=tok(d),
        out_shape=jax.ShapeDtypeStruct((t, d), F32),
        compiler_params=_cparams(("arbitrary",)),
        name="merge",
    )(x2, g, wg, *branches, wb, wo)


def _ffn_kernel(x_ref, g_ref, wgu_ref, wd_ref, fg_ref, out_ref, *, fc, final):
    x = x_ref[...]
    h = _rms(x, g_ref[...]).astype(BF16)
    acc = x
    for c in range(D_FF // fc):
        gate = _dot(h, wgu_ref[0, :, c * fc:(c + 1) * fc])
        up = _dot(h, wgu_ref[1, :, c * fc:(c + 1) * fc])
        act = (gate / (1.0 + jnp.exp(-gate)) * up).astype(BF16)
        acc = acc + _dot(act, wd_ref[c * fc:(c + 1) * fc, :])
    if final:
        acc = _rms(acc, fg_ref[...])
    out_ref[...] = acc


def _ffn(x2, g, wgu, wd, fg, tm, final):
    t, d = x2.shape

    def tok(w):
        return pl.BlockSpec((tm, w), lambda i: (i, 0))

    def full(a):
        return pl.BlockSpec(a.shape, lambda i: (0,) * a.ndim, pipeline_mode=pl.Buffered(1))

    return pl.pallas_call(
        functools.partial(_ffn_kernel, fc=256, final=final),
        grid=(t // tm,),
        in_specs=[tok(d), full(g), full(wgu), full(wd), full(fg)],
        out_specs=tok(d),
        out_shape=jax.ShapeDtypeStruct((t, d), F32),
        compiler_params=_cparams(("arbitrary",)),
        name="ffn",
    )(x2, g, wgu, wd, fg)


def _tile(n, pref):
    while n % pref:
        pref //= 2
    return pref


def kernel(x, positions, attn_norm_g, w_in, mla_q_norm_g, mla_w_q_up, mla_kv_norm_g, mla_w_kv_up, conv_w, gla_w_gate_up, gla_b_gate, gla_norm_g, w_branch, w_out, ffn_norm_g, w_ffn_gate, w_ffn_up, w_ffn_down, final_norm_g):
    b, s, d = x.shape
    depth = w_in.shape[0]
    t = b * s
    assert d == D_MODEL and s % GLA_CHUNK == 0
    tm_prep = _tile(s, 256)
    tm_wide = _tile(t, 512)
    tq_mla = _tile(s, 256)
    tk_att = _tile(s, 512)
    tq_dsa = _tile(s, 512)
    th_dsa = _tile(tq_dsa, 256)
    tg = _tile(s, 256)

    x2 = x.reshape(t, d)
    ctab, stab = _rope_tables(positions, _tile(t, 512))
    row = lambda a: a.reshape(1, -1).astype(F32)

    for l in range(depth):
        w_small = _pack_small_weight(w_in[l])
        wq, wqs, wk, wv = _pack_mla_weights(mla_w_q_up[l], mla_w_kv_up[l])
        wgate = jnp.concatenate([gla_w_gate_up[l], jnp.zeros((LANE - GLA_GATE_RANK, GLA_HEADS * GLA_DK), F32)],
                                axis=0).astype(BF16)
        (mq, mk, mv, o_conv, dq, dkk, dvv, iq, ikr, iw, gq, gk, gv, gg, gr) = _prep(
            x2, b, row(attn_norm_g[l]), w_small, row(mla_q_norm_g[l]), wq, wqs, row(mla_kv_norm_g[l]), wk, wv,
            ctab, stab, conv_w[l].astype(F32), wgate, row(gla_b_gate[l]), tm_prep)
        o_mla = _mla(mq, mk, mv, b, tq_mla, tk_att)
        o_dsa = _dsa(iq, iw, ikr, dq, dkk, dvv, b, tq_dsa, th_dsa, tk_att)
        o_gla = _gla(gq, gk, gv, gg, gr, gla_norm_g[l], b, tg)
        x2 = _merge(x2, row(attn_norm_g[l]), w_in[l][:, O_GATE:].astype(BF16), (o_mla, o_conv, o_dsa, o_gla),
                    w_branch[l].astype(BF16), w_out[l].astype(BF16), tm_wide)
        wgu = jnp.stack([w_ffn_gate[l], w_ffn_up[l]]).astype(BF16)
        x2 = _ffn(x2, row(ffn_norm_g[l]), wgu, w_ffn_down[l].astype(BF16), row(final_norm_g), tm_wide,
                  final=(l == depth - 1))
    return x2.reshape(b, s, d)
```

```python
import functools

import jax
import jax.numpy as jnp
import numpy as np
from jax import lax
from jax.experimental import pallas as pl
from jax.experimental.pallas import tpu as pltpu

F32 = jnp.float32
BF16 = jnp.bfloat16
I32 = jnp.int32
I16 = jnp.int16

D_MODEL = 1024
MLA_HEADS = 4
MLA_Q_RANK = 256
MLA_KV_RANK = 128
MLA_NOPE = 64
MLA_ROPE = 32
MLA_V = 64
ROPE_THETA = 10000.0
CONV_WIDTH = 256
CONV_K = 3
DSA_HEADS = 4
DSA_HEAD_DIM = 64
IDX_HEADS = 8
IDX_DIM = 32
DSA_TOPK_MAX = 256
GLA_HEADS = 4
GLA_DK = 32
GLA_DV = 64
GLA_GATE_RANK = 16
GLA_TAU = 16.0
GLA_CHUNK = 64
N_BRANCH = 4
BRANCH_WIDTH = 256
D_FF = ((8 * D_MODEL + 3 * 256 - 1) // (3 * 256)) * 256
EPS = 1e-6
NEG = -1e30
LOG2E = 1.4426950408889634

LANE = 128
HEAD_PAD = 128

IN_SIZES = (
    MLA_Q_RANK, MLA_KV_RANK, MLA_ROPE,
    CONV_WIDTH, CONV_WIDTH, CONV_WIDTH,
    DSA_HEADS * DSA_HEAD_DIM, DSA_HEAD_DIM, DSA_HEAD_DIM,
    IDX_HEADS * IDX_DIM, IDX_DIM, IDX_HEADS,
    GLA_HEADS * GLA_DK, GLA_HEADS * GLA_DK, GLA_HEADS * GLA_DV,
    GLA_GATE_RANK, GLA_HEADS * GLA_DV,
    N_BRANCH * D_MODEL,
)
_OFF = np.concatenate([[0], np.cumsum(IN_SIZES)]).astype(int)
(O_CQ, O_CKV, O_KR, O_CB, O_CC, O_CX, O_DQ, O_DK, O_DV, O_IQ, O_IK, O_IW,
 O_GQ, O_GK, O_GV, O_GLR, O_GR, O_GATE, O_END) = [int(v) for v in _OFF]

P_CQ, P_CKV, P_KRP, P_KRS = 0, 256, 384, 512
P_CB, P_CC, P_CX = 640, 896, 1152
P_DQ, P_DKK, P_DVV = 1408, 1664, 1792
P_IQ, P_IKR, P_IW = 1920, 2176, 2304
P_GQ, P_GK, P_GV, P_GLR, P_GR = 2432, 2560, 2688, 2944, 3072
P_TOTAL = 3328

VMEM_LIMIT = 56 * 1024 * 1024


def _cparams(sem):
    return pltpu.CompilerParams(dimension_semantics=sem, vmem_limit_bytes=VMEM_LIMIT)


def _rms(x, g):
    return x * lax.rsqrt(jnp.mean(x * x, axis=-1, keepdims=True) + EPS) * g


def _dot(a, b):
    return jnp.dot(a, b, preferred_element_type=F32)


def _dot_nt(a, b):
    return lax.dot_general(a, b, (((1,), (1,)), ((), ())), preferred_element_type=F32)


def _rope_kernel(pos_ref, pat_ref, c_ref, s_ref):
    pos = pos_ref[...].astype(F32)
    ang = pos * pat_ref[0:1, :]
    c_ref[...] = pat_ref[1:2, :] + pat_ref[2:3, :] * jnp.cos(ang)
    s_ref[...] = pat_ref[3:4, :] * jnp.sin(ang)


def _rope_tables(positions, tm):
    t = positions.size
    half = MLA_ROPE // 2
    inv_freq = ROPE_THETA ** (-jnp.arange(half, dtype=F32) / half)
    z64, z32 = jnp.zeros((MLA_NOPE,), F32), jnp.zeros((HEAD_PAD - MLA_NOPE - MLA_ROPE,), F32)
    o16 = jnp.ones((half,), F32)
    pat = jnp.stack([
        jnp.concatenate([z64, inv_freq, inv_freq, z32]),
        jnp.concatenate([z64 + 1.0, 0 * o16, 0 * o16, z32]),
        jnp.concatenate([z64, o16, o16, z32]),
        jnp.concatenate([z64, -o16, o16, z32]),
    ])
    pat = jnp.concatenate([pat, jnp.zeros((4, HEAD_PAD), F32)], axis=0)
    return pl.pallas_call(
        _rope_kernel,
        grid=(t // tm,),
        in_specs=[pl.BlockSpec((tm, 1), lambda i: (i, 0)), pl.BlockSpec((8, HEAD_PAD), lambda i: (0, 0))],
        out_specs=[pl.BlockSpec((tm, HEAD_PAD), lambda i: (i, 0))] * 2,
        out_shape=[jax.ShapeDtypeStruct((t, HEAD_PAD), F32)] * 2,
        compiler_params=_cparams(("arbitrary",)),
        name="rope_tables",
    )(positions.reshape(t, 1), pat)


def _prep_kernel(x_ref, g_ref, w_ref, qg_ref, wq_ref, wqs_ref, kvg_ref, wk_ref, wv_ref, c_ref, s_ref,
                 convw_ref, wgate_ref, bgate_ref,
                 mq_ref, mk_ref, mv_ref, oconv_ref, dq_ref, dkk_ref, dvv_ref, iq_ref, ikr_ref, iw_ref,
                 gq_ref, gk_ref, gv_ref, gg_ref, gr_ref, u_scr, *, tm):
    j = pl.program_id(1)
    h = _rms(x_ref[...], g_ref[...]).astype(BF16)
    p = _dot(h, w_ref[...])

    c1, s1 = c_ref[...], s_ref[...]
    c4 = jnp.concatenate([c1] * MLA_HEADS, axis=1)
    s4 = jnp.concatenate([s1] * MLA_HEADS, axis=1)

    qn = _rms(p[:, P_CQ:P_CQ + MLA_Q_RANK], qg_ref[...]).astype(BF16)
    q = _dot(qn, wq_ref[...]) * c4 + _dot(qn, wqs_ref[...]) * s4
    mq_ref[...] = (q * ((MLA_NOPE + MLA_ROPE) ** -0.5 * LOG2E)).astype(BF16)
    kvn = _rms(p[:, P_CKV:P_CKV + MLA_KV_RANK], kvg_ref[...]).astype(BF16)
    kr = p[:, P_KRP:P_KRP + HEAD_PAD] * c1 + p[:, P_KRS:P_KRS + HEAD_PAD] * s1
    mk_ref[...] = (_dot(kvn, wk_ref[...]) + jnp.concatenate([kr] * MLA_HEADS, axis=1)).astype(BF16)
    mv_ref[...] = _dot(kvn, wv_ref[...]).astype(BF16)

    u = p[:, P_CC:P_CC + CONV_WIDTH] * p[:, P_CX:P_CX + CONV_WIDTH]

    @pl.when(j == 0)
    def _():
        u_scr[0:8, :] = jnp.zeros((8, CONV_WIDTH), F32)

    u_scr[8:8 + tm, :] = u
    cw = convw_ref[...]
    y = cw[0:1, :] * u_scr[6:6 + tm, :] + cw[1:2, :] * u_scr[7:7 + tm, :] + cw[2:3, :] * u
    oconv_ref[...] = (p[:, P_CB:P_CB + CONV_WIDTH] * y).astype(BF16)
    u_scr[0:8, :] = u_scr[tm:tm + 8, :]

    dq_ref[...] = (p[:, P_DQ:P_DQ + 256] * (DSA_HEAD_DIM ** -0.5 * LOG2E)).astype(BF16)
    dkk_ref[...] = p[:, P_DKK:P_DKK + LANE].astype(BF16)
    dvv_ref[...] = p[:, P_DVV:P_DVV + LANE].astype(BF16)
    iq_ref[...] = p[:, P_IQ:P_IQ + 256].astype(BF16)
    ikr_ref[...] = p[:, P_IKR:P_IKR + LANE].astype(BF16)
    iw_ref[...] = p[:, P_IW:P_IW + LANE] * ((IDX_DIM ** -0.5) * (IDX_HEADS ** -0.5))

    gq_ref[...] = p[:, P_GQ:P_GQ + LANE] * (GLA_DK ** -0.5)
    gk_ref[...] = p[:, P_GK:P_GK + LANE]
    gv_ref[...] = p[:, P_GV:P_GV + 256]
    gate = _dot(p[:, P_GLR:P_GLR + LANE].astype(BF16), wgate_ref[...]) + bgate_ref[...]
    log_sig = jnp.minimum(gate, 0.0) - jnp.log1p(jnp.exp(-jnp.abs(gate)))
    gg_ref[...] = log_sig * (1.0 / GLA_TAU)
    gr = p[:, P_GR:P_GR + 256]
    gr_ref[...] = gr / (1.0 + jnp.exp(-gr))


def _pack_small_weight(w_in):
    d = w_in.shape[0]

    def cols(a, b):
        return w_in[:, a:b]

    def zeros(n):
        return jnp.zeros((d, n), w_in.dtype)

    half = MLA_ROPE // 2
    kr1, kr2 = cols(O_KR, O_KR + half), cols(O_KR + half, O_KR + MLA_ROPE)
    pad_tail = HEAD_PAD - MLA_NOPE - MLA_ROPE
    pieces = [
        cols(O_CQ, O_CKV), cols(O_CKV, O_KR),
        zeros(MLA_NOPE), kr1, kr2, zeros(pad_tail),
        zeros(MLA_NOPE), kr2, kr1, zeros(pad_tail),
        cols(O_CB, O_CC), cols(O_CC, O_CX), cols(O_CX, O_DQ),
        cols(O_DQ, O_DK), cols(O_DK, O_DV), cols(O_DK, O_DV), cols(O_DV, O_IQ), cols(O_DV, O_IQ),
        cols(O_IQ, O_IK), cols(O_IK, O_IW), cols(O_IK, O_IW), cols(O_IK, O_IW), cols(O_IK, O_IW),
        cols(O_IW, O_GQ), zeros(LANE - IDX_HEADS),
        cols(O_GQ, O_GK), cols(O_GK, O_GV), cols(O_GV, O_GLR),
        cols(O_GLR, O_GR), zeros(LANE - GLA_GATE_RANK),
        cols(O_GR, O_GATE),
    ]
    out = jnp.concatenate(pieces, axis=1)
    assert out.shape[1] == P_TOTAL, out.shape
    return out.astype(BF16)


def _pack_mla_weights(w_q_up, w_kv_up):
    half = MLA_ROPE // 2
    rq = w_q_up.shape[0]
    rk = w_kv_up.shape[0]
    wq = w_q_up.reshape(rq, MLA_HEADS, MLA_NOPE + MLA_ROPE)
    nope, x1, x2 = wq[..., :MLA_NOPE], wq[..., MLA_NOPE:MLA_NOPE + half], wq[..., MLA_NOPE + half:]
    zq = jnp.zeros((rq, MLA_HEADS, HEAD_PAD - MLA_NOPE - MLA_ROPE), w_q_up.dtype)
    wq_pad = jnp.concatenate([nope, x1, x2, zq], axis=-1).reshape(rq, MLA_HEADS * HEAD_PAD)
    wq_swap = jnp.concatenate([jnp.zeros_like(nope), x2, x1, zq], axis=-1).reshape(rq, MLA_HEADS * HEAD_PAD)
    wkv = w_kv_up.reshape(rk, MLA_HEADS, MLA_NOPE + MLA_V)
    zk = jnp.zeros((rk, MLA_HEADS, HEAD_PAD - MLA_NOPE), w_kv_up.dtype)
    wk_pad = jnp.concatenate([wkv[..., :MLA_NOPE], zk], axis=-1).reshape(rk, MLA_HEADS * HEAD_PAD)
    wv = wkv[..., MLA_NOPE:].reshape(rk, MLA_HEADS * MLA_V)
    return wq_pad.astype(BF16), wq_swap.astype(BF16), wk_pad.astype(BF16), wv.astype(BF16)


def _prep(x2, nb, g, w_small, qg, wq, wqs, kvg, wk, wv, ctab, stab, conv_w, wgate, bgate, tm):
    t, d = x2.shape
    nj = t // nb // tm

    def tok(width):
        return pl.BlockSpec((tm, width), lambda b, j: (b * nj + j, 0))

    def full(a):
        return pl.BlockSpec(a.shape, lambda b, j: (0,) * a.ndim, pipeline_mode=pl.Buffered(1))

    outs = [(512, BF16), (512, BF16), (256, BF16), (256, BF16), (256, BF16), (LANE, BF16), (LANE, BF16),
            (256, BF16), (LANE, BF16), (LANE, F32), (LANE, F32), (LANE, F32), (256, F32), (LANE, F32), (256, F32)]
    return pl.pallas_call(
        functools.partial(_prep_kernel, tm=tm),
        grid=(nb, nj),
        in_specs=[tok(d), full(g), full(w_small), full(qg), full(wq), full(wqs), full(kvg), full(wk), full(wv),
                  tok(HEAD_PAD), tok(HEAD_PAD), full(conv_w), full(wgate), full(bgate)],
        out_specs=[tok(w) for w, _ in outs],
        out_shape=[jax.ShapeDtypeStruct((t, w), dt) for w, dt in outs],
        scratch_shapes=[pltpu.VMEM((tm + 8, CONV_WIDTH), F32)],
        compiler_params=_cparams(("arbitrary", "arbitrary")),
        name="prep",
    )(x2, g, w_small, qg, wq, wqs, kvg, wk, wv, ctab, stab, conv_w, wgate, bgate)


def _fold_lanes(x, op):
    out = x[:, 0:LANE]
    for g in range(1, x.shape[1] // LANE):
        out = op(out, x[:, g * LANE:(g + 1) * LANE])
    return out


def _mla_kernel(q_ref, k_ref, v_ref, o_ref, s_scr, m_scr, l_scr, acc_scr, *, tq, tk):
    i = pl.program_id(1)
    n_chunks = ((i + 1) * tq + tk - 1) // tk
    row = i * tq + lax.broadcasted_iota(I32, (tq, tk), 0)
    col = lax.broadcasted_iota(I32, (tq, tk), 1)
    lane = lax.broadcasted_iota(I32, (tq, LANE), 1)
    heads = range(MLA_HEADS)
    hs = [slice(h * HEAD_PAD, (h + 1) * HEAD_PAD) for h in heads]
    vs = [slice((h // 2) * LANE, (h // 2 + 1) * LANE) for h in heads]

    m_scr[...] = jnp.full(m_scr.shape, NEG, F32)
    l_scr[...] = jnp.zeros(l_scr.shape, F32)
    acc_scr[...] = jnp.zeros(acc_scr.shape, F32)

    def phase1(c, carry, masked):
        off = pl.multiple_of(c * tk, tk)
        for h in heads:
            s = _dot_nt(q_ref[:, hs[h]], k_ref[pl.ds(off, tk), hs[h]])
            if masked:
                s = jnp.where(c * tk + col <= row, s, NEG)
            s_scr[h, c] = s
            m_scr[h] = jnp.maximum(m_scr[h], _fold_lanes(s, jnp.maximum))
        return carry

    lax.fori_loop(0, n_chunks - 1, functools.partial(phase1, masked=False), 0)
    phase1(n_chunks - 1, 0, True)
    for h in heads:
        m_scr[h] = jnp.broadcast_to(jnp.max(m_scr[h], axis=1, keepdims=True), (tq, LANE))

    def phase2(c, carry):
        off = pl.multiple_of(c * tk, tk)
        for h in heads:
            p = jnp.exp2(s_scr[h, c] - jnp.concatenate([m_scr[h]] * (tk // LANE), axis=1))
            l_scr[h] = l_scr[h] + _fold_lanes(p, jnp.add)
            acc_scr[h] = acc_scr[h] + _dot(p.astype(BF16), v_ref[pl.ds(off, tk), vs[h]])
        return carry

    lax.fori_loop(0, n_chunks, phase2, 0)
    res = [acc_scr[h] / jnp.sum(l_scr[h], axis=1, keepdims=True) for h in heads]
    for pair in range(MLA_HEADS // 2):
        o_ref[:, pair * LANE:(pair + 1) * LANE] = jnp.where(lane < MLA_V, res[2 * pair], res[2 * pair + 1]).astype(BF16)


def _mla(mq, mk, mv, nb, tq, tk):
    t = mq.shape[0]
    s = t // nb
    nq = s // tq
    return pl.pallas_call(
        functools.partial(_mla_kernel, tq=tq, tk=tk),
        grid=(nb, nq),
        in_specs=[pl.BlockSpec((tq, MLA_HEADS * HEAD_PAD), lambda b, i: (b * nq + i, 0)),
                  pl.BlockSpec((s, MLA_HEADS * HEAD_PAD), lambda b, i: (b, 0)),
                  pl.BlockSpec((s, MLA_HEADS * MLA_V), lambda b, i: (b, 0))],
        out_specs=pl.BlockSpec((tq, MLA_HEADS * MLA_V), lambda b, i: (b * nq + i, 0)),
        out_shape=jax.ShapeDtypeStruct((t, MLA_HEADS * MLA_V), BF16),
        scratch_shapes=[pltpu.VMEM((MLA_HEADS, s // tk, tq, tk), F32)] + [pltpu.VMEM((MLA_HEADS, tq, LANE), F32)] * 3,
        compiler_params=_cparams(("arbitrary", "arbitrary")),
        name="mla_attn",
    )(mq, mk, mv)


def _dsa_kernel(iq_ref, iw_ref, ikr_ref, dq_ref, dkk_ref, dvv_ref, tri_ref, o_ref,
                key_scr, plane_scr, eq_scr, gt_scr, thr_scr, z_scr, s_scr, m_scr, l_scr, acc_scr,
                *, tq, th, tk, topk):
    i = pl.program_id(1)
    q_end = (i + 1) * tq
    n_chunks = (q_end + tk - 1) // tk
    lane_q = lax.broadcasted_iota(I32, (th, LANE), 1)
    lane_k = lax.broadcasted_iota(I32, (tq, tk), 1)
    int_min = jnp.int32(-2 ** 31)
    rb = 64
    lane_rb = lax.broadcasted_iota(I32, (rb, tk), 1)
    row_rb = lax.broadcasted_iota(I32, (rb, 1), 0)

    for r0 in range(0, tq, th):
        iq = iq_ref[r0:r0 + th, :].astype(F32)
        iw = iw_ref[r0:r0 + th, :]
        q_heads = []
        w_heads = []
        for h in range(IDX_HEADS):
            grp = h // 4
            sub = iq[:, grp * LANE:(grp + 1) * LANE]
            lo = (h % 4) * IDX_DIM
            in_head = (lane_q >= lo) & (lane_q < lo + IDX_DIM)
            q_heads.append(jnp.where(in_head, sub, 0.0).astype(BF16))
            w_heads.append(iw[:, h:h + 1])
        q_stack = jnp.concatenate(q_heads, axis=0)

        def score_chunk(c, _, r0=r0, q_stack=q_stack, w_heads=w_heads):
            off = pl.multiple_of(c * tk, tk)
            z_scr[...] = _dot_nt(q_stack, ikr_ref[pl.ds(off, tk), :])
            for r in range(0, th, rb):
                acc = w_heads[0][r:r + rb] * jnp.maximum(z_scr[r:r + rb, :], 0.0)
                for h in range(1, IDX_HEADS):
                    acc = acc + w_heads[h][r:r + rb] * jnp.maximum(z_scr[h * th + r:h * th + r + rb, :], 0.0)
                bits = pltpu.bitcast(acc, I32)
                key = jnp.where(bits < 0, bits ^ jnp.int32(0x7FFFFFFF), bits)
                key = jnp.where(c * tk + lane_rb <= i * tq + r0 + r + row_rb, key, int_min)
                key_scr[c, r0 + r:r0 + r + rb, :] = key
            return 0

        lax.fori_loop(0, n_chunks, score_chunk, 0)

    kf = jnp.float32(topk)
    n_groups = key_scr.shape[0] * (tk // LANE)
    cb = 128

    def row_total(cnt):
        return _dot(cnt.astype(F32).astype(BF16), jnp.ones((LANE, LANE), BF16))

    def select():
        def fill(c, _):
            key_scr[c] = jnp.full((tq, tk), int_min, I32)
            return 0
        lax.fori_loop(n_chunks, key_scr.shape[0], fill, 0)

        def build_planes(blk, _):
            r = pl.multiple_of(blk * 8, 8)
            w = []
            for g in range(32):
                if g < n_groups:
                    gc, gl = divmod(g, tk // LANE)
                    w.append(key_scr[gc, pl.ds(r, 8), gl * LANE:(gl + 1) * LANE] ^ int_min)
                else:
                    w.append(jnp.zeros((8, LANE), I32))
            j, m = 16, 0x0000FFFF
            while j:
                mm = jnp.int32(m - (1 << 32) if m >= (1 << 31) else m)
                k = 0
                while k < 32:
                    t = (w[k] ^ lax.shift_right_logical(w[k + j], jnp.int32(j))) & mm
                    w[k] = w[k] ^ t
                    w[k + j] = w[k + j] ^ lax.shift_left(t, jnp.int32(j))
                    k = (k + j + 1) & ~j
                j >>= 1
                m = (m ^ (m << j)) & 0xFFFFFFFF
            for b in range(32):
                plane_scr[b, pl.ds(r, 8), :] = w[b]
            return 0
        lax.fori_loop(0, tq // 8, build_planes, 0)

        eq_scr[...] = jnp.full((tq, LANE), -1, I32)
        gt_scr[...] = jnp.zeros((tq, LANE), I32)
        thr_scr[...] = jnp.zeros((tq, LANE), I32)

        def bit_pass(it, _):
            bit = lax.shift_left(jnp.int32(1), 31 - it)
            for r in range(0, tq, cb):
                rows = slice(r, r + cb)
                eq, gtm = eq_scr[rows, :], gt_scr[rows, :]
                t = eq & plane_scr[it, rows, :]
                u = gtm | t
                take = row_total(lax.population_count(u)) >= kf
                eq_scr[rows, :] = jnp.where(take, t, eq ^ t)
                gt_scr[rows, :] = jnp.where(take, gtm, u)
                thr_scr[rows, :] = jnp.where(take, thr_scr[rows, :] | bit, thr_scr[rows, :])
            return 0
        lax.fori_loop(0, 32, bit_pass, 0)

        thr_u = thr_scr[:, 0:1]
        thr = jnp.maximum(thr_u ^ int_min, int_min + 1)
        c_gt = row_total(lax.population_count(gt_scr[...]))[:, 0:1]
        c_ge = c_gt + row_total(lax.population_count(eq_scr[...]))[:, 0:1]
        excess = (c_ge > kf) & (thr_u != 0)

        def tie_break():
            need = kf - c_gt

            def demote(c, seen):
                kk = key_scr[c]
                tie = kk == thr
                rank = seen + _dot(jnp.where(tie, 1.0, 0.0).astype(BF16), tri_ref[...])
                key_scr[c] = jnp.where(tie & (rank > need), thr - 1, kk)
                return rank[:, tk - 1:tk]
            lax.fori_loop(0, n_chunks, demote, jnp.zeros((tq, 1), F32))

        any_excess = jnp.max(jnp.where(excess, 1.0, 0.0)) > 0.0
        lax.cond(any_excess, tie_break, lambda: None)
        return thr

    thr = lax.cond(q_end > topk, select, lambda: jnp.full((tq, 1), int_min + 1, I32))

    heads = range(DSA_HEADS)
    for r0 in range(0, tq, th):
        rows = slice(r0, r0 + th)
        thr_h = thr[rows]
        dq = dq_ref[rows, :].astype(F32)
        qm = []
        for h in heads:
            sub = dq[:, (h // 2) * LANE:(h // 2 + 1) * LANE]
            lo = (h % 2) * DSA_HEAD_DIM
            in_head = (lane_q >= lo) & (lane_q < lo + DSA_HEAD_DIM)
            qm.append(jnp.where(in_head, sub, 0.0).astype(BF16))
        qm_stack = jnp.concatenate(qm, axis=0)

        m_scr[...] = jnp.full(m_scr.shape, NEG, F32)
        l_scr[...] = jnp.zeros(l_scr.shape, F32)
        acc_scr[...] = jnp.zeros(acc_scr.shape, F32)

        def phase1(c, carry, rows=rows, thr_h=thr_h, qm_stack=qm_stack):
            off = pl.multiple_of(c * tk, tk)
            sel = key_scr[c, rows, :] >= thr_h
            s = _dot_nt(qm_stack, dkk_ref[pl.ds(off, tk), :])
            for h in heads:
                sm = jnp.where(sel, s[h * th:(h + 1) * th], NEG)
                s_scr[h, c] = sm
                m_scr[h] = jnp.maximum(m_scr[h], _fold_lanes(sm, jnp.maximum))
            return carry

        lax.fori_loop(0, n_chunks, phase1, 0)
        for h in heads:
            m_scr[h] = jnp.broadcast_to(jnp.max(m_scr[h], axis=1, keepdims=True), (th, LANE))

        def phase2(c, carry):
            off = pl.multiple_of(c * tk, tk)
            ps = []
            for h in heads:
                p = jnp.exp2(s_scr[h, c] - jnp.concatenate([m_scr[h]] * (tk // LANE), axis=1))
                l_scr[h] = l_scr[h] + _fold_lanes(p, jnp.add)
                ps.append(p.astype(BF16))
            pv = _dot(jnp.concatenate(ps, axis=0), dvv_ref[pl.ds(off, tk), :])
            for h in heads:
                acc_scr[h] = acc_scr[h] + pv[h * th:(h + 1) * th]
            return carry

        lax.fori_loop(0, n_chunks, phase2, 0)
        outs = [acc_scr[h] / jnp.sum(l_scr[h], axis=1, keepdims=True) for h in heads]
        lo_half = lane_q < DSA_HEAD_DIM
        o_ref[rows, :] = jnp.concatenate([jnp.where(lo_half, outs[0], outs[1]),
                                          jnp.where(lo_half, outs[2], outs[3])], axis=1).astype(BF16)


def _dsa(iq, iw, ikr, dq, dkk, dvv, nb, tq, th, tk):
    t = iq.shape[0]
    s = t // nb
    nq = s // tq
    topk = min(DSA_TOPK_MAX, s // 4)
    assert s // LANE <= 32, "the bit-sliced selection packs one key per 128-lane group into a 32-bit word"
    tri = jnp.asarray(np.triu(np.ones((tk, tk), np.float32)), BF16)

    def qspec(w):
        return pl.BlockSpec((tq, w), lambda b, i: (b * nq + i, 0))

    def kspec(w):
        return pl.BlockSpec((s, w), lambda b, i: (b, 0))

    return pl.pallas_call(
        functools.partial(_dsa_kernel, tq=tq, th=th, tk=tk, topk=topk),
        grid=(nb, nq),
        in_specs=[qspec(256), qspec(LANE), kspec(LANE), qspec(256), kspec(LANE), kspec(LANE),
                  pl.BlockSpec((tk, tk), lambda b, i: (0, 0), pipeline_mode=pl.Buffered(1))],
        out_specs=qspec(256),
        out_shape=jax.ShapeDtypeStruct((t, DSA_HEADS * DSA_HEAD_DIM), BF16),
        scratch_shapes=[pltpu.VMEM((s // tk, tq, tk), I32),
                        pltpu.VMEM((32, tq, LANE), I32),
                        pltpu.VMEM((tq, LANE), I32),
                        pltpu.VMEM((tq, LANE), I32),
                        pltpu.VMEM((tq, LANE), I32),
                        pltpu.VMEM((IDX_HEADS * th, tk), F32),
                        pltpu.VMEM((DSA_HEADS, s // tk, th, tk), F32)] + [pltpu.VMEM((DSA_HEADS, th, LANE), F32)] * 3,
        compiler_params=_cparams(("arbitrary", "arbitrary")),
        name="dsa",
    )(iq, iw, ikr, dq, dkk, dvv, tri)


def _gla_kernel(gq_ref, gk_ref, gv_ref, gg_ref, gr_ref, ng_ref, segv_ref, seg64_ref, bmask_ref, o_ref,
                st_scr, kbuf, cbuf, vbuf, *, tg):
    j = pl.program_id(1)
    ch = GLA_CHUNK
    rowc = lax.broadcasted_iota(I32, (tg, 1), 0) % ch

    @pl.when(j == 0)
    def _():
        st_scr[...] = jnp.zeros_like(st_scr)
        kbuf[0:ch, :] = jnp.zeros((ch, LANE), F32)
        cbuf[0:ch, :] = jnp.zeros((ch, LANE), F32)
        vbuf[0:ch, :] = jnp.zeros((ch, 2 * LANE), F32)

    cbuf[ch:ch + tg, :] = gg_ref[...]
    sh = 1
    while sh < ch:
        cur = cbuf[ch:ch + tg, :]
        prev = cbuf[ch - sh:ch - sh + tg, :]
        cbuf[ch:ch + tg, :] = cur + jnp.where(rowc >= sh, prev, 0.0)
        sh *= 2
    cum = cbuf[ch:ch + tg, :]
    q = gq_ref[...]
    k = gk_ref[...]
    v = gv_ref[...]
    kbuf[ch:ch + tg, :] = k
    vbuf[ch:ch + tg, :] = v

    qd = (q * jnp.exp(cum)).astype(BF16)
    bmask = bmask_ref[...]
    inter = []
    for c in range(tg // ch):
        rs = slice(c * ch, (c + 1) * ch)
        st = st_scr[...]
        inter.append(_dot_nt(qd[rs], st.astype(BF16)))
        last = cum[c * ch + ch - 1:c * ch + ch, :]
        kp = k[rs] * jnp.exp(last - cum[rs])
        upd = lax.dot_general(v[rs].astype(BF16), kp.astype(BF16), (((0,), (0,)), ((), ())),
                              preferred_element_type=F32)
        st_scr[...] = jnp.exp(last) * st + upd * bmask
    o = jnp.concatenate(inter, axis=0)

    segv = segv_ref[...]
    for d in range(ch):
        kd = kbuf[ch - d:ch - d + tg, :]
        cd = cbuf[ch - d:ch - d + tg, :]
        vd = vbuf[ch - d:ch - d + tg, :]
        ok = rowc >= d
        t = jnp.where(ok, q * kd * jnp.exp(jnp.where(ok, cum - cd, 0.0)), 0.0)
        o = o + _dot(t.astype(BF16), segv) * vd

    sq = o * o
    hi = sq.astype(BF16)
    lo = (sq - hi.astype(F32)).astype(BF16)
    var = _dot(hi, seg64_ref[...]) + _dot(lo, seg64_ref[...])
    o_ref[...] = (gr_ref[...] * (o * lax.rsqrt(var + EPS) * ng_ref[...])).astype(BF16)


def _gla(gq, gk, gv, gg, gr, norm_g, nb, tg):
    t = gq.shape[0]
    nj = t // nb // tg
    ng = jnp.tile(norm_g.reshape(1, GLA_DV), (1, GLA_HEADS)).astype(F32)
    hk = np.arange(GLA_HEADS * GLA_DK) // GLA_DK
    hv = np.arange(GLA_HEADS * GLA_DV) // GLA_DV
    same = (hk[:, None] == hv[None, :])
    segv = jnp.asarray(same, BF16)
    bmask = jnp.asarray(same.T, F32)
    seg64 = jnp.asarray((hv[:, None] == hv[None, :]) / GLA_DV, BF16)

    def tok(w):
        return pl.BlockSpec((tg, w), lambda b, j: (b * nj + j, 0))

    def full(a):
        return pl.BlockSpec(a.shape, lambda b, j: (0,) * a.ndim, pipeline_mode=pl.Buffered(1))

    return pl.pallas_call(
        functools.partial(_gla_kernel, tg=tg),
        grid=(nb, nj),
        in_specs=[tok(LANE), tok(LANE), tok(256), tok(LANE), tok(256), full(ng), full(segv), full(seg64), full(bmask)],
        out_specs=tok(256),
        out_shape=jax.ShapeDtypeStruct((t, GLA_HEADS * GLA_DV), BF16),
        scratch_shapes=[pltpu.VMEM((2 * LANE, LANE), F32),
                        pltpu.VMEM((GLA_CHUNK + tg, LANE), F32),
                        pltpu.VMEM((GLA_CHUNK + tg, LANE), F32),
                        pltpu.VMEM((GLA_CHUNK + tg, 2 * LANE), F32)],
        compiler_params=_cparams(("arbitrary", "arbitrary")),
        name="gla",
    )(gq, gk, gv, gg, gr, ng, segv, seg64, bmask)


def _merge_kernel(x_ref, g_ref, wg_ref, o0_ref, o1_ref, o2_ref, o3_ref, wb_ref, wo_ref, out_ref):
    x = x_ref[...]
    h = _rms(x, g_ref[...]).astype(BF16)
    merged = None
    for n, o_ref in enumerate((o0_ref, o1_ref, o2_ref, o3_ref)):
        gate = _dot(h, wg_ref[:, n * D_MODEL:(n + 1) * D_MODEL])
        gate = 1.0 / (1.0 + jnp.exp(-gate))
        term = gate * _dot(o_ref[...], wb_ref[n])
        merged = term if merged is None else merged + term
    out_ref[...] = x + _dot(merged.astype(BF16), wo_ref[...])


def _merge(x2, g, wg, branches, wb, wo, tm):
    t, d = x2.shape

    def tok(w):
        return pl.BlockSpec((tm, w), lambda i: (i, 0))

    def full(a):
        return pl.BlockSpec(a.shape, lambda i: (0,) * a.ndim, pipeline_mode=pl.Buffered(1))

    return pl.pallas_call(
        _merge_kernel,
        grid=(t // tm,),
        in_specs=[tok(d), full(g), full(wg)] + [tok(BRANCH_WIDTH)] * N_BRANCH + [full(wb), full(wo)],
        out_specs=tok(d),
        out_shape=jax.ShapeDtypeStruct((t, d), F32),
        compiler_params=_cparams(("arbitrary",)),
        name="merge",
    )(x2, g, wg, *branches, wb, wo)


def _ffn_kernel(x_ref, g_ref, wgu_ref, wd_ref, fg_ref, out_ref, *, fc, final):
    x = x_ref[...]
    h = _rms(x, g_ref[...]).astype(BF16)
    acc = x
    for c in range(D_FF // fc):
        gate = _dot(h, wgu_ref[0, :, c * fc:(c + 1) * fc])
        up = _dot(h, wgu_ref[1, :, c * fc:(c + 1) * fc])
        act = (gate / (1.0 + jnp.exp(-gate)) * up).astype(BF16)
        acc = acc + _dot(act, wd_ref[c * fc:(c + 1) * fc, :])
    if final:
        acc = _rms(acc, fg_ref[...])
    out_ref[...] = acc


def _ffn(x2, g, wgu, wd, fg, tm, final):
    t, d = x2.shape

    def tok(w):
        return pl.BlockSpec((tm, w), lambda i: (i, 0))

    def full(a):
        return pl.BlockSpec(a.shape, lambda i: (0,) * a.ndim, pipeline_mode=pl.Buffered(1))

    return pl.pallas_call(
        functools.partial(_ffn_kernel, fc=256, final=final),
        grid=(t // tm,),
        in_specs=[tok(d), full(g), full(wgu), full(wd), full(fg)],
        out_specs=tok(d),
        out_shape=jax.ShapeDtypeStruct((t, d), F32),
        compiler_params=_cparams(("arbitrary",)),
        name="ffn",
    )(x2, g, wgu, wd, fg)


def _tile(n, pref):
    while n % pref:
        pref //= 2
    return pref


def kernel(x, positions, attn_norm_g, w_in, mla_q_norm_g, mla_w_q_up, mla_kv_norm_g, mla_w_kv_up, conv_w, gla_w_gate_up, gla_b_gate, gla_norm_g, w_branch, w_out, ffn_norm_g, w_ffn_gate, w_ffn_up, w_ffn_down, final_norm_g):
    b, s, d = x.shape
    depth = w_in.shape[0]
    t = b * s
    assert d == D_MODEL and s % GLA_CHUNK == 0
    tm_prep = _tile(s, 256)
    tm_wide = _tile(t, 512)
    tq_mla = _tile(s, 256)
    tk_att = _tile(s, 512)
    tq_dsa = _tile(s, 512)
    th_dsa = _tile(tq_dsa, 256)
    tg = _tile(s, 256)

    x2 = x.reshape(t, d)
    ctab, stab = _rope_tables(positions, _tile(t, 512))
    row = lambda a: a.reshape(1, -1).astype(F32)

    for l in range(depth):
        w_small = _pack_small_weight(w_in[l])
        wq, wqs, wk, wv = _pack_mla_weights(mla_w_q_up[l], mla_w_kv_up[l])
        wgate = jnp.concatenate([gla_w_gate_up[l], jnp.zeros((LANE - GLA_GATE_RANK, GLA_HEADS * GLA_DK), F32)],
                                axis=0).astype(BF16)
        (mq, mk, mv, o_conv, dq, dkk, dvv, iq, ikr, iw, gq, gk, gv, gg, gr) = _prep(
            x2, b, row(attn_norm_g[l]), w_small, row(mla_q_norm_g[l]), wq, wqs, row(mla_kv_norm_g[l]), wk, wv,
            ctab, stab, conv_w[l].astype(F32), wgate, row(gla_b_gate[l]), tm_prep)
        o_mla = _mla(mq, mk, mv, b, tq_mla, tk_att)
        o_dsa = _dsa(iq, iw, ikr, dq, dkk, dvv, b, tq_dsa, th_dsa, tk_att)
        o_gla = _gla(gq, gk, gv, gg, gr, gla_norm_g[l], b, tg)
        x2 = _merge(x2, row(attn_norm_g[l]), w_in[l][:, O_GATE:].astype(BF16), (o_mla, o_conv, o_dsa, o_gla),
                    w_branch[l].astype(BF16), w_out[l].astype(BF16), tm_wide)
        wgu = jnp.stack([w_ffn_gate[l], w_ffn_up[l]]).astype(BF16)
        x2 = _ffn(x2, row(ffn_norm_g[l]), wgu, w_ffn_down[l].astype(BF16), row(final_norm_g), tm_wide,
                  final=(l == depth - 1))
    return x2.reshape(b, s, d)
```

```python
import functools

import jax
import jax.numpy as jnp
import numpy as np
from jax import lax
from jax.experimental import pallas as pl
from jax.experimental.pallas import tpu as pltpu

F32 = jnp.float32
BF16 = jnp.bfloat16
I32 = jnp.int32
I16 = jnp.int16

D_MODEL = 1024
MLA_HEADS = 4
MLA_Q_RANK = 256
MLA_KV_RANK = 128
MLA_NOPE = 64
MLA_ROPE = 32
MLA_V = 64
ROPE_THETA = 10000.0
CONV_WIDTH = 256
CONV_K = 3
DSA_HEADS = 4
DSA_HEAD_DIM = 64
IDX_HEADS = 8
IDX_DIM = 32
DSA_TOPK_MAX = 256
GLA_HEADS = 4
GLA_DK = 32
GLA_DV = 64
GLA_GATE_RANK = 16
GLA_TAU = 16.0
GLA_CHUNK = 64
N_BRANCH = 4
BRANCH_WIDTH = 256
D_FF = ((8 * D_MODEL + 3 * 256 - 1) // (3 * 256)) * 256
EPS = 1e-6
NEG = -1e30
LOG2E = 1.4426950408889634

LANE = 128
HEAD_PAD = 128

IN_SIZES = (
    MLA_Q_RANK, MLA_KV_RANK, MLA_ROPE,
    CONV_WIDTH, CONV_WIDTH, CONV_WIDTH,
    DSA_HEADS * DSA_HEAD_DIM, DSA_HEAD_DIM, DSA_HEAD_DIM,
    IDX_HEADS * IDX_DIM, IDX_DIM, IDX_HEADS,
    GLA_HEADS * GLA_DK, GLA_HEADS * GLA_DK, GLA_HEADS * GLA_DV,
    GLA_GATE_RANK, GLA_HEADS * GLA_DV,
    N_BRANCH * D_MODEL,
)
_OFF = np.concatenate([[0], np.cumsum(IN_SIZES)]).astype(int)
(O_CQ, O_CKV, O_KR, O_CB, O_CC, O_CX, O_DQ, O_DK, O_DV, O_IQ, O_IK, O_IW,
 O_GQ, O_GK, O_GV, O_GLR, O_GR, O_GATE, O_END) = [int(v) for v in _OFF]

P_CQ, P_CKV, P_KRP, P_KRS = 0, 256, 384, 512
P_CB, P_CC, P_CX = 640, 896, 1152
P_DQ, P_DKK, P_DVV = 1408, 1664, 1792
P_IQ, P_IKR, P_IW = 1920, 2176, 2304
P_GQ, P_GK, P_GV, P_GLR, P_GR = 2432, 2560, 2688, 2944, 3072
P_TOTAL = 3328

VMEM_LIMIT = 56 * 1024 * 1024


def _cparams(sem):
    return pltpu.CompilerParams(dimension_semantics=sem, vmem_limit_bytes=VMEM_LIMIT)


def _rms(x, g):
    return x * lax.rsqrt(jnp.mean(x * x, axis=-1, keepdims=True) + EPS) * g


def _dot(a, b):
    return jnp.dot(a, b, preferred_element_type=F32)


def _dot_nt(a, b):
    return lax.dot_general(a, b, (((1,), (1,)), ((), ())), preferred_element_type=F32)


def _rope_kernel(pos_ref, pat_ref, c_ref, s_ref):
    pos = pos_ref[...].astype(F32)
    ang = pos * pat_ref[0:1, :]
    c_ref[...] = pat_ref[1:2, :] + pat_ref[2:3, :] * jnp.cos(ang)
    s_ref[...] = pat_ref[3:4, :] * jnp.sin(ang)


def _rope_tables(positions, tm):
    t = positions.size
    half = MLA_ROPE // 2
    inv_freq = ROPE_THETA ** (-jnp.arange(half, dtype=F32) / half)
    z64, z32 = jnp.zeros((MLA_NOPE,), F32), jnp.zeros((HEAD_PAD - MLA_NOPE - MLA_ROPE,), F32)
    o16 = jnp.ones((half,), F32)
    pat = jnp.stack([
        jnp.concatenate([z64, inv_freq, inv_freq, z32]),
        jnp.concatenate([z64 + 1.0, 0 * o16, 0 * o16, z32]),
        jnp.concatenate([z64, o16, o16, z32]),
        jnp.concatenate([z64, -o16, o16, z32]),
    ])
    pat = jnp.concatenate([pat, jnp.zeros((4, HEAD_PAD), F32)], axis=0)
    return pl.pallas_call(
        _rope_kernel,
        grid=(t // tm,),
        in_specs=[pl.BlockSpec((tm, 1), lambda i: (i, 0)), pl.BlockSpec((8, HEAD_PAD), lambda i: (0, 0))],
        out_specs=[pl.BlockSpec((tm, HEAD_PAD), lambda i: (i, 0))] * 2,
        out_shape=[jax.ShapeDtypeStruct((t, HEAD_PAD), F32)] * 2,
        compiler_params=_cparams(("arbitrary",)),
        name="rope_tables",
    )(positions.reshape(t, 1), pat)


def _prep_kernel(x_ref, g_ref, w_ref, qg_ref, wq_ref, wqs_ref, kvg_ref, wk_ref, wv_ref, c_ref, s_ref,
                 convw_ref, wgate_ref, bgate_ref,
                 mq_ref, mk_ref, mv_ref, oconv_ref, dq_ref, dkk_ref, dvv_ref, iq_ref, ikr_ref, iw_ref,
                 gq_ref, gk_ref, gv_ref, gg_ref, gr_ref, u_scr, *, tm):
    j = pl.program_id(1)
    h = _rms(x_ref[...], g_ref[...]).astype(BF16)
    p = _dot(h, w_ref[...])

    c1, s1 = c_ref[...], s_ref[...]
    c4 = jnp.concatenate([c1] * MLA_HEADS, axis=1)
    s4 = jnp.concatenate([s1] * MLA_HEADS, axis=1)

    qn = _rms(p[:, P_CQ:P_CQ + MLA_Q_RANK], qg_ref[...]).astype(BF16)
    q = _dot(qn, wq_ref[...]) * c4 + _dot(qn, wqs_ref[...]) * s4
    mq_ref[...] = (q * ((MLA_NOPE + MLA_ROPE) ** -0.5 * LOG2E)).astype(BF16)
    kvn = _rms(p[:, P_CKV:P_CKV + MLA_KV_RANK], kvg_ref[...]).astype(BF16)
    kr = p[:, P_KRP:P_KRP + HEAD_PAD] * c1 + p[:, P_KRS:P_KRS + HEAD_PAD] * s1
    mk_ref[...] = (_dot(kvn, wk_ref[...]) + jnp.concatenate([kr] * MLA_HEADS, axis=1)).astype(BF16)
    vv = _dot(kvn, wv_ref[...])
    ones = jnp.ones((tm, LANE), F32)
    mv_ref[...] = jnp.concatenate([vv[:, 0:LANE], ones, vv[:, LANE:2 * LANE], ones], axis=1).astype(BF16)

    u = p[:, P_CC:P_CC + CONV_WIDTH] * p[:, P_CX:P_CX + CONV_WIDTH]

    @pl.when(j == 0)
    def _():
        u_scr[0:8, :] = jnp.zeros((8, CONV_WIDTH), F32)

    u_scr[8:8 + tm, :] = u
    cw = convw_ref[...]
    y = cw[0:1, :] * u_scr[6:6 + tm, :] + cw[1:2, :] * u_scr[7:7 + tm, :] + cw[2:3, :] * u
    oconv_ref[...] = (p[:, P_CB:P_CB + CONV_WIDTH] * y).astype(BF16)
    u_scr[0:8, :] = u_scr[tm:tm + 8, :]

    dq_ref[...] = (p[:, P_DQ:P_DQ + 256] * (DSA_HEAD_DIM ** -0.5 * LOG2E)).astype(BF16)
    dkk_ref[...] = p[:, P_DKK:P_DKK + LANE].astype(BF16)
    hi_lanes = lax.broadcasted_iota(I32, (tm, LANE), 1) >= DSA_HEAD_DIM
    dvv_ref[...] = jnp.where(hi_lanes, 1.0, p[:, P_DVV:P_DVV + LANE]).astype(BF16)
    iq_ref[...] = p[:, P_IQ:P_IQ + 256].astype(BF16)
    ikr_ref[...] = p[:, P_IKR:P_IKR + LANE].astype(BF16)
    iw_ref[...] = p[:, P_IW:P_IW + LANE] * ((IDX_DIM ** -0.5) * (IDX_HEADS ** -0.5))

    gq_ref[...] = p[:, P_GQ:P_GQ + LANE] * (GLA_DK ** -0.5)
    gk_ref[...] = p[:, P_GK:P_GK + LANE]
    gv_ref[...] = p[:, P_GV:P_GV + 256]
    gate = _dot(p[:, P_GLR:P_GLR + LANE].astype(BF16), wgate_ref[...]) + bgate_ref[...]
    log_sig = jnp.minimum(gate, 0.0) - jnp.log1p(jnp.exp(-jnp.abs(gate)))
    gg_ref[...] = log_sig * (1.0 / GLA_TAU)
    gr = p[:, P_GR:P_GR + 256]
    gr_ref[...] = gr / (1.0 + jnp.exp(-gr))


def _pack_small_weight(w_in):
    def cols(a, b):
        return w_in[..., a:b]

    def zeros(n):
        return jnp.zeros(w_in.shape[:-1] + (n,), w_in.dtype)

    half = MLA_ROPE // 2
    kr1, kr2 = cols(O_KR, O_KR + half), cols(O_KR + half, O_KR + MLA_ROPE)
    pad_tail = HEAD_PAD - MLA_NOPE - MLA_ROPE
    pieces = [
        cols(O_CQ, O_CKV), cols(O_CKV, O_KR),
        zeros(MLA_NOPE), kr1, kr2, zeros(pad_tail),
        zeros(MLA_NOPE), kr2, kr1, zeros(pad_tail),
        cols(O_CB, O_CC), cols(O_CC, O_CX), cols(O_CX, O_DQ),
        cols(O_DQ, O_DK), cols(O_DK, O_DV), cols(O_DK, O_DV), cols(O_DV, O_IQ), zeros(DSA_HEAD_DIM),
        cols(O_IQ, O_IK), cols(O_IK, O_IW), cols(O_IK, O_IW), cols(O_IK, O_IW), cols(O_IK, O_IW),
        cols(O_IW, O_GQ), zeros(LANE - IDX_HEADS),
        cols(O_GQ, O_GK), cols(O_GK, O_GV), cols(O_GV, O_GLR),
        cols(O_GLR, O_GR), zeros(LANE - GLA_GATE_RANK),
        cols(O_GR, O_GATE),
    ]
    out = jnp.concatenate(pieces, axis=-1)
    assert out.shape[-1] == P_TOTAL, out.shape
    return out.astype(BF16)


def _pack_mla_weights(w_q_up, w_kv_up):
    half = MLA_ROPE // 2
    lq = w_q_up.shape[:-1]
    lk = w_kv_up.shape[:-1]
    wq = w_q_up.reshape(lq + (MLA_HEADS, MLA_NOPE + MLA_ROPE))
    nope, x1, x2 = wq[..., :MLA_NOPE], wq[..., MLA_NOPE:MLA_NOPE + half], wq[..., MLA_NOPE + half:]
    zq = jnp.zeros(lq + (MLA_HEADS, HEAD_PAD - MLA_NOPE - MLA_ROPE), w_q_up.dtype)
    wq_pad = jnp.concatenate([nope, x1, x2, zq], axis=-1).reshape(lq + (MLA_HEADS * HEAD_PAD,))
    wq_swap = jnp.concatenate([jnp.zeros_like(nope), x2, x1, zq], axis=-1).reshape(lq + (MLA_HEADS * HEAD_PAD,))
    wkv = w_kv_up.reshape(lk + (MLA_HEADS, MLA_NOPE + MLA_V))
    zk = jnp.zeros(lk + (MLA_HEADS, HEAD_PAD - MLA_NOPE), w_kv_up.dtype)
    wk_pad = jnp.concatenate([wkv[..., :MLA_NOPE], zk], axis=-1).reshape(lk + (MLA_HEADS * HEAD_PAD,))
    wv = wkv[..., MLA_NOPE:].reshape(lk + (MLA_HEADS * MLA_V,))
    return wq_pad.astype(BF16), wq_swap.astype(BF16), wk_pad.astype(BF16), wv.astype(BF16)


def _prep(x2, nb, g, w_small, qg, wq, wqs, kvg, wk, wv, ctab, stab, conv_w, wgate, bgate, tm):
    t, d = x2.shape
    nj = t // nb // tm

    def tok(width):
        return pl.BlockSpec((tm, width), lambda b, j: (b * nj + j, 0))

    def full(a):
        return pl.BlockSpec(a.shape, lambda b, j: (0,) * a.ndim, pipeline_mode=pl.Buffered(1))

    outs = [(512, BF16), (512, BF16), (512, BF16), (256, BF16), (256, BF16), (LANE, BF16), (LANE, BF16),
            (256, BF16), (LANE, BF16), (LANE, F32), (LANE, F32), (LANE, F32), (256, F32), (LANE, F32), (256, F32)]
    return pl.pallas_call(
        functools.partial(_prep_kernel, tm=tm),
        grid=(nb, nj),
        in_specs=[tok(d), full(g), full(w_small), full(qg), full(wq), full(wqs), full(kvg), full(wk), full(wv),
                  tok(HEAD_PAD), tok(HEAD_PAD), full(conv_w), full(wgate), full(bgate)],
        out_specs=[tok(w) for w, _ in outs],
        out_shape=[jax.ShapeDtypeStruct((t, w), dt) for w, dt in outs],
        scratch_shapes=[pltpu.VMEM((tm + 8, CONV_WIDTH), F32)],
        compiler_params=_cparams(("arbitrary", "arbitrary")),
        name="prep",
    )(x2, g, w_small, qg, wq, wqs, kvg, wk, wv, ctab, stab, conv_w, wgate, bgate)


def _fold_lanes(x, op):
    out = x[:, 0:LANE]
    for g in range(1, x.shape[1] // LANE):
        out = op(out, x[:, g * LANE:(g + 1) * LANE])
    return out


def _mla_kernel(q_ref, k_ref, v_ref, o_ref, s_scr, m_scr, acc_scr, *, tq, tk):
    i = pl.program_id(1)
    n_chunks = ((i + 1) * tq + tk - 1) // tk
    row = i * tq + lax.broadcasted_iota(I32, (tq, tk), 0)
    col = lax.broadcasted_iota(I32, (tq, tk), 1)
    lane = lax.broadcasted_iota(I32, (tq, LANE), 1)
    heads = range(MLA_HEADS)
    hs = [slice(h * HEAD_PAD, (h + 1) * HEAD_PAD) for h in heads]
    vs = [slice((h // 2) * 2 * LANE, (h // 2 + 1) * 2 * LANE) for h in heads]

    m_scr[...] = jnp.full(m_scr.shape, NEG, F32)
    acc_scr[...] = jnp.zeros(acc_scr.shape, F32)

    def phase1(c, carry, masked):
        off = pl.multiple_of(c * tk, tk)
        for h in heads:
            s = _dot_nt(q_ref[:, hs[h]], k_ref[pl.ds(off, tk), hs[h]])
            if masked:
                s = jnp.where(c * tk + col <= row, s, NEG)
            s_scr[h, c] = s
            m_scr[h] = jnp.maximum(m_scr[h], _fold_lanes(s, jnp.maximum))
        return carry

    lax.fori_loop(0, n_chunks - 1, functools.partial(phase1, masked=False), 0)
    phase1(n_chunks - 1, 0, True)
    for h in heads:
        m_scr[h] = jnp.broadcast_to(jnp.max(m_scr[h], axis=1, keepdims=True), (tq, LANE))

    def phase2(c, carry):
        off = pl.multiple_of(c * tk, tk)
        for h in heads:
            p = jnp.exp2(s_scr[h, c] - jnp.concatenate([m_scr[h]] * (tk // LANE), axis=1))
            acc_scr[h] = acc_scr[h] + _dot(p.astype(BF16), v_ref[pl.ds(off, tk), vs[h]])
        return carry

    lax.fori_loop(0, n_chunks, phase2, 0)
    res = [acc_scr[h, :, 0:LANE] / acc_scr[h, :, LANE:2 * LANE] for h in heads]
    for pair in range(MLA_HEADS // 2):
        o_ref[:, pair * LANE:(pair + 1) * LANE] = jnp.where(lane < MLA_V, res[2 * pair], res[2 * pair + 1]).astype(BF16)


def _mla(mq, mk, mv, nb, tq, tk):
    t = mq.shape[0]
    s = t // nb
    nq = s // tq
    return pl.pallas_call(
        functools.partial(_mla_kernel, tq=tq, tk=tk),
        grid=(nb, nq),
        in_specs=[pl.BlockSpec((tq, MLA_HEADS * HEAD_PAD), lambda b, i: (b * nq + i, 0)),
                  pl.BlockSpec((s, MLA_HEADS * HEAD_PAD), lambda b, i: (b, 0)),
                  pl.BlockSpec((s, MLA_HEADS * HEAD_PAD), lambda b, i: (b, 0))],
        out_specs=pl.BlockSpec((tq, MLA_HEADS * MLA_V), lambda b, i: (b * nq + i, 0)),
        out_shape=jax.ShapeDtypeStruct((t, MLA_HEADS * MLA_V), BF16),
        scratch_shapes=[pltpu.VMEM((MLA_HEADS, s // tk, tq, tk), F32),
                        pltpu.VMEM((MLA_HEADS, tq, LANE), F32),
                        pltpu.VMEM((MLA_HEADS, tq, 2 * LANE), F32)],
        compiler_params=_cparams(("arbitrary", "arbitrary")),
        name="mla_attn",
    )(mq, mk, mv)


def _dsa_kernel(iq_ref, iw_ref, ikr_ref, dq_ref, dkk_ref, dvv_ref, tri_ref, o_ref,
                key_scr, plane_scr, eq_scr, gt_scr, thr_scr, s_scr, m_scr, acc_scr,
                *, tq, th, tk, topk):
    i = pl.program_id(1)
    q_end = (i + 1) * tq
    n_chunks = (q_end + tk - 1) // tk
    lane_q = lax.broadcasted_iota(I32, (th, LANE), 1)
    lane_k = lax.broadcasted_iota(I32, (tq, tk), 1)
    int_min = jnp.int32(-2 ** 31)
    rb = 64
    lane_rb = lax.broadcasted_iota(I32, (rb, tk), 1)
    row_rb = lax.broadcasted_iota(I32, (rb, 1), 0)

    for r0 in range(0, tq, th):
        iq = iq_ref[r0:r0 + th, :].astype(F32)
        iw = iw_ref[r0:r0 + th, :]
        q_heads = []
        w_heads = []
        for h in range(IDX_HEADS):
            grp = h // 4
            sub = iq[:, grp * LANE:(grp + 1) * LANE]
            lo = (h % 4) * IDX_DIM
            in_head = (lane_q >= lo) & (lane_q < lo + IDX_DIM)
            q_heads.append(jnp.where(in_head, sub, 0.0).astype(BF16))
            w_heads.append(iw[:, h:h + 1])
        q_stacks = [jnp.concatenate([qh[r:r + rb] for qh in q_heads], axis=0) for r in range(0, th, rb)]

        def score_chunk(c, _, r0=r0, q_stacks=q_stacks, w_heads=w_heads):
            off = pl.multiple_of(c * tk, tk)
            kc = ikr_ref[pl.ds(off, tk), :]
            for bi, r in enumerate(range(0, th, rb)):
                z = _dot_nt(q_stacks[bi], kc)
                acc = w_heads[0][r:r + rb] * jnp.maximum(z[0:rb], 0.0)
                for h in range(1, IDX_HEADS):
                    acc = acc + w_heads[h][r:r + rb] * jnp.maximum(z[h * rb:(h + 1) * rb], 0.0)
                bits = pltpu.bitcast(acc, I32)
                key = jnp.where(bits < 0, bits ^ jnp.int32(0x7FFFFFFF), bits)
                key = jnp.where(c * tk + lane_rb <= i * tq + r0 + r + row_rb, key, int_min)
                key_scr[c, r0 + r:r0 + r + rb, :] = key
            return 0

        lax.fori_loop(0, n_chunks, score_chunk, 0)

    kf = jnp.float32(topk)
    n_groups = key_scr.shape[0] * (tk // LANE)
    cb = 128

    def row_total(cnt):
        return _dot(cnt.astype(F32).astype(BF16), jnp.ones((LANE, LANE), BF16))

    def select():
        def fill(c, _):
            key_scr[c] = jnp.full((tq, tk), int_min, I32)
            return 0
        lax.fori_loop(n_chunks, key_scr.shape[0], fill, 0)

        def build_planes(blk, _):
            r = pl.multiple_of(blk * 8, 8)
            w = []
            for g in range(32):
                if g < n_groups:
                    gc, gl = divmod(g, tk // LANE)
                    w.append(key_scr[gc, pl.ds(r, 8), gl * LANE:(gl + 1) * LANE] ^ int_min)
                else:
                    w.append(jnp.zeros((8, LANE), I32))
            j, m = 16, 0x0000FFFF
            while j:
                mm = jnp.int32(m - (1 << 32) if m >= (1 << 31) else m)
                k = 0
                while k < 32:
                    t = (w[k] ^ lax.shift_right_logical(w[k + j], jnp.int32(j))) & mm
                    w[k] = w[k] ^ t
                    w[k + j] = w[k + j] ^ lax.shift_left(t, jnp.int32(j))
                    k = (k + j + 1) & ~j
                j >>= 1
                m = (m ^ (m << j)) & 0xFFFFFFFF
            for b in range(32):
                plane_scr[b, pl.ds(r, 8), :] = w[b]
            return 0
        lax.fori_loop(0, tq // 8, build_planes, 0)

        eq_scr[...] = jnp.full((tq, LANE), -1, I32)
        gt_scr[...] = jnp.zeros((tq, LANE), I32)
        thr_scr[...] = jnp.zeros((tq, LANE), I32)

        def bit_pass(it, _):
            bit = lax.shift_left(jnp.int32(1), 31 - it)
            for r in range(0, tq, cb):
                rows = slice(r, r + cb)
                eq, gtm = eq_scr[rows, :], gt_scr[rows, :]
                t = eq & plane_scr[it, rows, :]
                u = gtm | t
                take = row_total(lax.population_count(u)) >= kf
                eq_scr[rows, :] = jnp.where(take, t, eq ^ t)
                gt_scr[rows, :] = jnp.where(take, gtm, u)
                thr_scr[rows, :] = jnp.where(take, thr_scr[rows, :] | bit, thr_scr[rows, :])
            return 0
        lax.fori_loop(0, 32, bit_pass, 0)

        thr_u = thr_scr[:, 0:1]
        thr = jnp.maximum(thr_u ^ int_min, int_min + 1)
        c_gt = row_total(lax.population_count(gt_scr[...]))[:, 0:1]
        c_ge = c_gt + row_total(lax.population_count(eq_scr[...]))[:, 0:1]
        excess = (c_ge > kf) & (thr_u != 0)

        def tie_break():
            need = kf - c_gt

            def demote(c, seen):
                kk = key_scr[c]
                tie = kk == thr
                rank = seen + _dot(jnp.where(tie, 1.0, 0.0).astype(BF16), tri_ref[...])
                key_scr[c] = jnp.where(tie & (rank > need), thr - 1, kk)
                return rank[:, tk - 1:tk]
            lax.fori_loop(0, n_chunks, demote, jnp.zeros((tq, 1), F32))

        any_excess = jnp.max(jnp.where(excess, 1.0, 0.0)) > 0.0
        lax.cond(any_excess, tie_break, lambda: None)
        return thr

    thr = lax.cond(q_end > topk, select, lambda: jnp.full((tq, 1), int_min + 1, I32))

    heads = range(DSA_HEADS)
    for r0 in range(0, tq, th):
        rows = slice(r0, r0 + th)
        thr_h = thr[rows]
        dq = dq_ref[rows, :].astype(F32)
        qm = []
        for h in heads:
            sub = dq[:, (h // 2) * LANE:(h // 2 + 1) * LANE]
            lo = (h % 2) * DSA_HEAD_DIM
            in_head = (lane_q >= lo) & (lane_q < lo + DSA_HEAD_DIM)
            qm.append(jnp.where(in_head, sub, 0.0).astype(BF16))
        qm_stack = jnp.concatenate(qm, axis=0)

        m_scr[...] = jnp.full(m_scr.shape, NEG, F32)
        acc_scr[...] = jnp.zeros(acc_scr.shape, F32)

        def phase1(c, carry, rows=rows, thr_h=thr_h, qm_stack=qm_stack):
            off = pl.multiple_of(c * tk, tk)
            sel = key_scr[c, rows, :] >= thr_h
            s = _dot_nt(qm_stack, dkk_ref[pl.ds(off, tk), :])
            for h in heads:
                sm = jnp.where(sel, s[h * th:(h + 1) * th], NEG)
                s_scr[h, c] = sm
                m_scr[h] = jnp.maximum(m_scr[h], _fold_lanes(sm, jnp.maximum))
            return carry

        lax.fori_loop(0, n_chunks, phase1, 0)
        for h in heads:
            m_scr[h] = jnp.broadcast_to(jnp.max(m_scr[h], axis=1, keepdims=True), (th, LANE))

        def phase2(c, carry):
            off = pl.multiple_of(c * tk, tk)
            ps = []
            for h in heads:
                p = jnp.exp2(s_scr[h, c] - jnp.concatenate([m_scr[h]] * (tk // LANE), axis=1))
                ps.append(p.astype(BF16))
            pv = _dot(jnp.concatenate(ps, axis=0), dvv_ref[pl.ds(off, tk), :])
            for h in heads:
                acc_scr[h] = acc_scr[h] + pv[h * th:(h + 1) * th]
            return carry

        lax.fori_loop(0, n_chunks, phase2, 0)
        outs = [acc_scr[h] / pltpu.roll(acc_scr[h], DSA_HEAD_DIM, 1) for h in heads]
        lo_half = lane_q < DSA_HEAD_DIM
        o_ref[rows, :] = jnp.concatenate(
            [jnp.where(lo_half, outs[0], pltpu.roll(outs[1], DSA_HEAD_DIM, 1)),
             jnp.where(lo_half, outs[2], pltpu.roll(outs[3], DSA_HEAD_DIM, 1))], axis=1).astype(BF16)


def _dsa(iq, iw, ikr, dq, dkk, dvv, nb, tq, th, tk):
    t = iq.shape[0]
    s = t // nb
    nq = s // tq
    topk = min(DSA_TOPK_MAX, s // 4)
    assert s // LANE <= 32, "the bit-sliced selection packs one key per 128-lane group into a 32-bit word"
    tri = jnp.asarray(np.triu(np.ones((tk, tk), np.float32)), BF16)

    def qspec(w):
        return pl.BlockSpec((tq, w), lambda b, i: (b * nq + i, 0))

    def kspec(w):
        return pl.BlockSpec((s, w), lambda b, i: (b, 0))

    return pl.pallas_call(
        functools.partial(_dsa_kernel, tq=tq, th=th, tk=tk, topk=topk),
        grid=(nb, nq),
        in_specs=[qspec(256), qspec(LANE), kspec(LANE), qspec(256), kspec(LANE), kspec(LANE),
                  pl.BlockSpec((tk, tk), lambda b, i: (0, 0), pipeline_mode=pl.Buffered(1))],
        out_specs=qspec(256),
        out_shape=jax.ShapeDtypeStruct((t, DSA_HEADS * DSA_HEAD_DIM), BF16),
        scratch_shapes=[pltpu.VMEM((s // tk, tq, tk), I32),
                        pltpu.VMEM((32, tq, LANE), I32),
                        pltpu.VMEM((tq, LANE), I32),
                        pltpu.VMEM((tq, LANE), I32),
                        pltpu.VMEM((tq, LANE), I32),
                        pltpu.VMEM((DSA_HEADS, s // tk, th, tk), F32)] + [pltpu.VMEM((DSA_HEADS, th, LANE), F32)] * 2,
        compiler_params=_cparams(("arbitrary", "arbitrary")),
        name="dsa",
    )(iq, iw, ikr, dq, dkk, dvv, tri)


def _gla_kernel(gq_ref, gk_ref, gv_ref, gg_ref, gr_ref, ng_ref, segv_ref, seg64_ref, bmask_ref, o_ref,
                st_scr, kbuf, cbuf, vbuf, *, tg):
    j = pl.program_id(1)
    ch = GLA_CHUNK
    rowc = lax.broadcasted_iota(I32, (tg, 1), 0) % ch

    @pl.when(j == 0)
    def _():
        st_scr[...] = jnp.zeros_like(st_scr)
        kbuf[0:ch, :] = jnp.zeros((ch, LANE), F32)
        cbuf[0:ch, :] = jnp.zeros((ch, LANE), F32)
        vbuf[0:ch, :] = jnp.zeros((ch, 2 * LANE), F32)

    cbuf[ch:ch + tg, :] = gg_ref[...]
    sh = 1
    while sh < ch:
        cur = cbuf[ch:ch + tg, :]
        prev = cbuf[ch - sh:ch - sh + tg, :]
        cbuf[ch:ch + tg, :] = cur + jnp.where(rowc >= sh, prev, 0.0)
        sh *= 2
    cum = cbuf[ch:ch + tg, :]
    q = gq_ref[...]
    k = gk_ref[...]
    v = gv_ref[...]
    kbuf[ch:ch + tg, :] = k
    vbuf[ch:ch + tg, :] = v

    qd = (q * jnp.exp(cum)).astype(BF16)
    bmask = bmask_ref[...]
    inter = []
    for c in range(tg // ch):
        rs = slice(c * ch, (c + 1) * ch)
        st = st_scr[...]
        inter.append(_dot_nt(qd[rs], st.astype(BF16)))
        last = cum[c * ch + ch - 1:c * ch + ch, :]
        kp = k[rs] * jnp.exp(last - cum[rs])
        upd = lax.dot_general(v[rs].astype(BF16), kp.astype(BF16), (((0,), (0,)), ((), ())),
                              preferred_element_type=F32)
        st_scr[...] = jnp.exp(last) * st + upd * bmask
    o = jnp.concatenate(inter, axis=0)

    segv = segv_ref[...]
    cum2 = cum * LOG2E
    cbuf[ch:ch + tg, :] = cum2
    sub = 16
    rows_sub = rowc % sub
    for d in range(sub):
        kd = kbuf[ch - d:ch - d + tg, :]
        cd = cbuf[ch - d:ch - d + tg, :]
        vd = vbuf[ch - d:ch - d + tg, :]
        e = jnp.exp2(jnp.where(rows_sub >= d, cum2 - cd, NEG))
        o = o + _dot((q * kd * e).astype(BF16), segv) * vd

    lane_k3 = lax.broadcasted_iota(I32, (1, 3 * LANE), 1) % LANE // GLA_DK
    lane_v = lax.broadcasted_iota(I32, (1, 2 * LANE), 1) // GLA_DV
    far = []
    for c in range(tg // ch):
        rs = slice(c * ch, (c + 1) * ch)
        qc, kc, cc, sb = q[rs], k[rs], cum2[rs], rowc[rs] // sub
        qh, kh = [], []
        for blk in range(1, ch // sub):
            ref_row = cc[blk * sub - 1:blk * sub, :]
            qh.append(qc * jnp.exp2(jnp.where(sb == blk, cc - ref_row, NEG)))
            kh.append(kc * jnp.exp2(jnp.where(sb < blk, ref_row - cc, NEG)))
        qh = jnp.concatenate(qh, axis=1)
        kh = jnp.concatenate(kh, axis=1).astype(BF16)
        q_heads = jnp.concatenate([jnp.where(lane_k3 == h, qh, 0.0) for h in range(GLA_HEADS)], axis=0)
        att = _dot_nt(q_heads.astype(BF16), kh)
        ov = _dot(att.astype(BF16), v[rs].astype(BF16))
        oc = jnp.where(lane_v == 0, ov[0:ch], 0.0)
        for h in range(1, GLA_HEADS):
            oc = oc + jnp.where(lane_v == h, ov[h * ch:(h + 1) * ch], 0.0)
        far.append(oc)
    o = o + jnp.concatenate(far, axis=0)

    sq = o * o
    hi = sq.astype(BF16)
    lo = (sq - hi.astype(F32)).astype(BF16)
    var = _dot(hi, seg64_ref[...]) + _dot(lo, seg64_ref[...])
    o_ref[...] = (gr_ref[...] * (o * lax.rsqrt(var + EPS) * ng_ref[...])).astype(BF16)


def _gla(gq, gk, gv, gg, gr, norm_g, nb, tg):
    t = gq.shape[0]
    nj = t // nb // tg
    ng = jnp.tile(norm_g.reshape(1, GLA_DV), (1, GLA_HEADS)).astype(F32)
    hk = np.arange(GLA_HEADS * GLA_DK) // GLA_DK
    hv = np.arange(GLA_HEADS * GLA_DV) // GLA_DV
    same = (hk[:, None] == hv[None, :])
    segv = jnp.asarray(same, BF16)
    bmask = jnp.asarray(same.T, F32)
    seg64 = jnp.asarray((hv[:, None] == hv[None, :]) / GLA_DV, BF16)

    def tok(w):
        return pl.BlockSpec((tg, w), lambda b, j: (b * nj + j, 0))

    def full(a):
        return pl.BlockSpec(a.shape, lambda b, j: (0,) * a.ndim, pipeline_mode=pl.Buffered(1))

    return pl.pallas_call(
        functools.partial(_gla_kernel, tg=tg),
        grid=(nb, nj),
        in_specs=[tok(LANE), tok(LANE), tok(256), tok(LANE), tok(256), full(ng), full(segv), full(seg64), full(bmask)],
        out_specs=tok(256),
        out_shape=jax.ShapeDtypeStruct((t, GLA_HEADS * GLA_DV), BF16),
        scratch_shapes=[pltpu.VMEM((2 * LANE, LANE), F32),
                        pltpu.VMEM((GLA_CHUNK + tg, LANE), F32),
                        pltpu.VMEM((GLA_CHUNK + tg, LANE), F32),
                        pltpu.VMEM((GLA_CHUNK + tg, 2 * LANE), F32)],
        compiler_params=_cparams(("arbitrary", "arbitrary")),
        name="gla",
    )(gq, gk, gv, gg, gr, ng, segv, seg64, bmask)


def _merge_kernel(x_ref, g_ref, wg_ref, o0_ref, o1_ref, o2_ref, o3_ref, wb_ref, wo_ref, out_ref):
    x = x_ref[...]
    h = _rms(x, g_ref[...]).astype(BF16)
    merged = None
    for n, o_ref in enumerate((o0_ref, o1_ref, o2_ref, o3_ref)):
        gate = _dot(h, wg_ref[:, n * D_MODEL:(n + 1) * D_MODEL])
        gate = 1.0 / (1.0 + jnp.exp(-gate))
        term = gate * _dot(o_ref[...], wb_ref[n])
        merged = term if merged is None else merged + term
    out_ref[...] = x + _dot(merged.astype(BF16), wo_ref[...])


def _merge(x2, g, wg, branches, wb, wo, tm):
    t, d = x2.shape

    def tok(w):
        return pl.BlockSpec((tm, w), lambda i: (i, 0))

    def full(a):
        return pl.BlockSpec(a.shape, lambda i: (0,) * a.ndim, pipeline_mode=pl.Buffered(1))

    return pl.pallas_call(
        _merge_kernel,
        grid=(t // tm,),
        in_specs=[tok(d), full(g), full(wg)] + [tok(BRANCH_WIDTH)] * N_BRANCH + [full(wb), full(wo)],
        out_specs=tok(d),
        out_shape=jax.ShapeDtypeStruct((t, d), F32),
        compiler_params=_cparams(("arbitrary",)),
        name="merge",
    )(x2, g, wg, *branches, wb, wo)


def _ffn_kernel(x_ref, g_ref, wgu_ref, wd_ref, fg_ref, out_ref, *, fc, final):
    x = x_ref[...]
    h = _rms(x, g_ref[...]).astype(BF16)
    acc = x
    for c in range(D_FF // fc):
        gate = _dot(h, wgu_ref[0, :, c * fc:(c + 1) * fc])
        up = _dot(h, wgu_ref[1, :, c * fc:(c + 1) * fc])
        act = (gate / (1.0 + jnp.exp(-gate)) * up).astype(BF16)
        acc = acc + _dot(act, wd_ref[c * fc:(c + 1) * fc, :])
    if final:
        acc = _rms(acc, fg_ref[...])
    out_ref[...] = acc


def _ffn(x2, g, wgu, wd, fg, tm, final):
    t, d = x2.shape

    def tok(w):
        return pl.BlockSpec((tm, w), lambda i: (i, 0))

    def full(a):
        return pl.BlockSpec(a.shape, lambda i: (0,) * a.ndim, pipeline_mode=pl.Buffered(1))

    return pl.pallas_call(
        functools.partial(_ffn_kernel, fc=256, final=final),
        grid=(t // tm,),
        in_specs=[tok(d), full(g), full(wgu), full(wd), full(fg)],
        out_specs=tok(d),
        out_shape=jax.ShapeDtypeStruct((t, d), F32),
        compiler_params=_cparams(("arbitrary",)),
        name="ffn",
    )(x2, g, wgu, wd, fg)


def _tile(n, pref):
    while n % pref:
        pref //= 2
    return pref


def kernel(x, positions, attn_norm_g, w_in, mla_q_norm_g, mla_w_q_up, mla_kv_norm_g, mla_w_kv_up, conv_w, gla_w_gate_up, gla_b_gate, gla_norm_g, w_branch, w_out, ffn_norm_g, w_ffn_gate, w_ffn_up, w_ffn_down, final_norm_g):
    b, s, d = x.shape
    depth = w_in.shape[0]
    t = b * s
    assert d == D_MODEL and s % GLA_CHUNK == 0
    tm_prep = _tile(s, 512)
    tm_wide = _tile(t, 512)
    tq_mla = _tile(s, 256)
    tk_att = _tile(s, 512)
    tq_dsa = _tile(s, 512)
    th_dsa = _tile(tq_dsa, 256)
    tg = _tile(s, 256)

    x2 = x.reshape(t, d)
    ctab, stab = _rope_tables(positions, _tile(t, 512))
    row = lambda a: a.reshape(1, -1).astype(F32)

    w_small = _pack_small_weight(w_in)
    wq, wqs, wk, wv = _pack_mla_weights(mla_w_q_up, mla_w_kv_up)
    wgate = jnp.concatenate([gla_w_gate_up, jnp.zeros((depth, LANE - GLA_GATE_RANK, GLA_HEADS * GLA_DK), F32)],
                            axis=1).astype(BF16)
    w_gates = w_in[:, :, O_GATE:].astype(BF16)
    w_branch_b, w_out_b = w_branch.astype(BF16), w_out.astype(BF16)
    wgu = jnp.stack([w_ffn_gate, w_ffn_up], axis=1).astype(BF16)
    w_down_b = w_ffn_down.astype(BF16)

    for l in range(depth):
        (mq, mk, mv, o_conv, dq, dkk, dvv, iq, ikr, iw, gq, gk, gv, gg, gr) = _prep(
            x2, b, row(attn_norm_g[l]), w_small[l], row(mla_q_norm_g[l]), wq[l], wqs[l], row(mla_kv_norm_g[l]),
            wk[l], wv[l], ctab, stab, conv_w[l].astype(F32), wgate[l], row(gla_b_gate[l]), tm_prep)
        o_mla = _mla(mq, mk, mv, b, tq_mla, tk_att)
        o_dsa = _dsa(iq, iw, ikr, dq, dkk, dvv, b, tq_dsa, th_dsa, tk_att)
        o_gla = _gla(gq, gk, gv, gg, gr, gla_norm_g[l], b, tg)
        x2 = _merge(x2, row(attn_norm_g[l]), w_gates[l], (o_mla, o_conv, o_dsa, o_gla), w_branch_b[l], w_out_b[l],
                    tm_wide)
        x2 = _ffn(x2, row(ffn_norm_g[l]), wgu[l], w_down_b[l], row(final_norm_g), tm_wide, final=(l == depth - 1))
    return x2.reshape(b, s, d)
```

```python
import functools

import jax
import jax.numpy as jnp
import numpy as np
from jax import lax
from jax.experimental import pallas as pl
from jax.experimental.pallas import tpu as pltpu

F32 = jnp.float32
BF16 = jnp.bfloat16
I32 = jnp.int32
I16 = jnp.int16

D_MODEL = 1024
MLA_HEADS = 4
MLA_Q_RANK = 256
MLA_KV_RANK = 128
MLA_NOPE = 64
MLA_ROPE = 32
MLA_V = 64
ROPE_THETA = 10000.0
CONV_WIDTH = 256
CONV_K = 3
DSA_HEADS = 4
DSA_HEAD_DIM = 64
IDX_HEADS = 8
IDX_DIM = 32
DSA_TOPK_MAX = 256
GLA_HEADS = 4
GLA_DK = 32
GLA_DV = 64
GLA_GATE_RANK = 16
GLA_TAU = 16.0
GLA_CHUNK = 64
N_BRANCH = 4
BRANCH_WIDTH = 256
D_FF = ((8 * D_MODEL + 3 * 256 - 1) // (3 * 256)) * 256
EPS = 1e-6
NEG = -1e30
LOG2E = 1.4426950408889634

LANE = 128
HEAD_PAD = 128

IN_SIZES = (
    MLA_Q_RANK, MLA_KV_RANK, MLA_ROPE,
    CONV_WIDTH, CONV_WIDTH, CONV_WIDTH,
    DSA_HEADS * DSA_HEAD_DIM, DSA_HEAD_DIM, DSA_HEAD_DIM,
    IDX_HEADS * IDX_DIM, IDX_DIM, IDX_HEADS,
    GLA_HEADS * GLA_DK, GLA_HEADS * GLA_DK, GLA_HEADS * GLA_DV,
    GLA_GATE_RANK, GLA_HEADS * GLA_DV,
    N_BRANCH * D_MODEL,
)
_OFF = np.concatenate([[0], np.cumsum(IN_SIZES)]).astype(int)
(O_CQ, O_CKV, O_KR, O_CB, O_CC, O_CX, O_DQ, O_DK, O_DV, O_IQ, O_IK, O_IW,
 O_GQ, O_GK, O_GV, O_GLR, O_GR, O_GATE, O_END) = [int(v) for v in _OFF]

P_CQ, P_CKV, P_KRP, P_KRS = 0, 256, 384, 512
P_CB, P_CC, P_CX = 640, 896, 1152
P_DQ, P_DKK, P_DVV = 1408, 1664, 1792
P_IQ, P_IKR, P_IW = 1920, 2176, 2304
P_GQ, P_GK, P_GV, P_GLR, P_GR = 2432, 2560, 2688, 2944, 3072
P_TOTAL = 3328

VMEM_LIMIT = 56 * 1024 * 1024


def _cparams(sem):
    return pltpu.CompilerParams(dimension_semantics=sem, vmem_limit_bytes=VMEM_LIMIT)


def _rms(x, g):
    return x * lax.rsqrt(jnp.mean(x * x, axis=-1, keepdims=True) + EPS) * g


def _dot(a, b):
    return jnp.dot(a, b, preferred_element_type=F32)


def _dot_nt(a, b):
    return lax.dot_general(a, b, (((1,), (1,)), ((), ())), preferred_element_type=F32)


def _rope_kernel(pos_ref, pat_ref, c_ref, s_ref):
    pos = pos_ref[...].astype(F32)
    ang = pos * pat_ref[0:1, :]
    c_ref[...] = pat_ref[1:2, :] + pat_ref[2:3, :] * jnp.cos(ang)
    s_ref[...] = pat_ref[3:4, :] * jnp.sin(ang)


def _rope_tables(positions, tm):
    t = positions.size
    half = MLA_ROPE // 2
    inv_freq = ROPE_THETA ** (-jnp.arange(half, dtype=F32) / half)
    z64, z32 = jnp.zeros((MLA_NOPE,), F32), jnp.zeros((HEAD_PAD - MLA_NOPE - MLA_ROPE,), F32)
    o16 = jnp.ones((half,), F32)
    pat = jnp.stack([
        jnp.concatenate([z64, inv_freq, inv_freq, z32]),
        jnp.concatenate([z64 + 1.0, 0 * o16, 0 * o16, z32]),
        jnp.concatenate([z64, o16, o16, z32]),
        jnp.concatenate([z64, -o16, o16, z32]),
    ])
    pat = jnp.concatenate([pat, jnp.zeros((4, HEAD_PAD), F32)], axis=0)
    return pl.pallas_call(
        _rope_kernel,
        grid=(t // tm,),
        in_specs=[pl.BlockSpec((tm, 1), lambda i: (i, 0)), pl.BlockSpec((8, HEAD_PAD), lambda i: (0, 0))],
        out_specs=[pl.BlockSpec((tm, HEAD_PAD), lambda i: (i, 0))] * 2,
        out_shape=[jax.ShapeDtypeStruct((t, HEAD_PAD), F32)] * 2,
        compiler_params=_cparams(("arbitrary",)),
        name="rope_tables",
    )(positions.reshape(t, 1), pat)


def _prep_kernel(x_ref, g_ref, w_ref, qg_ref, wq_ref, wqs_ref, kvg_ref, wk_ref, wv_ref, c_ref, s_ref,
                 convw_ref, wgate_ref, bgate_ref,
                 mq_ref, mk_ref, mv_ref, oconv_ref, dq_ref, dkk_ref, dvv_ref, iq_ref, ikr_ref, iw_ref,
                 gq_ref, gk_ref, gv_ref, gg_ref, gr_ref, u_scr, *, tm):
    j = pl.program_id(1)
    h = _rms(x_ref[...], g_ref[...]).astype(BF16)
    p = _dot(h, w_ref[...])

    c1, s1 = c_ref[...], s_ref[...]
    c4 = jnp.concatenate([c1] * MLA_HEADS, axis=1)
    s4 = jnp.concatenate([s1] * MLA_HEADS, axis=1)

    qn = _rms(p[:, P_CQ:P_CQ + MLA_Q_RANK], qg_ref[...]).astype(BF16)
    q = _dot(qn, wq_ref[...]) * c4 + _dot(qn, wqs_ref[...]) * s4
    mq_ref[...] = (q * ((MLA_NOPE + MLA_ROPE) ** -0.5 * LOG2E)).astype(BF16)
    kvn = _rms(p[:, P_CKV:P_CKV + MLA_KV_RANK], kvg_ref[...]).astype(BF16)
    kr = p[:, P_KRP:P_KRP + HEAD_PAD] * c1 + p[:, P_KRS:P_KRS + HEAD_PAD] * s1
    mk_ref[...] = (_dot(kvn, wk_ref[...]) + jnp.concatenate([kr] * MLA_HEADS, axis=1)).astype(BF16)
    vv = _dot(kvn, wv_ref[...])
    ones = jnp.ones((tm, LANE), F32)
    mv_ref[...] = jnp.concatenate([vv[:, 0:LANE], ones, vv[:, LANE:2 * LANE], ones], axis=1).astype(BF16)

    u = p[:, P_CC:P_CC + CONV_WIDTH] * p[:, P_CX:P_CX + CONV_WIDTH]

    @pl.when(j == 0)
    def _():
        u_scr[0:8, :] = jnp.zeros((8, CONV_WIDTH), F32)

    u_scr[8:8 + tm, :] = u
    cw = convw_ref[...]
    y = cw[0:1, :] * u_scr[6:6 + tm, :] + cw[1:2, :] * u_scr[7:7 + tm, :] + cw[2:3, :] * u
    oconv_ref[...] = (p[:, P_CB:P_CB + CONV_WIDTH] * y).astype(BF16)
    u_scr[0:8, :] = u_scr[tm:tm + 8, :]

    dq_ref[...] = (p[:, P_DQ:P_DQ + 256] * (DSA_HEAD_DIM ** -0.5 * LOG2E)).astype(BF16)
    dkk_ref[...] = p[:, P_DKK:P_DKK + LANE].astype(BF16)
    hi_lanes = lax.broadcasted_iota(I32, (tm, LANE), 1) >= DSA_HEAD_DIM
    dvv_ref[...] = jnp.where(hi_lanes, 1.0, p[:, P_DVV:P_DVV + LANE]).astype(BF16)
    iq_ref[...] = p[:, P_IQ:P_IQ + 256].astype(BF16)
    ikr_ref[...] = p[:, P_IKR:P_IKR + LANE].astype(BF16)
    iw_ref[...] = p[:, P_IW:P_IW + LANE] * ((IDX_DIM ** -0.5) * (IDX_HEADS ** -0.5))

    gq_ref[...] = p[:, P_GQ:P_GQ + LANE] * (GLA_DK ** -0.5)
    gk_ref[...] = p[:, P_GK:P_GK + LANE]
    gv_ref[...] = p[:, P_GV:P_GV + 256]
    gate = _dot(p[:, P_GLR:P_GLR + LANE].astype(BF16), wgate_ref[...]) + bgate_ref[...]
    log_sig = jnp.minimum(gate, 0.0) - jnp.log1p(jnp.exp(-jnp.abs(gate)))
    gg_ref[...] = log_sig * (1.0 / GLA_TAU)
    gr = p[:, P_GR:P_GR + 256]
    gr_ref[...] = gr / (1.0 + jnp.exp(-gr))


def _pack_small_weight(w_in):
    def cols(a, b):
        return w_in[..., a:b]

    def zeros(n):
        return jnp.zeros(w_in.shape[:-1] + (n,), w_in.dtype)

    half = MLA_ROPE // 2
    kr1, kr2 = cols(O_KR, O_KR + half), cols(O_KR + half, O_KR + MLA_ROPE)
    pad_tail = HEAD_PAD - MLA_NOPE - MLA_ROPE
    pieces = [
        cols(O_CQ, O_CKV), cols(O_CKV, O_KR),
        zeros(MLA_NOPE), kr1, kr2, zeros(pad_tail),
        zeros(MLA_NOPE), kr2, kr1, zeros(pad_tail),
        cols(O_CB, O_CC), cols(O_CC, O_CX), cols(O_CX, O_DQ),
        cols(O_DQ, O_DK), cols(O_DK, O_DV), cols(O_DK, O_DV), cols(O_DV, O_IQ), zeros(DSA_HEAD_DIM),
        cols(O_IQ, O_IK), cols(O_IK, O_IW), cols(O_IK, O_IW), cols(O_IK, O_IW), cols(O_IK, O_IW),
        cols(O_IW, O_GQ), zeros(LANE - IDX_HEADS),
        cols(O_GQ, O_GK), cols(O_GK, O_GV), cols(O_GV, O_GLR),
        cols(O_GLR, O_GR), zeros(LANE - GLA_GATE_RANK),
        cols(O_GR, O_GATE),
    ]
    out = jnp.concatenate(pieces, axis=-1)
    assert out.shape[-1] == P_TOTAL, out.shape
    return out.astype(BF16)


def _pack_mla_weights(w_q_up, w_kv_up):
    half = MLA_ROPE // 2
    lq = w_q_up.shape[:-1]
    lk = w_kv_up.shape[:-1]
    wq = w_q_up.reshape(lq + (MLA_HEADS, MLA_NOPE + MLA_ROPE))
    nope, x1, x2 = wq[..., :MLA_NOPE], wq[..., MLA_NOPE:MLA_NOPE + half], wq[..., MLA_NOPE + half:]
    zq = jnp.zeros(lq + (MLA_HEADS, HEAD_PAD - MLA_NOPE - MLA_ROPE), w_q_up.dtype)
    wq_pad = jnp.concatenate([nope, x1, x2, zq], axis=-1).reshape(lq + (MLA_HEADS * HEAD_PAD,))
    wq_swap = jnp.concatenate([jnp.zeros_like(nope), x2, x1, zq], axis=-1).reshape(lq + (MLA_HEADS * HEAD_PAD,))
    wkv = w_kv_up.reshape(lk + (MLA_HEADS, MLA_NOPE + MLA_V))
    zk = jnp.zeros(lk + (MLA_HEADS, HEAD_PAD - MLA_NOPE), w_kv_up.dtype)
    wk_pad = jnp.concatenate([wkv[..., :MLA_NOPE], zk], axis=-1).reshape(lk + (MLA_HEADS * HEAD_PAD,))
    wv = wkv[..., MLA_NOPE:].reshape(lk + (MLA_HEADS * MLA_V,))
    return wq_pad.astype(BF16), wq_swap.astype(BF16), wk_pad.astype(BF16), wv.astype(BF16)


def _prep(x2, nb, g, w_small, qg, wq, wqs, kvg, wk, wv, ctab, stab, conv_w, wgate, bgate, tm):
    t, d = x2.shape
    nj = t // nb // tm

    def tok(width):
        return pl.BlockSpec((tm, width), lambda b, j: (b * nj + j, 0))

    def full(a):
        return pl.BlockSpec(a.shape, lambda b, j: (0,) * a.ndim, pipeline_mode=pl.Buffered(1))

    outs = [(512, BF16), (512, BF16), (512, BF16), (256, BF16), (256, BF16), (LANE, BF16), (LANE, BF16),
            (256, BF16), (LANE, BF16), (LANE, F32), (LANE, F32), (LANE, F32), (256, F32), (LANE, F32), (256, F32)]
    return pl.pallas_call(
        functools.partial(_prep_kernel, tm=tm),
        grid=(nb, nj),
        in_specs=[tok(d), full(g), full(w_small), full(qg), full(wq), full(wqs), full(kvg), full(wk), full(wv),
                  tok(HEAD_PAD), tok(HEAD_PAD), full(conv_w), full(wgate), full(bgate)],
        out_specs=[tok(w) for w, _ in outs],
        out_shape=[jax.ShapeDtypeStruct((t, w), dt) for w, dt in outs],
        scratch_shapes=[pltpu.VMEM((tm + 8, CONV_WIDTH), F32)],
        compiler_params=_cparams(("arbitrary", "arbitrary")),
        name="prep",
    )(x2, g, w_small, qg, wq, wqs, kvg, wk, wv, ctab, stab, conv_w, wgate, bgate)


def _for_chunks(n, body):
    def group(i, carry):
        for u in range(4):
            body(4 * i + u, carry)
        return carry

    lax.fori_loop(0, n // 4, group, 0)
    rem = n % 4

    @pl.when(rem >= 2)
    def _():
        body(n - rem, 0)
        body(n - rem + 1, 0)

    @pl.when(rem % 2 == 1)
    def _():
        body(n - 1, 0)


def _fold_lanes(x, op):
    out = x[:, 0:LANE]
    for g in range(1, x.shape[1] // LANE):
        out = op(out, x[:, g * LANE:(g + 1) * LANE])
    return out


def _mla_kernel(q_ref, k_ref, v_ref, o_ref, s_scr, m_scr, acc_scr, *, tq, tk):
    i = pl.program_id(1)
    n_chunks = ((i + 1) * tq + tk - 1) // tk
    row = i * tq + lax.broadcasted_iota(I32, (tq, tk), 0)
    col = lax.broadcasted_iota(I32, (tq, tk), 1)
    lane = lax.broadcasted_iota(I32, (tq, LANE), 1)
    heads = range(MLA_HEADS)
    hs = [slice(h * HEAD_PAD, (h + 1) * HEAD_PAD) for h in heads]
    vs = [slice((h // 2) * 2 * LANE, (h // 2 + 1) * 2 * LANE) for h in heads]

    m_scr[...] = jnp.full(m_scr.shape, NEG, F32)
    acc_scr[...] = jnp.zeros(acc_scr.shape, F32)

    def phase1(c, carry):
        off = pl.multiple_of(c * tk, tk)
        causal = c * tk + col <= row
        for h in heads:
            s = jnp.where(causal, _dot_nt(q_ref[:, hs[h]], k_ref[pl.ds(off, tk), hs[h]]), NEG)
            s_scr[h, c] = s
            m_scr[h] = jnp.maximum(m_scr[h], _fold_lanes(s, jnp.maximum))
        return carry

    _for_chunks(n_chunks, phase1)
    for h in heads:
        m_scr[h] = jnp.broadcast_to(jnp.max(m_scr[h], axis=1, keepdims=True), (tq, LANE))

    def phase2(c, carry):
        off = pl.multiple_of(c * tk, tk)
        for h in heads:
            p = jnp.exp2(s_scr[h, c] - jnp.concatenate([m_scr[h]] * (tk // LANE), axis=1))
            acc_scr[h] = acc_scr[h] + _dot(p.astype(BF16), v_ref[pl.ds(off, tk), vs[h]])
        return carry

    _for_chunks(n_chunks, phase2)
    res = [acc_scr[h, :, 0:LANE] / acc_scr[h, :, LANE:2 * LANE] for h in heads]
    for pair in range(MLA_HEADS // 2):
        o_ref[:, pair * LANE:(pair + 1) * LANE] = jnp.where(lane < MLA_V, res[2 * pair], res[2 * pair + 1]).astype(BF16)


def _mla(mq, mk, mv, nb, tq, tk):
    t = mq.shape[0]
    s = t // nb
    nq = s // tq
    return pl.pallas_call(
        functools.partial(_mla_kernel, tq=tq, tk=tk),
        grid=(nb, nq),
        in_specs=[pl.BlockSpec((tq, MLA_HEADS * HEAD_PAD), lambda b, i: (b * nq + i, 0)),
                  pl.BlockSpec((s, MLA_HEADS * HEAD_PAD), lambda b, i: (b, 0)),
                  pl.BlockSpec((s, MLA_HEADS * HEAD_PAD), lambda b, i: (b, 0))],
        out_specs=pl.BlockSpec((tq, MLA_HEADS * MLA_V), lambda b, i: (b * nq + i, 0)),
        out_shape=jax.ShapeDtypeStruct((t, MLA_HEADS * MLA_V), BF16),
        scratch_shapes=[pltpu.VMEM((MLA_HEADS, s // tk, tq, tk), F32),
                        pltpu.VMEM((MLA_HEADS, tq, LANE), F32),
                        pltpu.VMEM((MLA_HEADS, tq, 2 * LANE), F32)],
        compiler_params=_cparams(("arbitrary", "arbitrary")),
        name="mla_attn",
    )(mq, mk, mv)


def _dsa_kernel(iq_ref, iw_ref, ikr_ref, dq_ref, dkk_ref, dvv_ref, tri_ref, o_ref,
                key_scr, plane_scr, eq_scr, gt_scr, thr_scr, s_scr, m_scr, acc_scr,
                *, tq, th, tk, topk):
    i = pl.program_id(1)
    q_end = (i + 1) * tq
    n_chunks = (q_end + tk - 1) // tk
    lane_q = lax.broadcasted_iota(I32, (th, LANE), 1)
    lane_k = lax.broadcasted_iota(I32, (tq, tk), 1)
    int_min = jnp.int32(-2 ** 31)
    rb = 64
    lane_rb = lax.broadcasted_iota(I32, (rb, tk), 1)
    row_rb = lax.broadcasted_iota(I32, (rb, 1), 0)

    for r0 in range(0, tq, th):
        iq = iq_ref[r0:r0 + th, :].astype(F32)
        iw = iw_ref[r0:r0 + th, :]
        q_heads = []
        w_heads = []
        for h in range(IDX_HEADS):
            grp = h // 4
            sub = iq[:, grp * LANE:(grp + 1) * LANE]
            lo = (h % 4) * IDX_DIM
            in_head = (lane_q >= lo) & (lane_q < lo + IDX_DIM)
            q_heads.append(jnp.where(in_head, sub, 0.0).astype(BF16))
            w_heads.append(iw[:, h:h + 1])
        q_stacks = [jnp.concatenate([qh[r:r + rb] for qh in q_heads], axis=0) for r in range(0, th, rb)]

        def score_chunk(c, _, r0=r0, q_stacks=q_stacks, w_heads=w_heads):
            off = pl.multiple_of(c * tk, tk)
            kc = ikr_ref[pl.ds(off, tk), :]
            for bi, r in enumerate(range(0, th, rb)):
                z = _dot_nt(q_stacks[bi], kc)
                acc = w_heads[0][r:r + rb] * jnp.maximum(z[0:rb], 0.0)
                for h in range(1, IDX_HEADS):
                    acc = acc + w_heads[h][r:r + rb] * jnp.maximum(z[h * rb:(h + 1) * rb], 0.0)
                bits = pltpu.bitcast(acc, I32)
                key = jnp.where(bits < 0, bits ^ jnp.int32(0x7FFFFFFF), bits)
                key = jnp.where(c * tk + lane_rb <= i * tq + r0 + r + row_rb, key, int_min)
                key_scr[c, r0 + r:r0 + r + rb, :] = key
            return 0

        _for_chunks(n_chunks, score_chunk)

    kf = jnp.float32(topk)
    n_groups = key_scr.shape[0] * (tk // LANE)
    cb = 128

    def row_total(cnt):
        return _dot(cnt.astype(F32).astype(BF16), jnp.ones((LANE, LANE), BF16))

    def select():
        def fill(c, _):
            key_scr[c] = jnp.full((tq, tk), int_min, I32)
            return 0
        lax.fori_loop(n_chunks, key_scr.shape[0], fill, 0)

        stage_mask = {16: 0x0000FFFF, 8: 0x00FF00FF, 4: 0x0F0F0F0F, 2: 0x33333333, 1: 0x55555555}

        def build_planes(blk, _, nw):
            r = pl.multiple_of(blk * 8, 8)
            w = []
            for g in range(nw):
                if g < n_groups:
                    gc, gl = divmod(g, tk // LANE)
                    w.append(key_scr[gc, pl.ds(r, 8), gl * LANE:(gl + 1) * LANE] ^ int_min)
                else:
                    w.append(jnp.zeros((8, LANE), I32))
            j = nw // 2
            while j:
                k = 0
                while k < nw:
                    t = (w[k] ^ lax.shift_right_logical(w[k + j], jnp.int32(j))) & jnp.int32(stage_mask[j])
                    w[k] = w[k] ^ t
                    w[k + j] = w[k + j] ^ lax.shift_left(t, jnp.int32(j))
                    k = (k + j + 1) & ~j
                j >>= 1
            if nw == 16:
                w = [lax.shift_right_logical(u, jnp.int32(16)) for u in w] + [u & jnp.int32(0xFFFF) for u in w]
            for b in range(32):
                plane_scr[b, pl.ds(r, 8), :] = w[b]
            return 0

        few_groups = n_chunks * (tk // LANE) <= 16

        @pl.when(few_groups)
        def _():
            lax.fori_loop(0, tq // 8, functools.partial(build_planes, nw=16), 0)

        if n_groups > 16:
            @pl.when(jnp.logical_not(few_groups))
            def _():
                lax.fori_loop(0, tq // 8, functools.partial(build_planes, nw=32), 0)

        eq_scr[...] = jnp.full((tq, LANE), -1, I32)
        gt_scr[...] = jnp.zeros((tq, LANE), I32)
        thr_scr[...] = jnp.zeros((tq, LANE), I32)

        def bit_pass(it, _):
            bit = lax.shift_left(jnp.int32(1), 31 - it)
            for r in range(0, tq, cb):
                rows = slice(r, r + cb)
                eq, gtm = eq_scr[rows, :], gt_scr[rows, :]
                t = eq & plane_scr[it, rows, :]
                u = gtm | t
                take = row_total(lax.population_count(u)) >= kf
                eq_scr[rows, :] = jnp.where(take, t, eq ^ t)
                gt_scr[rows, :] = jnp.where(take, gtm, u)
                thr_scr[rows, :] = jnp.where(take, thr_scr[rows, :] | bit, thr_scr[rows, :])
            return 0
        lax.fori_loop(0, 32, bit_pass, 0)

        thr_u = thr_scr[:, 0:1]
        thr = jnp.maximum(thr_u ^ int_min, int_min + 1)
        c_gt = row_total(lax.population_count(gt_scr[...]))[:, 0:1]
        c_ge = c_gt + row_total(lax.population_count(eq_scr[...]))[:, 0:1]
        excess = (c_ge > kf) & (thr_u != 0)

        def tie_break():
            need = kf - c_gt

            def demote(c, seen):
                kk = key_scr[c]
                tie = kk == thr
                tie_b = jnp.where(tie, 1.0, 0.0).astype(BF16)
                rank = seen + jnp.concatenate([_dot(tie_b[r:r + th], tri_ref[...]) for r in range(0, tq, th)], axis=0)
                key_scr[c] = jnp.where(tie & (rank > need), thr - 1, kk)
                return rank[:, tk - 1:tk]
            lax.fori_loop(0, n_chunks, demote, jnp.zeros((tq, 1), F32))

        any_excess = jnp.max(jnp.where(excess, 1.0, 0.0)) > 0.0
        lax.cond(any_excess, tie_break, lambda: None)
        return thr

    thr = lax.cond(q_end > topk, select, lambda: jnp.full((tq, 1), int_min + 1, I32))

    heads = range(DSA_HEADS)
    for r0 in range(0, tq, th):
        rows = slice(r0, r0 + th)
        thr_h = thr[rows]
        dq = dq_ref[rows, :].astype(F32)
        qm = []
        for h in heads:
            sub = dq[:, (h // 2) * LANE:(h // 2 + 1) * LANE]
            lo = (h % 2) * DSA_HEAD_DIM
            in_head = (lane_q >= lo) & (lane_q < lo + DSA_HEAD_DIM)
            qm.append(jnp.where(in_head, sub, 0.0).astype(BF16))
        qm_stack = jnp.concatenate(qm, axis=0)

        m_scr[...] = jnp.full(m_scr.shape, NEG, F32)
        acc_scr[...] = jnp.zeros(acc_scr.shape, F32)

        def phase1(c, carry, rows=rows, thr_h=thr_h, qm_stack=qm_stack):
            off = pl.multiple_of(c * tk, tk)
            sel = key_scr[c, rows, :] >= thr_h
            s = _dot_nt(qm_stack, dkk_ref[pl.ds(off, tk), :])
            for h in heads:
                sm = jnp.where(sel, s[h * th:(h + 1) * th], NEG)
                s_scr[h, c] = sm
                m_scr[h] = jnp.maximum(m_scr[h], _fold_lanes(sm, jnp.maximum))
            return carry

        _for_chunks(n_chunks, phase1)
        for h in heads:
            m_scr[h] = jnp.broadcast_to(jnp.max(m_scr[h], axis=1, keepdims=True), (th, LANE))

        def phase2(c, carry):
            off = pl.multiple_of(c * tk, tk)
            ps = []
            for h in heads:
                p = jnp.exp2(s_scr[h, c] - jnp.concatenate([m_scr[h]] * (tk // LANE), axis=1))
                ps.append(p.astype(BF16))
            pv = _dot(jnp.concatenate(ps, axis=0), dvv_ref[pl.ds(off, tk), :])
            for h in heads:
                acc_scr[h] = acc_scr[h] + pv[h * th:(h + 1) * th]
            return carry

        _for_chunks(n_chunks, phase2)
        outs = [acc_scr[h] / pltpu.roll(acc_scr[h], DSA_HEAD_DIM, 1) for h in heads]
        lo_half = lane_q < DSA_HEAD_DIM
        o_ref[rows, :] = jnp.concatenate(
            [jnp.where(lo_half, outs[0], pltpu.roll(outs[1], DSA_HEAD_DIM, 1)),
             jnp.where(lo_half, outs[2], pltpu.roll(outs[3], DSA_HEAD_DIM, 1))], axis=1).astype(BF16)


def _dsa(iq, iw, ikr, dq, dkk, dvv, nb, tq, th, tk):
    t = iq.shape[0]
    s = t // nb
    nq = s // tq
    topk = min(DSA_TOPK_MAX, s // 4)
    assert s // LANE <= 32, "the bit-sliced selection packs one key per 128-lane group into a 32-bit word"
    tri = jnp.asarray(np.triu(np.ones((tk, tk), np.float32)), BF16)

    def qspec(w):
        return pl.BlockSpec((tq, w), lambda b, i: (b * nq + i, 0))

    def kspec(w):
        return pl.BlockSpec((s, w), lambda b, i: (b, 0))

    return pl.pallas_call(
        functools.partial(_dsa_kernel, tq=tq, th=th, tk=tk, topk=topk),
        grid=(nb, nq),
        in_specs=[qspec(256), qspec(LANE), kspec(LANE), qspec(256), kspec(LANE), kspec(LANE),
                  pl.BlockSpec((tk, tk), lambda b, i: (0, 0), pipeline_mode=pl.Buffered(1))],
        out_specs=qspec(256),
        out_shape=jax.ShapeDtypeStruct((t, DSA_HEADS * DSA_HEAD_DIM), BF16),
        scratch_shapes=[pltpu.VMEM((s // tk, tq, tk), I32),
                        pltpu.VMEM((32, tq, LANE), I32),
                        pltpu.VMEM((tq, LANE), I32),
                        pltpu.VMEM((tq, LANE), I32),
                        pltpu.VMEM((tq, LANE), I32),
                        pltpu.VMEM((DSA_HEADS, s // tk, th, tk), F32)] + [pltpu.VMEM((DSA_HEADS, th, LANE), F32)] * 2,
        compiler_params=_cparams(("arbitrary", "arbitrary")),
        name="dsa",
    )(iq, iw, ikr, dq, dkk, dvv, tri)


def _gla_kernel(gq_ref, gk_ref, gv_ref, gg_ref, gr_ref, ng_ref, segv_ref, seg64_ref, bmask_ref, o_ref,
                st_scr, kbuf, cbuf, vbuf, *, tg):
    j = pl.program_id(1)
    ch = GLA_CHUNK
    rowc = lax.broadcasted_iota(I32, (tg, 1), 0) % ch

    @pl.when(j == 0)
    def _():
        st_scr[...] = jnp.zeros_like(st_scr)
        kbuf[0:ch, :] = jnp.zeros((ch, LANE), F32)
        cbuf[0:ch, :] = jnp.zeros((ch, LANE), F32)
        vbuf[0:ch, :] = jnp.zeros((ch, 2 * LANE), F32)

    cbuf[ch:ch + tg, :] = gg_ref[...]
    sh = 1
    while sh < ch:
        cur = cbuf[ch:ch + tg, :]
        prev = cbuf[ch - sh:ch - sh + tg, :]
        cbuf[ch:ch + tg, :] = cur + jnp.where(rowc >= sh, prev, 0.0)
        sh *= 2
    cum = cbuf[ch:ch + tg, :]
    q = gq_ref[...]
    k = gk_ref[...]
    v = gv_ref[...]
    kbuf[ch:ch + tg, :] = k
    vbuf[ch:ch + tg, :] = v

    qd = (q * jnp.exp(cum)).astype(BF16)
    bmask = bmask_ref[...]
    inter = []
    for c in range(tg // ch):
        rs = slice(c * ch, (c + 1) * ch)
        st = st_scr[...]
        inter.append(_dot_nt(qd[rs], st.astype(BF16)))
        last = cum[c * ch + ch - 1:c * ch + ch, :]
        kp = k[rs] * jnp.exp(last - cum[rs])
        upd = lax.dot_general(v[rs].astype(BF16), kp.astype(BF16), (((0,), (0,)), ((), ())),
                              preferred_element_type=F32)
        st_scr[...] = jnp.exp(last) * st + upd * bmask
    o = jnp.concatenate(inter, axis=0)

    segv = segv_ref[...]
    cum2 = cum * LOG2E
    cbuf[ch:ch + tg, :] = cum2
    sub = 16
    rows_sub = rowc % sub
    for d in range(sub):
        kd = kbuf[ch - d:ch - d + tg, :]
        cd = cbuf[ch - d:ch - d + tg, :]
        vd = vbuf[ch - d:ch - d + tg, :]
        e = jnp.exp2(jnp.where(rows_sub >= d, cum2 - cd, NEG))
        o = o + _dot((q * kd * e).astype(BF16), segv) * vd

    lane_k3 = lax.broadcasted_iota(I32, (1, 3 * LANE), 1) % LANE // GLA_DK
    lane_v = lax.broadcasted_iota(I32, (1, 2 * LANE), 1) // GLA_DV
    far = []
    for c in range(tg // ch):
        rs = slice(c * ch, (c + 1) * ch)
        qc, kc, cc, sb = q[rs], k[rs], cum2[rs], rowc[rs] // sub
        qh, kh = [], []
        for blk in range(1, ch // sub):
            ref_row = cc[blk * sub - 1:blk * sub, :]
            qh.append(qc * jnp.exp2(jnp.where(sb == blk, cc - ref_row, NEG)))
            kh.append(kc * jnp.exp2(jnp.where(sb < blk, ref_row - cc, NEG)))
        qh = jnp.concatenate(qh, axis=1)
        kh = jnp.concatenate(kh, axis=1).astype(BF16)
        q_heads = jnp.concatenate([jnp.where(lane_k3 == h, qh, 0.0) for h in range(GLA_HEADS)], axis=0)
        att = _dot_nt(q_heads.astype(BF16), kh)
        ov = _dot(att.astype(BF16), v[rs].astype(BF16))
        oc = jnp.where(lane_v == 0, ov[0:ch], 0.0)
        for h in range(1, GLA_HEADS):
            oc = oc + jnp.where(lane_v == h, ov[h * ch:(h + 1) * ch], 0.0)
        far.append(oc)
    o = o + jnp.concatenate(far, axis=0)

    sq = o * o
    hi = sq.astype(BF16)
    lo = (sq - hi.astype(F32)).astype(BF16)
    var = _dot(hi, seg64_ref[...]) + _dot(lo, seg64_ref[...])
    o_ref[...] = (gr_ref[...] * (o * lax.rsqrt(var + EPS) * ng_ref[...])).astype(BF16)


def _gla(gq, gk, gv, gg, gr, norm_g, nb, tg):
    t = gq.shape[0]
    nj = t // nb // tg
    ng = jnp.tile(norm_g.reshape(1, GLA_DV), (1, GLA_HEADS)).astype(F32)
    hk = np.arange(GLA_HEADS * GLA_DK) // GLA_DK
    hv = np.arange(GLA_HEADS * GLA_DV) // GLA_DV
    same = (hk[:, None] == hv[None, :])
    segv = jnp.asarray(same, BF16)
    bmask = jnp.asarray(same.T, F32)
    seg64 = jnp.asarray((hv[:, None] == hv[None, :]) / GLA_DV, BF16)

    def tok(w):
        return pl.BlockSpec((tg, w), lambda b, j: (b * nj + j, 0))

    def full(a):
        return pl.BlockSpec(a.shape, lambda b, j: (0,) * a.ndim, pipeline_mode=pl.Buffered(1))

    return pl.pallas_call(
        functools.partial(_gla_kernel, tg=tg),
        grid=(nb, nj),
        in_specs=[tok(LANE), tok(LANE), tok(256), tok(LANE), tok(256), full(ng), full(segv), full(seg64), full(bmask)],
        out_specs=tok(256),
        out_shape=jax.ShapeDtypeStruct((t, GLA_HEADS * GLA_DV), BF16),
        scratch_shapes=[pltpu.VMEM((2 * LANE, LANE), F32),
                        pltpu.VMEM((GLA_CHUNK + tg, LANE), F32),
                        pltpu.VMEM((GLA_CHUNK + tg, LANE), F32),
                        pltpu.VMEM((GLA_CHUNK + tg, 2 * LANE), F32)],
        compiler_params=_cparams(("arbitrary", "arbitrary")),
        name="gla",
    )(gq, gk, gv, gg, gr, ng, segv, seg64, bmask)


def _merge_kernel(x_ref, g_ref, wg_ref, o0_ref, o1_ref, o2_ref, o3_ref, wb_ref, wo_ref, out_ref):
    x = x_ref[...]
    h = _rms(x, g_ref[...]).astype(BF16)
    merged = None
    for n, o_ref in enumerate((o0_ref, o1_ref, o2_ref, o3_ref)):
        gate = _dot(h, wg_ref[:, n * D_MODEL:(n + 1) * D_MODEL])
        gate = 1.0 / (1.0 + jnp.exp(-gate))
        term = gate * _dot(o_ref[...], wb_ref[n])
        merged = term if merged is None else merged + term
    out_ref[...] = x + _dot(merged.astype(BF16), wo_ref[...])


def _merge(x2, g, wg, branches, wb, wo, tm):
    t, d = x2.shape

    def tok(w):
        return pl.BlockSpec((tm, w), lambda i: (i, 0))

    def full(a):
        return pl.BlockSpec(a.shape, lambda i: (0,) * a.ndim, pipeline_mode=pl.Buffered(1))

    return pl.pallas_call(
        _merge_kernel,
        grid=(t // tm,),
        in_specs=[tok(d), full(g), full(wg)] + [tok(BRANCH_WIDTH)] * N_BRANCH + [full(wb), full(wo)],
        out_specs=tok(d),
        out_shape=jax.ShapeDtypeStruct((t, d), F32),
        compiler_params=_cparams(("arbitrary",)),
        name="merge",
    )(x2, g, wg, *branches, wb, wo)


def _ffn_kernel(x_ref, g_ref, wgu_ref, wd_ref, fg_ref, out_ref, *, fc, final):
    x = x_ref[...]
    h = _rms(x, g_ref[...]).astype(BF16)
    acc = x
    for c in range(D_FF // fc):
        gate = _dot(h, wgu_ref[0, :, c * fc:(c + 1) * fc])
        up = _dot(h, wgu_ref[1, :, c * fc:(c + 1) * fc])
        act = (gate / (1.0 + jnp.exp(-gate)) * up).astype(BF16)
        acc = acc + _dot(act, wd_ref[c * fc:(c + 1) * fc, :])
    if final:
        acc = _rms(acc, fg_ref[...])
    out_ref[...] = acc


def _ffn(x2, g, wgu, wd, fg, tm, final):
    t, d = x2.shape

    def tok(w):
        return pl.BlockSpec((tm, w), lambda i: (i, 0))

    def full(a):
        return pl.BlockSpec(a.shape, lambda i: (0,) * a.ndim, pipeline_mode=pl.Buffered(1))

    return pl.pallas_call(
        functools.partial(_ffn_kernel, fc=256, final=final),
        grid=(t // tm,),
        in_specs=[tok(d), full(g), full(wgu), full(wd), full(fg)],
        out_specs=tok(d),
        out_shape=jax.ShapeDtypeStruct((t, d), F32),
        compiler_params=_cparams(("arbitrary",)),
        name="ffn",
    )(x2, g, wgu, wd, fg)


def _tile(n, pref):
    while n % pref:
        pref //= 2
    return pref


def kernel(x, positions, attn_norm_g, w_in, mla_q_norm_g, mla_w_q_up, mla_kv_norm_g, mla_w_kv_up, conv_w, gla_w_gate_up, gla_b_gate, gla_norm_g, w_branch, w_out, ffn_norm_g, w_ffn_gate, w_ffn_up, w_ffn_down, final_norm_g):
    b, s, d = x.shape
    depth = w_in.shape[0]
    t = b * s
    assert d == D_MODEL and s % GLA_CHUNK == 0
    tm_prep = _tile(s, 512)
    tm_wide = _tile(t, 512)
    tq_mla = _tile(s, 256)
    tk_att = _tile(s, 512)
    tq_dsa = _tile(s, 512)
    th_dsa = _tile(tq_dsa, 256)
    tg = _tile(s, 256)

    x2 = x.reshape(t, d)
    ctab, stab = _rope_tables(positions, _tile(t, 512))
    row = lambda a: a.reshape(1, -1).astype(F32)

    w_small = _pack_small_weight(w_in)
    wq, wqs, wk, wv = _pack_mla_weights(mla_w_q_up, mla_w_kv_up)
    wgate = jnp.concatenate([gla_w_gate_up, jnp.zeros((depth, LANE - GLA_GATE_RANK, GLA_HEADS * GLA_DK), F32)],
                            axis=1).astype(BF16)
    w_gates = w_in[:, :, O_GATE:].astype(BF16)
    w_branch_b, w_out_b = w_branch.astype(BF16), w_out.astype(BF16)
    wgu = jnp.stack([w_ffn_gate, w_ffn_up], axis=1).astype(BF16)
    w_down_b = w_ffn_down.astype(BF16)

    for l in range(depth):
        (mq, mk, mv, o_conv, dq, dkk, dvv, iq, ikr, iw, gq, gk, gv, gg, gr) = _prep(
            x2, b, row(attn_norm_g[l]), w_small[l], row(mla_q_norm_g[l]), wq[l], wqs[l], row(mla_kv_norm_g[l]),
            wk[l], wv[l], ctab, stab, conv_w[l].astype(F32), wgate[l], row(gla_b_gate[l]), tm_prep)
        o_mla = _mla(mq, mk, mv, b, tq_mla, tk_att)
        o_dsa = _dsa(iq, iw, ikr, dq, dkk, dvv, b, tq_dsa, th_dsa, tk_att)
        o_gla = _gla(gq, gk, gv, gg, gr, gla_norm_g[l], b, tg)
        x2 = _merge(x2, row(attn_norm_g[l]), w_gates[l], (o_mla, o_conv, o_dsa, o_gla), w_branch_b[l], w_out_b[l],
                    tm_wide)
        x2 = _ffn(x2, row(ffn_norm_g[l]), wgu[l], w_down_b[l], row(final_norm_g), tm_wide, final=(l == depth - 1))
    return x2.reshape(b, s, d)
```

```python
import functools

import jax
import jax.numpy as jnp
import numpy as np
from jax import lax
from jax.experimental import pallas as pl
from jax.experimental.pallas import tpu as pltpu

F32 = jnp.float32
BF16 = jnp.bfloat16
I32 = jnp.int32
I16 = jnp.int16

D_MODEL = 1024
MLA_HEADS = 4
MLA_Q_RANK = 256
MLA_KV_RANK = 128
MLA_NOPE = 64
MLA_ROPE = 32
MLA_V = 64
ROPE_THETA = 10000.0
CONV_WIDTH = 256
CONV_K = 3
DSA_HEADS = 4
DSA_HEAD_DIM = 64
IDX_HEADS = 8
IDX_DIM = 32
DSA_TOPK_MAX = 256
GLA_HEADS = 4
GLA_DK = 32
GLA_DV = 64
GLA_GATE_RANK = 16
GLA_TAU = 16.0
GLA_CHUNK = 64
N_BRANCH = 4
BRANCH_WIDTH = 256
D_FF = ((8 * D_MODEL + 3 * 256 - 1) // (3 * 256)) * 256
EPS = 1e-6
NEG = -1e30
LOG2E = 1.4426950408889634

LANE = 128
HEAD_PAD = 128

IN_SIZES = (
    MLA_Q_RANK, MLA_KV_RANK, MLA_ROPE,
    CONV_WIDTH, CONV_WIDTH, CONV_WIDTH,
    DSA_HEADS * DSA_HEAD_DIM, DSA_HEAD_DIM, DSA_HEAD_DIM,
    IDX_HEADS * IDX_DIM, IDX_DIM, IDX_HEADS,
    GLA_HEADS * GLA_DK, GLA_HEADS * GLA_DK, GLA_HEADS * GLA_DV,
    GLA_GATE_RANK, GLA_HEADS * GLA_DV,
    N_BRANCH * D_MODEL,
)
_OFF = np.concatenate([[0], np.cumsum(IN_SIZES)]).astype(int)
(O_CQ, O_CKV, O_KR, O_CB, O_CC, O_CX, O_DQ, O_DK, O_DV, O_IQ, O_IK, O_IW,
 O_GQ, O_GK, O_GV, O_GLR, O_GR, O_GATE, O_END) = [int(v) for v in _OFF]

P_CQ, P_CKV, P_KRP, P_KRS = 0, 256, 384, 512
P_CB, P_CC, P_CX = 640, 896, 1152
P_DQ, P_DKK, P_DVV = 1408, 1664, 1792
P_IQ, P_IKR, P_IW = 1920, 2176, 2304
P_GQ, P_GK, P_GV, P_GLR, P_GR = 2432, 2560, 2688, 2944, 3072
P_TOTAL = 3328

VMEM_LIMIT = 56 * 1024 * 1024


def _cparams(sem):
    return pltpu.CompilerParams(dimension_semantics=sem, vmem_limit_bytes=VMEM_LIMIT)


def _rms(x, g):
    return x * lax.rsqrt(jnp.mean(x * x, axis=-1, keepdims=True) + EPS) * g


def _dot(a, b):
    return jnp.dot(a, b, preferred_element_type=F32)


def _dot_nt(a, b):
    return lax.dot_general(a, b, (((1,), (1,)), ((), ())), preferred_element_type=F32)


def _rope_kernel(pos_ref, pat_ref, c_ref, s_ref):
    pos = pos_ref[...].astype(F32)
    ang = pos * pat_ref[0:1, :]
    c_ref[...] = pat_ref[1:2, :] + pat_ref[2:3, :] * jnp.cos(ang)
    s_ref[...] = pat_ref[3:4, :] * jnp.sin(ang)


def _rope_tables(positions, tm):
    t = positions.size
    half = MLA_ROPE // 2
    inv_freq = ROPE_THETA ** (-jnp.arange(half, dtype=F32) / half)
    z64, z32 = jnp.zeros((MLA_NOPE,), F32), jnp.zeros((HEAD_PAD - MLA_NOPE - MLA_ROPE,), F32)
    o16 = jnp.ones((half,), F32)
    pat = jnp.stack([
        jnp.concatenate([z64, inv_freq, inv_freq, z32]),
        jnp.concatenate([z64 + 1.0, 0 * o16, 0 * o16, z32]),
        jnp.concatenate([z64, o16, o16, z32]),
        jnp.concatenate([z64, -o16, o16, z32]),
    ])
    pat = jnp.concatenate([pat, jnp.zeros((4, HEAD_PAD), F32)], axis=0)
    return pl.pallas_call(
        _rope_kernel,
        grid=(t // tm,),
        in_specs=[pl.BlockSpec((tm, 1), lambda i: (i, 0)), pl.BlockSpec((8, HEAD_PAD), lambda i: (0, 0))],
        out_specs=[pl.BlockSpec((tm, HEAD_PAD), lambda i: (i, 0))] * 2,
        out_shape=[jax.ShapeDtypeStruct((t, HEAD_PAD), F32)] * 2,
        compiler_params=_cparams(("arbitrary",)),
        name="rope_tables",
    )(positions.reshape(t, 1), pat)


def _prep_kernel(x_ref, g_ref, w_ref, qg_ref, wq_ref, wqs_ref, kvg_ref, wk_ref, wv_ref, c_ref, s_ref,
                 convw_ref, wgate_ref, bgate_ref,
                 mq_ref, mk_ref, mv_ref, oconv_ref, dq_ref, dkk_ref, dvv_ref, iq_ref, ikr_ref, iw_ref,
                 gq_ref, gk_ref, gv_ref, gg_ref, gr_ref, u_scr, *, tm):
    j = pl.program_id(1)
    h = _rms(x_ref[...], g_ref[...]).astype(BF16)
    p = _dot(h, w_ref[...])

    c1, s1 = c_ref[...], s_ref[...]
    c4 = jnp.concatenate([c1] * MLA_HEADS, axis=1)
    s4 = jnp.concatenate([s1] * MLA_HEADS, axis=1)

    qn = _rms(p[:, P_CQ:P_CQ + MLA_Q_RANK], qg_ref[...]).astype(BF16)
    q = _dot(qn, wq_ref[...]) * c4 + _dot(qn, wqs_ref[...]) * s4
    mq_ref[...] = (q * ((MLA_NOPE + MLA_ROPE) ** -0.5 * LOG2E)).astype(BF16)
    kvn = _rms(p[:, P_CKV:P_CKV + MLA_KV_RANK], kvg_ref[...]).astype(BF16)
    kr = p[:, P_KRP:P_KRP + HEAD_PAD] * c1 + p[:, P_KRS:P_KRS + HEAD_PAD] * s1
    mk_ref[...] = (_dot(kvn, wk_ref[...]) + jnp.concatenate([kr] * MLA_HEADS, axis=1)).astype(BF16)
    vv = _dot(kvn, wv_ref[...])
    ones = jnp.ones((tm, LANE), F32)
    mv_ref[...] = jnp.concatenate([vv[:, 0:LANE], ones, vv[:, LANE:2 * LANE], ones], axis=1).astype(BF16)

    u = p[:, P_CC:P_CC + CONV_WIDTH] * p[:, P_CX:P_CX + CONV_WIDTH]

    @pl.when(j == 0)
    def _():
        u_scr[0:8, :] = jnp.zeros((8, CONV_WIDTH), F32)

    u_scr[8:8 + tm, :] = u
    cw = convw_ref[...]
    y = cw[0:1, :] * u_scr[6:6 + tm, :] + cw[1:2, :] * u_scr[7:7 + tm, :] + cw[2:3, :] * u
    oconv_ref[...] = (p[:, P_CB:P_CB + CONV_WIDTH] * y).astype(BF16)
    u_scr[0:8, :] = u_scr[tm:tm + 8, :]

    dq_ref[...] = (p[:, P_DQ:P_DQ + 256] * (DSA_HEAD_DIM ** -0.5 * LOG2E)).astype(BF16)
    dkk_ref[...] = p[:, P_DKK:P_DKK + LANE].astype(BF16)
    hi_lanes = lax.broadcasted_iota(I32, (tm, LANE), 1) >= DSA_HEAD_DIM
    dvv_ref[...] = jnp.where(hi_lanes, 1.0, p[:, P_DVV:P_DVV + LANE]).astype(BF16)
    iq_ref[...] = p[:, P_IQ:P_IQ + 256].astype(BF16)
    ikr_ref[...] = p[:, P_IKR:P_IKR + LANE].astype(BF16)
    iw_ref[...] = p[:, P_IW:P_IW + LANE] * ((IDX_DIM ** -0.5) * (IDX_HEADS ** -0.5))

    gq_ref[...] = p[:, P_GQ:P_GQ + LANE] * (GLA_DK ** -0.5)
    gk_ref[...] = p[:, P_GK:P_GK + LANE]
    gv_ref[...] = p[:, P_GV:P_GV + 256]
    gate = _dot(p[:, P_GLR:P_GLR + LANE].astype(BF16), wgate_ref[...]) + bgate_ref[...]
    log_sig = jnp.minimum(gate, 0.0) - jnp.log1p(jnp.exp(-jnp.abs(gate)))
    gg_ref[...] = log_sig * (1.0 / GLA_TAU)
    gr = p[:, P_GR:P_GR + 256]
    gr_ref[...] = gr / (1.0 + jnp.exp(-gr))


def _pack_small_weight(w_in):
    def cols(a, b):
        return w_in[..., a:b]

    def zeros(n):
        return jnp.zeros(w_in.shape[:-1] + (n,), w_in.dtype)

    half = MLA_ROPE // 2
    kr1, kr2 = cols(O_KR, O_KR + half), cols(O_KR + half, O_KR + MLA_ROPE)
    pad_tail = HEAD_PAD - MLA_NOPE - MLA_ROPE
    pieces = [
        cols(O_CQ, O_CKV), cols(O_CKV, O_KR),
        zeros(MLA_NOPE), kr1, kr2, zeros(pad_tail),
        zeros(MLA_NOPE), kr2, kr1, zeros(pad_tail),
        cols(O_CB, O_CC), cols(O_CC, O_CX), cols(O_CX, O_DQ),
        cols(O_DQ, O_DK), cols(O_DK, O_DV), cols(O_DK, O_DV), cols(O_DV, O_IQ), zeros(DSA_HEAD_DIM),
        cols(O_IQ, O_IK), cols(O_IK, O_IW), cols(O_IK, O_IW), cols(O_IK, O_IW), cols(O_IK, O_IW),
        cols(O_IW, O_GQ), zeros(LANE - IDX_HEADS),
        cols(O_GQ, O_GK), cols(O_GK, O_GV), cols(O_GV, O_GLR),
        cols(O_GLR, O_GR), zeros(LANE - GLA_GATE_RANK),
        cols(O_GR, O_GATE),
    ]
    out = jnp.concatenate(pieces, axis=-1)
    assert out.shape[-1] == P_TOTAL, out.shape
    return out.astype(BF16)


def _pack_mla_weights(w_q_up, w_kv_up):
    half = MLA_ROPE // 2
    lq = w_q_up.shape[:-1]
    lk = w_kv_up.shape[:-1]
    wq = w_q_up.reshape(lq + (MLA_HEADS, MLA_NOPE + MLA_ROPE))
    nope, x1, x2 = wq[..., :MLA_NOPE], wq[..., MLA_NOPE:MLA_NOPE + half], wq[..., MLA_NOPE + half:]
    zq = jnp.zeros(lq + (MLA_HEADS, HEAD_PAD - MLA_NOPE - MLA_ROPE), w_q_up.dtype)
    wq_pad = jnp.concatenate([nope, x1, x2, zq], axis=-1).reshape(lq + (MLA_HEADS * HEAD_PAD,))
    wq_swap = jnp.concatenate([jnp.zeros_like(nope), x2, x1, zq], axis=-1).reshape(lq + (MLA_HEADS * HEAD_PAD,))
    wkv = w_kv_up.reshape(lk + (MLA_HEADS, MLA_NOPE + MLA_V))
    zk = jnp.zeros(lk + (MLA_HEADS, HEAD_PAD - MLA_NOPE), w_kv_up.dtype)
    wk_pad = jnp.concatenate([wkv[..., :MLA_NOPE], zk], axis=-1).reshape(lk + (MLA_HEADS * HEAD_PAD,))
    wv = wkv[..., MLA_NOPE:].reshape(lk + (MLA_HEADS * MLA_V,))
    return wq_pad.astype(BF16), wq_swap.astype(BF16), wk_pad.astype(BF16), wv.astype(BF16)


def _prep(x2, nb, g, w_small, qg, wq, wqs, kvg, wk, wv, ctab, stab, conv_w, wgate, bgate, tm):
    t, d = x2.shape
    nj = t // nb // tm

    def tok(width):
        return pl.BlockSpec((tm, width), lambda b, j: (b * nj + j, 0))

    def full(a):
        return pl.BlockSpec(a.shape, lambda b, j: (0,) * a.ndim, pipeline_mode=pl.Buffered(1))

    outs = [(512, BF16), (512, BF16), (512, BF16), (256, BF16), (256, BF16), (LANE, BF16), (LANE, BF16),
            (256, BF16), (LANE, BF16), (LANE, F32), (LANE, F32), (LANE, F32), (256, F32), (LANE, F32), (256, F32)]
    return pl.pallas_call(
        functools.partial(_prep_kernel, tm=tm),
        grid=(nb, nj),
        in_specs=[tok(d), full(g), full(w_small), full(qg), full(wq), full(wqs), full(kvg), full(wk), full(wv),
                  tok(HEAD_PAD), tok(HEAD_PAD), full(conv_w), full(wgate), full(bgate)],
        out_specs=[tok(w) for w, _ in outs],
        out_shape=[jax.ShapeDtypeStruct((t, w), dt) for w, dt in outs],
        scratch_shapes=[pltpu.VMEM((tm + 8, CONV_WIDTH), F32)],
        compiler_params=_cparams(("arbitrary", "arbitrary")),
        name="prep",
    )(x2, g, w_small, qg, wq, wqs, kvg, wk, wv, ctab, stab, conv_w, wgate, bgate)


def _for_chunks(n, body):
    def group(i, carry):
        for u in range(4):
            body(4 * i + u, carry)
        return carry

    lax.fori_loop(0, n // 4, group, 0)
    rem = n % 4

    @pl.when(rem >= 2)
    def _():
        body(n - rem, 0)
        body(n - rem + 1, 0)

    @pl.when(rem % 2 == 1)
    def _():
        body(n - 1, 0)


def _fold_lanes(x, op):
    out = x[:, 0:LANE]
    for g in range(1, x.shape[1] // LANE):
        out = op(out, x[:, g * LANE:(g + 1) * LANE])
    return out


def _mla_kernel(q_ref, k_ref, v_ref, o_ref, s_scr, m_scr, acc_scr, *, tq, tk):
    i = pl.program_id(1)
    n_chunks = ((i + 1) * tq + tk - 1) // tk
    row = i * tq + lax.broadcasted_iota(I32, (tq, tk), 0)
    col = lax.broadcasted_iota(I32, (tq, tk), 1)
    lane = lax.broadcasted_iota(I32, (tq, LANE), 1)
    heads = range(MLA_HEADS)
    hs = [slice(h * HEAD_PAD, (h + 1) * HEAD_PAD) for h in heads]
    vs = [slice((h // 2) * 2 * LANE, (h // 2 + 1) * 2 * LANE) for h in heads]

    m_scr[...] = jnp.full(m_scr.shape, NEG, F32)
    acc_scr[...] = jnp.zeros(acc_scr.shape, F32)

    def phase1(c, carry):
        off = pl.multiple_of(c * tk, tk)
        causal = c * tk + col <= row
        for h in heads:
            s = jnp.where(causal, _dot_nt(q_ref[:, hs[h]], k_ref[pl.ds(off, tk), hs[h]]), NEG)
            s_scr[h, c] = s
            m_scr[h] = jnp.maximum(m_scr[h], _fold_lanes(s, jnp.maximum))
        return carry

    _for_chunks(n_chunks, phase1)
    for h in heads:
        m_scr[h] = jnp.broadcast_to(jnp.max(m_scr[h], axis=1, keepdims=True), (tq, LANE))

    def phase2(c, carry):
        off = pl.multiple_of(c * tk, tk)
        for h in heads:
            p = jnp.exp2(s_scr[h, c] - jnp.concatenate([m_scr[h]] * (tk // LANE), axis=1))
            acc_scr[h] = acc_scr[h] + _dot(p.astype(BF16), v_ref[pl.ds(off, tk), vs[h]])
        return carry

    _for_chunks(n_chunks, phase2)
    res = [acc_scr[h, :, 0:LANE] / acc_scr[h, :, LANE:2 * LANE] for h in heads]
    for pair in range(MLA_HEADS // 2):
        o_ref[:, pair * LANE:(pair + 1) * LANE] = jnp.where(lane < MLA_V, res[2 * pair], res[2 * pair + 1]).astype(BF16)


def _mla(mq, mk, mv, nb, tq, tk):
    t = mq.shape[0]
    s = t // nb
    nq = s // tq
    return pl.pallas_call(
        functools.partial(_mla_kernel, tq=tq, tk=tk),
        grid=(nb, nq),
        in_specs=[pl.BlockSpec((tq, MLA_HEADS * HEAD_PAD), lambda b, i: (b * nq + i, 0)),
                  pl.BlockSpec((s, MLA_HEADS * HEAD_PAD), lambda b, i: (b, 0)),
                  pl.BlockSpec((s, MLA_HEADS * HEAD_PAD), lambda b, i: (b, 0))],
        out_specs=pl.BlockSpec((tq, MLA_HEADS * MLA_V), lambda b, i: (b * nq + i, 0)),
        out_shape=jax.ShapeDtypeStruct((t, MLA_HEADS * MLA_V), BF16),
        scratch_shapes=[pltpu.VMEM((MLA_HEADS, s // tk, tq, tk), F32),
                        pltpu.VMEM((MLA_HEADS, tq, LANE), F32),
                        pltpu.VMEM((MLA_HEADS, tq, 2 * LANE), F32)],
        compiler_params=_cparams(("arbitrary", "arbitrary")),
        name="mla_attn",
    )(mq, mk, mv)


def _dsa_kernel(iq_ref, iw_ref, ikr_ref, dq_ref, dkk_ref, dvv_ref, tri_ref, o_ref,
                key_scr, plane_scr, eq_scr, gt_scr, thr_scr, s_scr, m_scr, acc_scr,
                *, tq, th, tk, topk):
    i = pl.program_id(1)
    q_end = (i + 1) * tq
    n_chunks = (q_end + tk - 1) // tk
    lane_q = lax.broadcasted_iota(I32, (th, LANE), 1)
    lane_k = lax.broadcasted_iota(I32, (tq, tk), 1)
    int_min = jnp.int32(-2 ** 31)
    rb = 64
    lane_rb = lax.broadcasted_iota(I32, (rb, tk), 1)
    row_rb = lax.broadcasted_iota(I32, (rb, 1), 0)

    for r0 in range(0, tq, th):
        iq = iq_ref[r0:r0 + th, :].astype(F32)
        iw = iw_ref[r0:r0 + th, :]
        q_heads = []
        w_heads = []
        for h in range(IDX_HEADS):
            grp = h // 4
            sub = iq[:, grp * LANE:(grp + 1) * LANE]
            lo = (h % 4) * IDX_DIM
            in_head = (lane_q >= lo) & (lane_q < lo + IDX_DIM)
            q_heads.append(jnp.where(in_head, sub, 0.0).astype(BF16))
            w_heads.append(iw[:, h:h + 1])
        q_stacks = [jnp.concatenate([qh[r:r + rb] for qh in q_heads], axis=0) for r in range(0, th, rb)]

        def score_chunk(c, _, r0=r0, q_stacks=q_stacks, w_heads=w_heads):
            off = pl.multiple_of(c * tk, tk)
            kc = ikr_ref[pl.ds(off, tk), :]
            for bi, r in enumerate(range(0, th, rb)):
                z = _dot_nt(q_stacks[bi], kc)
                acc = w_heads[0][r:r + rb] * jnp.maximum(z[0:rb], 0.0)
                for h in range(1, IDX_HEADS):
                    acc = acc + w_heads[h][r:r + rb] * jnp.maximum(z[h * rb:(h + 1) * rb], 0.0)
                bits = pltpu.bitcast(acc, I32)
                key = jnp.where(bits < 0, bits ^ jnp.int32(0x7FFFFFFF), bits)
                key = jnp.where(c * tk + lane_rb <= i * tq + r0 + r + row_rb, key, int_min)
                key_scr[c, r0 + r:r0 + r + rb, :] = key
            return 0

        _for_chunks(n_chunks, score_chunk)

    kf = jnp.float32(topk)
    n_groups = key_scr.shape[0] * (tk // LANE)
    cb = 128

    def row_total(cnt):
        return _dot(cnt.astype(F32).astype(BF16), jnp.ones((LANE, LANE), BF16))

    def select():
        def fill(c, _):
            key_scr[c] = jnp.full((tq, tk), int_min, I32)
            return 0
        lax.fori_loop(n_chunks, key_scr.shape[0], fill, 0)

        stage_mask = {16: 0x0000FFFF, 8: 0x00FF00FF, 4: 0x0F0F0F0F, 2: 0x33333333, 1: 0x55555555}

        def build_planes(blk, _, nw):
            r = pl.multiple_of(blk * 8, 8)
            w = []
            for g in range(nw):
                if g < n_groups:
                    gc, gl = divmod(g, tk // LANE)
                    w.append(key_scr[gc, pl.ds(r, 8), gl * LANE:(gl + 1) * LANE] ^ int_min)
                else:
                    w.append(jnp.zeros((8, LANE), I32))
            j = nw // 2
            while j:
                k = 0
                while k < nw:
                    t = (w[k] ^ lax.shift_right_logical(w[k + j], jnp.int32(j))) & jnp.int32(stage_mask[j])
                    w[k] = w[k] ^ t
                    w[k + j] = w[k + j] ^ lax.shift_left(t, jnp.int32(j))
                    k = (k + j + 1) & ~j
                j >>= 1
            if nw == 16:
                w = [lax.shift_right_logical(u, jnp.int32(16)) for u in w] + [u & jnp.int32(0xFFFF) for u in w]
            for b in range(32):
                plane_scr[b, pl.ds(r, 8), :] = w[b]
            return 0

        few_groups = n_chunks * (tk // LANE) <= 16

        @pl.when(few_groups)
        def _():
            lax.fori_loop(0, tq // 8, functools.partial(build_planes, nw=16), 0)

        if n_groups > 16:
            @pl.when(jnp.logical_not(few_groups))
            def _():
                lax.fori_loop(0, tq // 8, functools.partial(build_planes, nw=32), 0)

        eq_scr[...] = jnp.full((tq, LANE), -1, I32)
        gt_scr[...] = jnp.zeros((tq, LANE), I32)
        thr_scr[...] = jnp.zeros((tq, LANE), I32)

        def bit_pass(it, _):
            bit = lax.shift_left(jnp.int32(1), 31 - it)
            for r in range(0, tq, cb):
                rows = slice(r, r + cb)
                eq, gtm = eq_scr[rows, :], gt_scr[rows, :]
                t = eq & plane_scr[it, rows, :]
                u = gtm | t
                take = jnp.sum(lax.population_count(u).astype(F32), axis=1, keepdims=True) >= kf
                eq_scr[rows, :] = jnp.where(take, t, eq ^ t)
                gt_scr[rows, :] = jnp.where(take, gtm, u)
                thr_scr[rows, :] = jnp.where(take, thr_scr[rows, :] | bit, thr_scr[rows, :])
            return 0
        lax.fori_loop(0, 32, bit_pass, 0)

        thr_u = thr_scr[:, 0:1]
        thr = jnp.maximum(thr_u ^ int_min, int_min + 1)
        c_gt = row_total(lax.population_count(gt_scr[...]))[:, 0:1]
        c_ge = c_gt + row_total(lax.population_count(eq_scr[...]))[:, 0:1]
        excess = (c_ge > kf) & (thr_u != 0)

        def tie_break():
            need = kf - c_gt

            def demote(c, seen):
                kk = key_scr[c]
                tie = kk == thr
                tie_b = jnp.where(tie, 1.0, 0.0).astype(BF16)
                rank = seen + jnp.concatenate([_dot(tie_b[r:r + th], tri_ref[...]) for r in range(0, tq, th)], axis=0)
                key_scr[c] = jnp.where(tie & (rank > need), thr - 1, kk)
                return rank[:, tk - 1:tk]
            lax.fori_loop(0, n_chunks, demote, jnp.zeros((tq, 1), F32))

        any_excess = jnp.max(jnp.where(excess, 1.0, 0.0)) > 0.0
        lax.cond(any_excess, tie_break, lambda: None)
        return thr

    thr = lax.cond(q_end > topk, select, lambda: jnp.full((tq, 1), int_min + 1, I32))

    heads = range(DSA_HEADS)
    for r0 in range(0, tq, th):
        rows = slice(r0, r0 + th)
        thr_h = thr[rows]
        dq = dq_ref[rows, :].astype(F32)
        qm = []
        for h in heads:
            sub = dq[:, (h // 2) * LANE:(h // 2 + 1) * LANE]
            lo = (h % 2) * DSA_HEAD_DIM
            in_head = (lane_q >= lo) & (lane_q < lo + DSA_HEAD_DIM)
            qm.append(jnp.where(in_head, sub, 0.0).astype(BF16))
        qm_stack = jnp.concatenate(qm, axis=0)

        m_scr[...] = jnp.full(m_scr.shape, NEG, F32)
        acc_scr[...] = jnp.zeros(acc_scr.shape, F32)

        def phase1(c, carry, rows=rows, thr_h=thr_h, qm_stack=qm_stack):
            off = pl.multiple_of(c * tk, tk)
            sel = key_scr[c, rows, :] >= thr_h
            s = _dot_nt(qm_stack, dkk_ref[pl.ds(off, tk), :])
            for h in heads:
                sm = jnp.where(sel, s[h * th:(h + 1) * th], NEG)
                s_scr[h, c] = sm
                m_scr[h] = jnp.maximum(m_scr[h], _fold_lanes(sm, jnp.maximum))
            return carry

        _for_chunks(n_chunks, phase1)
        for h in heads:
            m_scr[h] = jnp.broadcast_to(jnp.max(m_scr[h], axis=1, keepdims=True), (th, LANE))

        def phase2(c, carry):
            off = pl.multiple_of(c * tk, tk)
            ps = []
            for h in heads:
                p = jnp.exp2(s_scr[h, c] - jnp.concatenate([m_scr[h]] * (tk // LANE), axis=1))
                ps.append(p.astype(BF16))
            pv = _dot(jnp.concatenate(ps, axis=0), dvv_ref[pl.ds(off, tk), :])
            for h in heads:
                acc_scr[h] = acc_scr[h] + pv[h * th:(h + 1) * th]
            return carry

        _for_chunks(n_chunks, phase2)
        outs = [acc_scr[h] / pltpu.roll(acc_scr[h], DSA_HEAD_DIM, 1) for h in heads]
        lo_half = lane_q < DSA_HEAD_DIM
        o_ref[rows, :] = jnp.concatenate(
            [jnp.where(lo_half, outs[0], pltpu.roll(outs[1], DSA_HEAD_DIM, 1)),
             jnp.where(lo_half, outs[2], pltpu.roll(outs[3], DSA_HEAD_DIM, 1))], axis=1).astype(BF16)


def _dsa(iq, iw, ikr, dq, dkk, dvv, nb, tq, th, tk):
    t = iq.shape[0]
    s = t // nb
    nq = s // tq
    topk = min(DSA_TOPK_MAX, s // 4)
    assert s // LANE <= 32, "the bit-sliced selection packs one key per 128-lane group into a 32-bit word"
    tri = jnp.asarray(np.triu(np.ones((tk, tk), np.float32)), BF16)

    def qspec(w):
        return pl.BlockSpec((tq, w), lambda b, i: (b * nq + i, 0))

    def kspec(w):
        return pl.BlockSpec((s, w), lambda b, i: (b, 0))

    return pl.pallas_call(
        functools.partial(_dsa_kernel, tq=tq, th=th, tk=tk, topk=topk),
        grid=(nb, nq),
        in_specs=[qspec(256), qspec(LANE), kspec(LANE), qspec(256), kspec(LANE), kspec(LANE),
                  pl.BlockSpec((tk, tk), lambda b, i: (0, 0), pipeline_mode=pl.Buffered(1))],
        out_specs=qspec(256),
        out_shape=jax.ShapeDtypeStruct((t, DSA_HEADS * DSA_HEAD_DIM), BF16),
        scratch_shapes=[pltpu.VMEM((s // tk, tq, tk), I32),
                        pltpu.VMEM((32, tq, LANE), I32),
                        pltpu.VMEM((tq, LANE), I32),
                        pltpu.VMEM((tq, LANE), I32),
                        pltpu.VMEM((tq, LANE), I32),
                        pltpu.VMEM((DSA_HEADS, s // tk, th, tk), F32)] + [pltpu.VMEM((DSA_HEADS, th, LANE), F32)] * 2,
        compiler_params=_cparams(("arbitrary", "arbitrary")),
        name="dsa",
    )(iq, iw, ikr, dq, dkk, dvv, tri)


def _gla_kernel(gq_ref, gk_ref, gv_ref, gg_ref, gr_ref, ng_ref, segv_ref, seg64_ref, bmask_ref, o_ref,
                st_scr, kbuf, cbuf, vbuf, *, tg):
    j = pl.program_id(1)
    ch = GLA_CHUNK
    rowc = lax.broadcasted_iota(I32, (tg, 1), 0) % ch

    @pl.when(j == 0)
    def _():
        st_scr[...] = jnp.zeros_like(st_scr)
        kbuf[0:ch, :] = jnp.zeros((ch, LANE), F32)
        cbuf[0:ch, :] = jnp.zeros((ch, LANE), F32)
        vbuf[0:ch, :] = jnp.zeros((ch, 2 * LANE), F32)

    cbuf[ch:ch + tg, :] = gg_ref[...]
    sh = 1
    while sh < ch:
        cur = cbuf[ch:ch + tg, :]
        prev = cbuf[ch - sh:ch - sh + tg, :]
        cbuf[ch:ch + tg, :] = cur + jnp.where(rowc >= sh, prev, 0.0)
        sh *= 2
    cum = cbuf[ch:ch + tg, :]
    q = gq_ref[...]
    k = gk_ref[...]
    v = gv_ref[...]
    kbuf[ch:ch + tg, :] = k
    vbuf[ch:ch + tg, :] = v

    qd = (q * jnp.exp(cum)).astype(BF16)
    bmask = bmask_ref[...]
    inter = []
    for c in range(tg // ch):
        rs = slice(c * ch, (c + 1) * ch)
        st = st_scr[...]
        inter.append(_dot_nt(qd[rs], st.astype(BF16)))
        last = cum[c * ch + ch - 1:c * ch + ch, :]
        kp = k[rs] * jnp.exp(last - cum[rs])
        upd = lax.dot_general(v[rs].astype(BF16), kp.astype(BF16), (((0,), (0,)), ((), ())),
                              preferred_element_type=F32)
        st_scr[...] = jnp.exp(last) * st + upd * bmask
    o = jnp.concatenate(inter, axis=0)

    segv = segv_ref[...]
    cum2 = cum * LOG2E
    cbuf[ch:ch + tg, :] = cum2
    sub = 16
    rows_sub = rowc % sub
    for d in range(sub):
        kd = kbuf[ch - d:ch - d + tg, :]
        cd = cbuf[ch - d:ch - d + tg, :]
        vd = vbuf[ch - d:ch - d + tg, :]
        e = jnp.exp2(jnp.where(rows_sub >= d, cum2 - cd, NEG))
        o = o + _dot((q * kd * e).astype(BF16), segv) * vd

    lane_k3 = lax.broadcasted_iota(I32, (1, 3 * LANE), 1) % LANE // GLA_DK
    lane_v = lax.broadcasted_iota(I32, (1, 2 * LANE), 1) // GLA_DV
    far = []
    for c in range(tg // ch):
        rs = slice(c * ch, (c + 1) * ch)
        qc, kc, cc, sb = q[rs], k[rs], cum2[rs], rowc[rs] // sub
        qh, kh = [], []
        for blk in range(1, ch // sub):
            ref_row = cc[blk * sub - 1:blk * sub, :]
            qh.append(qc * jnp.exp2(jnp.where(sb == blk, cc - ref_row, NEG)))
            kh.append(kc * jnp.exp2(jnp.where(sb < blk, ref_row - cc, NEG)))
        qh = jnp.concatenate(qh, axis=1)
        kh = jnp.concatenate(kh, axis=1).astype(BF16)
        q_heads = jnp.concatenate([jnp.where(lane_k3 == h, qh, 0.0) for h in range(GLA_HEADS)], axis=0)
        att = _dot_nt(q_heads.astype(BF16), kh)
        ov = _dot(att.astype(BF16), v[rs].astype(BF16))
        oc = jnp.where(lane_v == 0, ov[0:ch], 0.0)
        for h in range(1, GLA_HEADS):
            oc = oc + jnp.where(lane_v == h, ov[h * ch:(h + 1) * ch], 0.0)
        far.append(oc)
    o = o + jnp.concatenate(far, axis=0)

    sq = o * o
    hi = sq.astype(BF16)
    lo = (sq - hi.astype(F32)).astype(BF16)
    var = _dot(hi, seg64_ref[...]) + _dot(lo, seg64_ref[...])
    o_ref[...] = (gr_ref[...] * (o * lax.rsqrt(var + EPS) * ng_ref[...])).astype(BF16)


def _gla(gq, gk, gv, gg, gr, norm_g, nb, tg):
    t = gq.shape[0]
    nj = t // nb // tg
    ng = jnp.tile(norm_g.reshape(1, GLA_DV), (1, GLA_HEADS)).astype(F32)
    hk = np.arange(GLA_HEADS * GLA_DK) // GLA_DK
    hv = np.arange(GLA_HEADS * GLA_DV) // GLA_DV
    same = (hk[:, None] == hv[None, :])
    segv = jnp.asarray(same, BF16)
    bmask = jnp.asarray(same.T, F32)
    seg64 = jnp.asarray((hv[:, None] == hv[None, :]) / GLA_DV, BF16)

    def tok(w):
        return pl.BlockSpec((tg, w), lambda b, j: (b * nj + j, 0))

    def full(a):
        return pl.BlockSpec(a.shape, lambda b, j: (0,) * a.ndim, pipeline_mode=pl.Buffered(1))

    return pl.pallas_call(
        functools.partial(_gla_kernel, tg=tg),
        grid=(nb, nj),
        in_specs=[tok(LANE), tok(LANE), tok(256), tok(LANE), tok(256), full(ng), full(segv), full(seg64), full(bmask)],
        out_specs=tok(256),
        out_shape=jax.ShapeDtypeStruct((t, GLA_HEADS * GLA_DV), BF16),
        scratch_shapes=[pltpu.VMEM((2 * LANE, LANE), F32),
                        pltpu.VMEM((GLA_CHUNK + tg, LANE), F32),
                        pltpu.VMEM((GLA_CHUNK + tg, LANE), F32),
                        pltpu.VMEM((GLA_CHUNK + tg, 2 * LANE), F32)],
        compiler_params=_cparams(("arbitrary", "arbitrary")),
        name="gla",
    )(gq, gk, gv, gg, gr, ng, segv, seg64, bmask)


def _merge_kernel(x_ref, g_ref, wg_ref, o0_ref, o1_ref, o2_ref, o3_ref, wb_ref, wo_ref, out_ref):
    x = x_ref[...]
    h = _rms(x, g_ref[...]).astype(BF16)
    merged = None
    for n, o_ref in enumerate((o0_ref, o1_ref, o2_ref, o3_ref)):
        gate = _dot(h, wg_ref[:, n * D_MODEL:(n + 1) * D_MODEL])
        gate = 1.0 / (1.0 + jnp.exp(-gate))
        term = gate * _dot(o_ref[...], wb_ref[n])
        merged = term if merged is None else merged + term
    out_ref[...] = x + _dot(merged.astype(BF16), wo_ref[...])


def _merge(x2, g, wg, branches, wb, wo, tm):
    t, d = x2.shape

    def tok(w):
        return pl.BlockSpec((tm, w), lambda i: (i, 0))

    def full(a):
        return pl.BlockSpec(a.shape, lambda i: (0,) * a.ndim, pipeline_mode=pl.Buffered(1))

    return pl.pallas_call(
        _merge_kernel,
        grid=(t // tm,),
        in_specs=[tok(d), full(g), full(wg)] + [tok(BRANCH_WIDTH)] * N_BRANCH + [full(wb), full(wo)],
        out_specs=tok(d),
        out_shape=jax.ShapeDtypeStruct((t, d), F32),
        compiler_params=_cparams(("arbitrary",)),
        name="merge",
    )(x2, g, wg, *branches, wb, wo)


def _ffn_kernel(x_ref, g_ref, wg_ref, wu_ref, wd_ref, fg_ref, out_ref, *, fc, final):
    x = x_ref[...]
    h = _rms(x, g_ref[...]).astype(BF16)
    acc = x
    for c in range(D_FF // fc):
        gate = _dot(h, wg_ref[:, c * fc:(c + 1) * fc])
        up = _dot(h, wu_ref[:, c * fc:(c + 1) * fc])
        act = (gate / (1.0 + jnp.exp(-gate)) * up).astype(BF16)
        acc = acc + _dot(act, wd_ref[c * fc:(c + 1) * fc, :])
    if final:
        acc = _rms(acc, fg_ref[...])
    out_ref[...] = acc


def _ffn(x2, g, wg, wu, wd, fg, tm, final):
    t, d = x2.shape

    def tok(w):
        return pl.BlockSpec((tm, w), lambda i: (i, 0))

    def full(a):
        return pl.BlockSpec(a.shape, lambda i: (0,) * a.ndim, pipeline_mode=pl.Buffered(1))

    return pl.pallas_call(
        functools.partial(_ffn_kernel, fc=256, final=final),
        grid=(t // tm,),
        in_specs=[tok(d), full(g), full(wg), full(wu), full(wd), full(fg)],
        out_specs=tok(d),
        out_shape=jax.ShapeDtypeStruct((t, d), F32),
        compiler_params=_cparams(("arbitrary",)),
        name="ffn",
    )(x2, g, wg, wu, wd, fg)


def _tile(n, pref):
    while n % pref:
        pref //= 2
    return pref


def kernel(x, positions, attn_norm_g, w_in, mla_q_norm_g, mla_w_q_up, mla_kv_norm_g, mla_w_kv_up, conv_w, gla_w_gate_up, gla_b_gate, gla_norm_g, w_branch, w_out, ffn_norm_g, w_ffn_gate, w_ffn_up, w_ffn_down, final_norm_g):
    b, s, d = x.shape
    depth = w_in.shape[0]
    t = b * s
    assert d == D_MODEL and s % GLA_CHUNK == 0
    tm_prep = _tile(s, 512)
    tm_wide = _tile(t, 1024)
    tq_mla = _tile(s, 256)
    tk_att = _tile(s, 512)
    tq_dsa = _tile(s, 512)
    th_dsa = _tile(tq_dsa, 256)
    tg = _tile(s, 512)

    x2 = x.reshape(t, d)
    ctab, stab = _rope_tables(positions, _tile(t, 512))
    row = lambda a: a.reshape(1, -1).astype(F32)

    w_small = _pack_small_weight(w_in)
    wq, wqs, wk, wv = _pack_mla_weights(mla_w_q_up, mla_w_kv_up)
    wgate = jnp.concatenate([gla_w_gate_up, jnp.zeros((depth, LANE - GLA_GATE_RANK, GLA_HEADS * GLA_DK), F32)],
                            axis=1).astype(BF16)
    w_gates = w_in[:, :, O_GATE:].astype(BF16)
    w_branch_b, w_out_b = w_branch.astype(BF16), w_out.astype(BF16)
    w_gate_b, w_up_b = w_ffn_gate.astype(BF16), w_ffn_up.astype(BF16)
    w_down_b = w_ffn_down.astype(BF16)

    for l in range(depth):
        (mq, mk, mv, o_conv, dq, dkk, dvv, iq, ikr, iw, gq, gk, gv, gg, gr) = _prep(
            x2, b, row(attn_norm_g[l]), w_small[l], row(mla_q_norm_g[l]), wq[l], wqs[l], row(mla_kv_norm_g[l]),
            wk[l], wv[l], ctab, stab, conv_w[l].astype(F32), wgate[l], row(gla_b_gate[l]), tm_prep)
        o_mla = _mla(mq, mk, mv, b, tq_mla, tk_att)
        o_dsa = _dsa(iq, iw, ikr, dq, dkk, dvv, b, tq_dsa, th_dsa, tk_att)
        o_gla = _gla(gq, gk, gv, gg, gr, gla_norm_g[l], b, tg)
        x2 = _merge(x2, row(attn_norm_g[l]), w_gates[l], (o_mla, o_conv, o_dsa, o_gla), w_branch_b[l], w_out_b[l],
                    tm_wide)
        x2 = _ffn(x2, row(ffn_norm_g[l]), w_gate_b[l], w_up_b[l], w_down_b[l], row(final_norm_g), tm_wide,
                  final=(l == depth - 1))
    return x2.reshape(b, s, d)
```

```python
import functools

import jax
import jax.numpy as jnp
import numpy as np
from jax import lax
from jax.experimental import pallas as pl
from jax.experimental.pallas import tpu as pltpu

F32 = jnp.float32
BF16 = jnp.bfloat16
I32 = jnp.int32
I16 = jnp.int16

D_MODEL = 1024
MLA_HEADS = 4
MLA_Q_RANK = 256
MLA_KV_RANK = 128
MLA_NOPE = 64
MLA_ROPE = 32
MLA_V = 64
ROPE_THETA = 10000.0
CONV_WIDTH = 256
CONV_K = 3
DSA_HEADS = 4
DSA_HEAD_DIM = 64
IDX_HEADS = 8
IDX_DIM = 32
DSA_TOPK_MAX = 256
GLA_HEADS = 4
GLA_DK = 32
GLA_DV = 64
GLA_GATE_RANK = 16
GLA_TAU = 16.0
GLA_CHUNK = 64
N_BRANCH = 4
BRANCH_WIDTH = 256
D_FF = ((8 * D_MODEL + 3 * 256 - 1) // (3 * 256)) * 256
EPS = 1e-6
NEG = -1e30
LOG2E = 1.4426950408889634

LANE = 128
HEAD_PAD = 128

IN_SIZES = (
    MLA_Q_RANK, MLA_KV_RANK, MLA_ROPE,
    CONV_WIDTH, CONV_WIDTH, CONV_WIDTH,
    DSA_HEADS * DSA_HEAD_DIM, DSA_HEAD_DIM, DSA_HEAD_DIM,
    IDX_HEADS * IDX_DIM, IDX_DIM, IDX_HEADS,
    GLA_HEADS * GLA_DK, GLA_HEADS * GLA_DK, GLA_HEADS * GLA_DV,
    GLA_GATE_RANK, GLA_HEADS * GLA_DV,
    N_BRANCH * D_MODEL,
)
_OFF = np.concatenate([[0], np.cumsum(IN_SIZES)]).astype(int)
(O_CQ, O_CKV, O_KR, O_CB, O_CC, O_CX, O_DQ, O_DK, O_DV, O_IQ, O_IK, O_IW,
 O_GQ, O_GK, O_GV, O_GLR, O_GR, O_GATE, O_END) = [int(v) for v in _OFF]

P_CQ, P_CKV, P_KRP, P_KRS = 0, 256, 384, 512
P_CB, P_CC, P_CX = 640, 896, 1152
P_DQ, P_DKK, P_DVV = 1408, 1664, 1792
P_IQ, P_IKR, P_IW = 1920, 2176, 2304
P_GQ, P_GK, P_GV, P_GLR, P_GR = 2432, 2560, 2688, 2944, 3072
P_TOTAL = 3328

VMEM_LIMIT = 56 * 1024 * 1024


def _cparams(sem):
    return pltpu.CompilerParams(dimension_semantics=sem, vmem_limit_bytes=VMEM_LIMIT)


def _rms(x, g):
    return x * lax.rsqrt(jnp.mean(x * x, axis=-1, keepdims=True) + EPS) * g


def _dot(a, b):
    return jnp.dot(a, b, preferred_element_type=F32)


def _dot_nt(a, b):
    return lax.dot_general(a, b, (((1,), (1,)), ((), ())), preferred_element_type=F32)


def _rope_kernel(pos_ref, pat_ref, c_ref, s_ref):
    pos = pos_ref[...].astype(F32)
    ang = pos * pat_ref[0:1, :]
    c_ref[...] = pat_ref[1:2, :] + pat_ref[2:3, :] * jnp.cos(ang)
    s_ref[...] = pat_ref[3:4, :] * jnp.sin(ang)


def _rope_tables(positions, tm):
    t = positions.size
    half = MLA_ROPE // 2
    inv_freq = ROPE_THETA ** (-jnp.arange(half, dtype=F32) / half)
    z64, z32 = jnp.zeros((MLA_NOPE,), F32), jnp.zeros((HEAD_PAD - MLA_NOPE - MLA_ROPE,), F32)
    o16 = jnp.ones((half,), F32)
    pat = jnp.stack([
        jnp.concatenate([z64, inv_freq, inv_freq, z32]),
        jnp.concatenate([z64 + 1.0, 0 * o16, 0 * o16, z32]),
        jnp.concatenate([z64, o16, o16, z32]),
        jnp.concatenate([z64, -o16, o16, z32]),
    ])
    pat = jnp.concatenate([pat, jnp.zeros((4, HEAD_PAD), F32)], axis=0)
    return pl.pallas_call(
        _rope_kernel,
        grid=(t // tm,),
        in_specs=[pl.BlockSpec((tm, 1), lambda i: (i, 0)), pl.BlockSpec((8, HEAD_PAD), lambda i: (0, 0))],
        out_specs=[pl.BlockSpec((tm, HEAD_PAD), lambda i: (i, 0))] * 2,
        out_shape=[jax.ShapeDtypeStruct((t, HEAD_PAD), F32)] * 2,
        compiler_params=_cparams(("arbitrary",)),
        name="rope_tables",
    )(positions.reshape(t, 1), pat)


def _prep_kernel(x_ref, g_ref, w_ref, qg_ref, wq_ref, wqs_ref, kvg_ref, wk_ref, wv_ref, c_ref, s_ref,
                 convw_ref, wgate_ref, bgate_ref,
                 mq_ref, mk_ref, mv_ref, oconv_ref, dq_ref, dkk_ref, dvv_ref, iq_ref, ikr_ref, iw_ref,
                 gq_ref, gk_ref, gv_ref, gg_ref, gr_ref, u_scr, *, tm):
    j = pl.program_id(1)
    h = _rms(x_ref[...], g_ref[...]).astype(BF16)
    p = _dot(h, w_ref[...])

    c1, s1 = c_ref[...], s_ref[...]
    c4 = jnp.concatenate([c1] * MLA_HEADS, axis=1)
    s4 = jnp.concatenate([s1] * MLA_HEADS, axis=1)

    qn = _rms(p[:, P_CQ:P_CQ + MLA_Q_RANK], qg_ref[...]).astype(BF16)
    q = _dot(qn, wq_ref[...]) * c4 + _dot(qn, wqs_ref[...]) * s4
    mq_ref[...] = (q * ((MLA_NOPE + MLA_ROPE) ** -0.5 * LOG2E)).astype(BF16)
    kvn = _rms(p[:, P_CKV:P_CKV + MLA_KV_RANK], kvg_ref[...]).astype(BF16)
    kr = p[:, P_KRP:P_KRP + HEAD_PAD] * c1 + p[:, P_KRS:P_KRS + HEAD_PAD] * s1
    mk_ref[...] = (_dot(kvn, wk_ref[...]) + jnp.concatenate([kr] * MLA_HEADS, axis=1)).astype(BF16)
    vv = _dot(kvn, wv_ref[...])
    ones = jnp.ones((tm, LANE), F32)
    mv_ref[...] = jnp.concatenate([vv[:, 0:LANE], ones, vv[:, LANE:2 * LANE], ones], axis=1).astype(BF16)

    u = p[:, P_CC:P_CC + CONV_WIDTH] * p[:, P_CX:P_CX + CONV_WIDTH]

    @pl.when(j == 0)
    def _():
        u_scr[0:8, :] = jnp.zeros((8, CONV_WIDTH), F32)

    u_scr[8:8 + tm, :] = u
    cw = convw_ref[...]
    y = cw[0:1, :] * u_scr[6:6 + tm, :] + cw[1:2, :] * u_scr[7:7 + tm, :] + cw[2:3, :] * u
    oconv_ref[...] = (p[:, P_CB:P_CB + CONV_WIDTH] * y).astype(BF16)
    u_scr[0:8, :] = u_scr[tm:tm + 8, :]

    dq_ref[...] = (p[:, P_DQ:P_DQ + 256] * (DSA_HEAD_DIM ** -0.5 * LOG2E)).astype(BF16)
    dkk_ref[...] = p[:, P_DKK:P_DKK + LANE].astype(BF16)
    hi_lanes = lax.broadcasted_iota(I32, (tm, LANE), 1) >= DSA_HEAD_DIM
    dvv_ref[...] = jnp.where(hi_lanes, 1.0, p[:, P_DVV:P_DVV + LANE]).astype(BF16)
    iq_ref[...] = p[:, P_IQ:P_IQ + 256].astype(BF16)
    ikr_ref[...] = p[:, P_IKR:P_IKR + LANE].astype(BF16)
    iw_ref[...] = p[:, P_IW:P_IW + LANE] * ((IDX_DIM ** -0.5) * (IDX_HEADS ** -0.5))

    gq_ref[...] = p[:, P_GQ:P_GQ + LANE] * (GLA_DK ** -0.5)
    gk_ref[...] = p[:, P_GK:P_GK + LANE]
    gv_ref[...] = p[:, P_GV:P_GV + 256]
    gate = _dot(p[:, P_GLR:P_GLR + LANE].astype(BF16), wgate_ref[...]) + bgate_ref[...]
    log_sig = jnp.minimum(gate, 0.0) - jnp.log1p(jnp.exp(-jnp.abs(gate)))
    gg_ref[...] = log_sig * (1.0 / GLA_TAU)
    gr = p[:, P_GR:P_GR + 256]
    gr_ref[...] = gr / (1.0 + jnp.exp(-gr))


def _pack_small_weight(w_in):
    def cols(a, b):
        return w_in[..., a:b]

    def zeros(n):
        return jnp.zeros(w_in.shape[:-1] + (n,), w_in.dtype)

    half = MLA_ROPE // 2
    kr1, kr2 = cols(O_KR, O_KR + half), cols(O_KR + half, O_KR + MLA_ROPE)
    pad_tail = HEAD_PAD - MLA_NOPE - MLA_ROPE
    pieces = [
        cols(O_CQ, O_CKV), cols(O_CKV, O_KR),
        zeros(MLA_NOPE), kr1, kr2, zeros(pad_tail),
        zeros(MLA_NOPE), kr2, kr1, zeros(pad_tail),
        cols(O_CB, O_CC), cols(O_CC, O_CX), cols(O_CX, O_DQ),
        cols(O_DQ, O_DK), cols(O_DK, O_DV), cols(O_DK, O_DV), cols(O_DV, O_IQ), zeros(DSA_HEAD_DIM),
        cols(O_IQ, O_IK), cols(O_IK, O_IW), cols(O_IK, O_IW), cols(O_IK, O_IW), cols(O_IK, O_IW),
        cols(O_IW, O_GQ), zeros(LANE - IDX_HEADS),
        cols(O_GQ, O_GK), cols(O_GK, O_GV), cols(O_GV, O_GLR),
        cols(O_GLR, O_GR), zeros(LANE - GLA_GATE_RANK),
        cols(O_GR, O_GATE),
    ]
    out = jnp.concatenate(pieces, axis=-1)
    assert out.shape[-1] == P_TOTAL, out.shape
    return out.astype(BF16)


def _pack_mla_weights(w_q_up, w_kv_up):
    half = MLA_ROPE // 2
    lq = w_q_up.shape[:-1]
    lk = w_kv_up.shape[:-1]
    wq = w_q_up.reshape(lq + (MLA_HEADS, MLA_NOPE + MLA_ROPE))
    nope, x1, x2 = wq[..., :MLA_NOPE], wq[..., MLA_NOPE:MLA_NOPE + half], wq[..., MLA_NOPE + half:]
    zq = jnp.zeros(lq + (MLA_HEADS, HEAD_PAD - MLA_NOPE - MLA_ROPE), w_q_up.dtype)
    wq_pad = jnp.concatenate([nope, x1, x2, zq], axis=-1).reshape(lq + (MLA_HEADS * HEAD_PAD,))
    wq_swap = jnp.concatenate([jnp.zeros_like(nope), x2, x1, zq], axis=-1).reshape(lq + (MLA_HEADS * HEAD_PAD,))
    wkv = w_kv_up.reshape(lk + (MLA_HEADS, MLA_NOPE + MLA_V))
    zk = jnp.zeros(lk + (MLA_HEADS, HEAD_PAD - MLA_NOPE), w_kv_up.dtype)
    wk_pad = jnp.concatenate([wkv[..., :MLA_NOPE], zk], axis=-1).reshape(lk + (MLA_HEADS * HEAD_PAD,))
    wv = wkv[..., MLA_NOPE:].reshape(lk + (MLA_HEADS * MLA_V,))
    return wq_pad.astype(BF16), wq_swap.astype(BF16), wk_pad.astype(BF16), wv.astype(BF16)


def _prep(x2, nb, g, w_small, qg, wq, wqs, kvg, wk, wv, ctab, stab, conv_w, wgate, bgate, tm):
    t, d = x2.shape
    nj = t // nb // tm

    def tok(width):
        return pl.BlockSpec((tm, width), lambda b, j: (b * nj + j, 0))

    def full(a):
        return pl.BlockSpec(a.shape, lambda b, j: (0,) * a.ndim, pipeline_mode=pl.Buffered(1))

    outs = [(512, BF16), (512, BF16), (512, BF16), (256, BF16), (256, BF16), (LANE, BF16), (LANE, BF16),
            (256, BF16), (LANE, BF16), (LANE, F32), (LANE, F32), (LANE, F32), (256, F32), (LANE, F32), (256, F32)]
    return pl.pallas_call(
        functools.partial(_prep_kernel, tm=tm),
        grid=(nb, nj),
        in_specs=[tok(d), full(g), full(w_small), full(qg), full(wq), full(wqs), full(kvg), full(wk), full(wv),
                  tok(HEAD_PAD), tok(HEAD_PAD), full(conv_w), full(wgate), full(bgate)],
        out_specs=[tok(w) for w, _ in outs],
        out_shape=[jax.ShapeDtypeStruct((t, w), dt) for w, dt in outs],
        scratch_shapes=[pltpu.VMEM((tm + 8, CONV_WIDTH), F32)],
        compiler_params=_cparams(("arbitrary", "arbitrary")),
        name="prep",
    )(x2, g, w_small, qg, wq, wqs, kvg, wk, wv, ctab, stab, conv_w, wgate, bgate)


def _for_chunks(n, body):
    def group(i, carry):
        for u in range(4):
            body(4 * i + u, carry)
        return carry

    lax.fori_loop(0, n // 4, group, 0)
    rem = n % 4

    @pl.when(rem >= 2)
    def _():
        body(n - rem, 0)
        body(n - rem + 1, 0)

    @pl.when(rem % 2 == 1)
    def _():
        body(n - 1, 0)


def _fold_lanes(x, op):
    out = x[:, 0:LANE]
    for g in range(1, x.shape[1] // LANE):
        out = op(out, x[:, g * LANE:(g + 1) * LANE])
    return out


def _mla_kernel(q_ref, k_ref, v_ref, o_ref, s_scr, m_scr, acc_scr, *, tq, tk):
    i = pl.program_id(1)
    n_chunks = ((i + 1) * tq + tk - 1) // tk
    row = i * tq + lax.broadcasted_iota(I32, (tq, tk), 0)
    col = lax.broadcasted_iota(I32, (tq, tk), 1)
    lane = lax.broadcasted_iota(I32, (tq, LANE), 1)
    heads = range(MLA_HEADS)
    hs = [slice(h * HEAD_PAD, (h + 1) * HEAD_PAD) for h in heads]
    vs = [slice((h // 2) * 2 * LANE, (h // 2 + 1) * 2 * LANE) for h in heads]

    m_scr[...] = jnp.full(m_scr.shape, NEG, F32)
    acc_scr[...] = jnp.zeros(acc_scr.shape, F32)

    def phase1(c, carry):
        off = pl.multiple_of(c * tk, tk)
        causal = c * tk + col <= row
        for h in heads:
            s = jnp.where(causal, _dot_nt(q_ref[:, hs[h]], k_ref[pl.ds(off, tk), hs[h]]), NEG)
            s_scr[h, c] = s
            m_scr[h] = jnp.maximum(m_scr[h], _fold_lanes(s, jnp.maximum))
        return carry

    _for_chunks(n_chunks, phase1)
    for h in heads:
        m_scr[h] = jnp.broadcast_to(jnp.max(m_scr[h], axis=1, keepdims=True), (tq, LANE))

    def phase2(c, carry):
        off = pl.multiple_of(c * tk, tk)
        for h in heads:
            p = jnp.exp2(s_scr[h, c] - jnp.concatenate([m_scr[h]] * (tk // LANE), axis=1))
            acc_scr[h] = acc_scr[h] + _dot(p.astype(BF16), v_ref[pl.ds(off, tk), vs[h]])
        return carry

    _for_chunks(n_chunks, phase2)
    res = [acc_scr[h, :, 0:LANE] / acc_scr[h, :, LANE:2 * LANE] for h in heads]
    for pair in range(MLA_HEADS // 2):
        o_ref[:, pair * LANE:(pair + 1) * LANE] = jnp.where(lane < MLA_V, res[2 * pair], res[2 * pair + 1]).astype(BF16)


def _mla(mq, mk, mv, nb, tq, tk):
    t = mq.shape[0]
    s = t // nb
    nq = s // tq
    return pl.pallas_call(
        functools.partial(_mla_kernel, tq=tq, tk=tk),
        grid=(nb, nq),
        in_specs=[pl.BlockSpec((tq, MLA_HEADS * HEAD_PAD), lambda b, i: (b * nq + i, 0)),
                  pl.BlockSpec((s, MLA_HEADS * HEAD_PAD), lambda b, i: (b, 0)),
                  pl.BlockSpec((s, MLA_HEADS * HEAD_PAD), lambda b, i: (b, 0))],
        out_specs=pl.BlockSpec((tq, MLA_HEADS * MLA_V), lambda b, i: (b * nq + i, 0)),
        out_shape=jax.ShapeDtypeStruct((t, MLA_HEADS * MLA_V), BF16),
        scratch_shapes=[pltpu.VMEM((MLA_HEADS, s // tk, tq, tk), F32),
                        pltpu.VMEM((MLA_HEADS, tq, LANE), F32),
                        pltpu.VMEM((MLA_HEADS, tq, 2 * LANE), F32)],
        compiler_params=_cparams(("arbitrary", "arbitrary")),
        name="mla_attn",
    )(mq, mk, mv)


def _dsa_kernel(iq_ref, iw_ref, ikr_ref, dq_ref, dkk_ref, dvv_ref, tri_ref, o_ref,
                key_scr, plane_scr, eq_scr, gt_scr, thr_scr, s_scr, m_scr, acc_scr,
                *, tq, th, tk, topk):
    i = pl.program_id(1)
    q_end = (i + 1) * tq
    n_chunks = (q_end + tk - 1) // tk
    lane_q = lax.broadcasted_iota(I32, (th, LANE), 1)
    lane_k = lax.broadcasted_iota(I32, (tq, tk), 1)
    int_min = jnp.int32(-2 ** 31)
    rb = 64
    lane_rb = lax.broadcasted_iota(I32, (rb, tk), 1)
    row_rb = lax.broadcasted_iota(I32, (rb, 1), 0)

    for r0 in range(0, tq, th):
        iq = iq_ref[r0:r0 + th, :].astype(F32)
        iw = iw_ref[r0:r0 + th, :]
        q_heads = []
        w_heads = []
        for h in range(IDX_HEADS):
            grp = h // 4
            sub = iq[:, grp * LANE:(grp + 1) * LANE]
            lo = (h % 4) * IDX_DIM
            in_head = (lane_q >= lo) & (lane_q < lo + IDX_DIM)
            q_heads.append(jnp.where(in_head, sub, 0.0).astype(BF16))
            w_heads.append(iw[:, h:h + 1])
        q_stacks = [jnp.concatenate([qh[r:r + rb] for qh in q_heads], axis=0) for r in range(0, th, rb)]

        def score_chunk(c, _, r0=r0, q_stacks=q_stacks, w_heads=w_heads):
            off = pl.multiple_of(c * tk, tk)
            kc = ikr_ref[pl.ds(off, tk), :]
            for bi, r in enumerate(range(0, th, rb)):
                z = _dot_nt(q_stacks[bi], kc)
                acc = w_heads[0][r:r + rb] * jnp.maximum(z[0:rb], 0.0)
                for h in range(1, IDX_HEADS):
                    acc = acc + w_heads[h][r:r + rb] * jnp.maximum(z[h * rb:(h + 1) * rb], 0.0)
                bits = pltpu.bitcast(acc, I32)
                key = jnp.where(bits < 0, bits ^ jnp.int32(0x7FFFFFFF), bits)
                key = jnp.where(c * tk + lane_rb <= i * tq + r0 + r + row_rb, key, int_min)
                key_scr[c, r0 + r:r0 + r + rb, :] = key
            return 0

        _for_chunks(n_chunks, score_chunk)

    kf = jnp.float32(topk)
    n_groups = key_scr.shape[0] * (tk // LANE)
    cb = 128

    def row_total(cnt):
        return _dot(cnt.astype(F32).astype(BF16), jnp.ones((LANE, LANE), BF16))

    def select():
        def fill(c, _):
            key_scr[c] = jnp.full((tq, tk), int_min, I32)
            return 0
        few_groups = n_chunks * (tk // LANE) <= 16
        lax.fori_loop(n_chunks, jnp.where(few_groups, min(16 // (tk // LANE), key_scr.shape[0]), key_scr.shape[0]),
                      fill, 0)

        stage_mask = {16: 0x0000FFFF, 8: 0x00FF00FF, 4: 0x0F0F0F0F, 2: 0x33333333, 1: 0x55555555}

        def build_planes(blk, _, nw):
            r = pl.multiple_of(blk * 8, 8)
            w = []
            for g in range(nw):
                if g < n_groups:
                    gc, gl = divmod(g, tk // LANE)
                    w.append(key_scr[gc, pl.ds(r, 8), gl * LANE:(gl + 1) * LANE] ^ int_min)
                else:
                    w.append(jnp.zeros((8, LANE), I32))
            j = nw // 2
            while j:
                k = 0
                while k < nw:
                    t = (w[k] ^ lax.shift_right_logical(w[k + j], jnp.int32(j))) & jnp.int32(stage_mask[j])
                    w[k] = w[k] ^ t
                    w[k + j] = w[k + j] ^ lax.shift_left(t, jnp.int32(j))
                    k = (k + j + 1) & ~j
                j >>= 1
            if nw == 16:
                w = [lax.shift_right_logical(u, jnp.int32(16)) for u in w] + [u & jnp.int32(0xFFFF) for u in w]
            for b in range(32):
                plane_scr[b, pl.ds(r, 8), :] = w[b]
            return 0

        @pl.when(few_groups)
        def _():
            lax.fori_loop(0, tq // 8, functools.partial(build_planes, nw=16), 0)

        if n_groups > 16:
            @pl.when(jnp.logical_not(few_groups))
            def _():
                lax.fori_loop(0, tq // 8, functools.partial(build_planes, nw=32), 0)

        eq_scr[...] = jnp.full((tq, LANE), -1, I32)
        gt_scr[...] = jnp.zeros((tq, LANE), I32)
        thr_scr[...] = jnp.zeros((tq, LANE), I32)

        def bit_pass(it, _):
            bit = lax.shift_left(jnp.int32(1), 31 - it)
            for r in range(0, tq, cb):
                rows = slice(r, r + cb)
                eq, gtm = eq_scr[rows, :], gt_scr[rows, :]
                t = eq & plane_scr[it, rows, :]
                u = gtm | t
                take = jnp.sum(lax.population_count(u).astype(F32), axis=1, keepdims=True) >= kf
                eq_scr[rows, :] = jnp.where(take, t, eq ^ t)
                gt_scr[rows, :] = jnp.where(take, gtm, u)
                thr_scr[rows, :] = jnp.where(take, thr_scr[rows, :] | bit, thr_scr[rows, :])
            return 0
        lax.fori_loop(0, 32, bit_pass, 0)

        thr_u = thr_scr[:, 0:1]
        thr = jnp.maximum(thr_u ^ int_min, int_min + 1)
        c_gt = row_total(lax.population_count(gt_scr[...]))[:, 0:1]
        c_ge = c_gt + row_total(lax.population_count(eq_scr[...]))[:, 0:1]
        excess = (c_ge > kf) & (thr_u != 0)

        need = kf - c_gt

        def tie_break(r):
            rows = slice(r, r + cb)
            thr_b, need_b = thr[rows], need[rows]

            def demote(c, seen):
                kk = key_scr[c, rows, :]
                tie = kk == thr_b
                rank = seen + _dot(jnp.where(tie, 1.0, 0.0).astype(BF16), tri_ref[...])
                key_scr[c, rows, :] = jnp.where(tie & (rank > need_b), thr_b - 1, kk)
                return rank[:, tk - 1:tk]
            lax.fori_loop(0, n_chunks, demote, jnp.zeros((cb, 1), F32))

        for r in range(0, tq, cb):
            pl.when(jnp.max(jnp.where(excess[r:r + cb], 1.0, 0.0)) > 0.0)(functools.partial(tie_break, r))
        return thr

    thr = lax.cond(q_end > topk, select, lambda: jnp.full((tq, 1), int_min + 1, I32))

    heads = range(DSA_HEADS)
    for r0 in range(0, tq, th):
        rows = slice(r0, r0 + th)
        thr_h = thr[rows]
        dq = dq_ref[rows, :].astype(F32)
        qm = []
        for h in heads:
            sub = dq[:, (h // 2) * LANE:(h // 2 + 1) * LANE]
            lo = (h % 2) * DSA_HEAD_DIM
            in_head = (lane_q >= lo) & (lane_q < lo + DSA_HEAD_DIM)
            qm.append(jnp.where(in_head, sub, 0.0).astype(BF16))
        qm_stack = jnp.concatenate(qm, axis=0)

        m_scr[...] = jnp.full(m_scr.shape, NEG, F32)
        acc_scr[...] = jnp.zeros(acc_scr.shape, F32)

        def phase1(c, carry, rows=rows, thr_h=thr_h, qm_stack=qm_stack):
            off = pl.multiple_of(c * tk, tk)
            sel = key_scr[c, rows, :] >= thr_h
            s = _dot_nt(qm_stack, dkk_ref[pl.ds(off, tk), :])
            for h in heads:
                sm = jnp.where(sel, s[h * th:(h + 1) * th], NEG)
                s_scr[h, c] = sm
                m_scr[h] = jnp.maximum(m_scr[h], _fold_lanes(sm, jnp.maximum))
            return carry

        _for_chunks(n_chunks, phase1)
        for h in heads:
            m_scr[h] = jnp.broadcast_to(jnp.max(m_scr[h], axis=1, keepdims=True), (th, LANE))

        def phase2(c, carry):
            off = pl.multiple_of(c * tk, tk)
            ps = []
            for h in heads:
                p = jnp.exp2(s_scr[h, c] - jnp.concatenate([m_scr[h]] * (tk // LANE), axis=1))
                ps.append(p.astype(BF16))
            pv = _dot(jnp.concatenate(ps, axis=0), dvv_ref[pl.ds(off, tk), :])
            for h in heads:
                acc_scr[h] = acc_scr[h] + pv[h * th:(h + 1) * th]
            return carry

        _for_chunks(n_chunks, phase2)
        outs = [acc_scr[h] / pltpu.roll(acc_scr[h], DSA_HEAD_DIM, 1) for h in heads]
        lo_half = lane_q < DSA_HEAD_DIM
        o_ref[rows, :] = jnp.concatenate(
            [jnp.where(lo_half, outs[0], pltpu.roll(outs[1], DSA_HEAD_DIM, 1)),
             jnp.where(lo_half, outs[2], pltpu.roll(outs[3], DSA_HEAD_DIM, 1))], axis=1).astype(BF16)


def _dsa(iq, iw, ikr, dq, dkk, dvv, nb, tq, th, tk):
    t = iq.shape[0]
    s = t // nb
    nq = s // tq
    topk = min(DSA_TOPK_MAX, s // 4)
    assert s // LANE <= 32, "the bit-sliced selection packs one key per 128-lane group into a 32-bit word"
    tri = jnp.asarray(np.triu(np.ones((tk, tk), np.float32)), BF16)

    def qspec(w):
        return pl.BlockSpec((tq, w), lambda b, i: (b * nq + i, 0))

    def kspec(w):
        return pl.BlockSpec((s, w), lambda b, i: (b, 0))

    return pl.pallas_call(
        functools.partial(_dsa_kernel, tq=tq, th=th, tk=tk, topk=topk),
        grid=(nb, nq),
        in_specs=[qspec(256), qspec(LANE), kspec(LANE), qspec(256), kspec(LANE), kspec(LANE),
                  pl.BlockSpec((tk, tk), lambda b, i: (0, 0), pipeline_mode=pl.Buffered(1))],
        out_specs=qspec(256),
        out_shape=jax.ShapeDtypeStruct((t, DSA_HEADS * DSA_HEAD_DIM), BF16),
        scratch_shapes=[pltpu.VMEM((s // tk, tq, tk), I32),
                        pltpu.VMEM((32, tq, LANE), I32),
                        pltpu.VMEM((tq, LANE), I32),
                        pltpu.VMEM((tq, LANE), I32),
                        pltpu.VMEM((tq, LANE), I32),
                        pltpu.VMEM((DSA_HEADS, s // tk, th, tk), F32)] + [pltpu.VMEM((DSA_HEADS, th, LANE), F32)] * 2,
        compiler_params=_cparams(("arbitrary", "arbitrary")),
        name="dsa",
    )(iq, iw, ikr, dq, dkk, dvv, tri)


def _gla_kernel(gq_ref, gk_ref, gv_ref, gg_ref, gr_ref, ng_ref, segv_ref, seg64_ref, bmask_ref, o_ref,
                st_scr, kbuf, cbuf, vbuf, *, tg):
    j = pl.program_id(1)
    ch = GLA_CHUNK
    rowc = lax.broadcasted_iota(I32, (tg, 1), 0) % ch

    @pl.when(j == 0)
    def _():
        st_scr[...] = jnp.zeros_like(st_scr)
        kbuf[0:ch, :] = jnp.zeros((ch, LANE), F32)
        cbuf[0:ch, :] = jnp.zeros((ch, LANE), F32)
        vbuf[0:ch, :] = jnp.zeros((ch, 2 * LANE), F32)

    cbuf[ch:ch + tg, :] = gg_ref[...]
    sh = 1
    while sh < ch:
        cur = cbuf[ch:ch + tg, :]
        prev = cbuf[ch - sh:ch - sh + tg, :]
        cbuf[ch:ch + tg, :] = cur + jnp.where(rowc >= sh, prev, 0.0)
        sh *= 2
    cum = cbuf[ch:ch + tg, :]
    q = gq_ref[...]
    k = gk_ref[...]
    v = gv_ref[...]
    kbuf[ch:ch + tg, :] = k
    vbuf[ch:ch + tg, :] = v

    qd = (q * jnp.exp(cum)).astype(BF16)
    bmask = bmask_ref[...]
    inter = []
    for c in range(tg // ch):
        rs = slice(c * ch, (c + 1) * ch)
        st = st_scr[...]
        inter.append(_dot_nt(qd[rs], st.astype(BF16)))
        last = cum[c * ch + ch - 1:c * ch + ch, :]
        kp = k[rs] * jnp.exp(last - cum[rs])
        upd = lax.dot_general(v[rs].astype(BF16), kp.astype(BF16), (((0,), (0,)), ((), ())),
                              preferred_element_type=F32)
        st_scr[...] = jnp.exp(last) * st + upd * bmask
    o = jnp.concatenate(inter, axis=0)

    segv = segv_ref[...]
    cum2 = cum * LOG2E
    cbuf[ch:ch + tg, :] = cum2
    sub = 16
    rows_sub = rowc % sub
    for d in range(sub):
        kd = kbuf[ch - d:ch - d + tg, :]
        cd = cbuf[ch - d:ch - d + tg, :]
        vd = vbuf[ch - d:ch - d + tg, :]
        e = jnp.exp2(jnp.where(rows_sub >= d, cum2 - cd, NEG))
        o = o + _dot((q * kd * e).astype(BF16), segv) * vd

    lane_k3 = lax.broadcasted_iota(I32, (1, 3 * LANE), 1) % LANE // GLA_DK
    lane_v = lax.broadcasted_iota(I32, (1, 2 * LANE), 1) // GLA_DV
    far = []
    for c in range(tg // ch):
        rs = slice(c * ch, (c + 1) * ch)
        qc, kc, cc, sb = q[rs], k[rs], cum2[rs], rowc[rs] // sub
        qh, kh = [], []
        for blk in range(1, ch // sub):
            ref_row = cc[blk * sub - 1:blk * sub, :]
            qh.append(qc * jnp.exp2(jnp.where(sb == blk, cc - ref_row, NEG)))
            kh.append(kc * jnp.exp2(jnp.where(sb < blk, ref_row - cc, NEG)))
        qh = jnp.concatenate(qh, axis=1)
        kh = jnp.concatenate(kh, axis=1).astype(BF16)
        q_heads = jnp.concatenate([jnp.where(lane_k3 == h, qh, 0.0) for h in range(GLA_HEADS)], axis=0)
        att = _dot_nt(q_heads.astype(BF16), kh)
        ov = _dot(att.astype(BF16), v[rs].astype(BF16))
        oc = jnp.where(lane_v == 0, ov[0:ch], 0.0)
        for h in range(1, GLA_HEADS):
            oc = oc + jnp.where(lane_v == h, ov[h * ch:(h + 1) * ch], 0.0)
        far.append(oc)
    o = o + jnp.concatenate(far, axis=0)

    sq = o * o
    hi = sq.astype(BF16)
    lo = (sq - hi.astype(F32)).astype(BF16)
    var = _dot(hi, seg64_ref[...]) + _dot(lo, seg64_ref[...])
    o_ref[...] = (gr_ref[...] * (o * lax.rsqrt(var + EPS) * ng_ref[...])).astype(BF16)


def _gla(gq, gk, gv, gg, gr, norm_g, nb, tg):
    t = gq.shape[0]
    nj = t // nb // tg
    ng = jnp.tile(norm_g.reshape(1, GLA_DV), (1, GLA_HEADS)).astype(F32)
    hk = np.arange(GLA_HEADS * GLA_DK) // GLA_DK
    hv = np.arange(GLA_HEADS * GLA_DV) // GLA_DV
    same = (hk[:, None] == hv[None, :])
    segv = jnp.asarray(same, BF16)
    bmask = jnp.asarray(same.T, F32)
    seg64 = jnp.asarray((hv[:, None] == hv[None, :]) / GLA_DV, BF16)

    def tok(w):
        return pl.BlockSpec((tg, w), lambda b, j: (b * nj + j, 0))

    def full(a):
        return pl.BlockSpec(a.shape, lambda b, j: (0,) * a.ndim, pipeline_mode=pl.Buffered(1))

    return pl.pallas_call(
        functools.partial(_gla_kernel, tg=tg),
        grid=(nb, nj),
        in_specs=[tok(LANE), tok(LANE), tok(256), tok(LANE), tok(256), full(ng), full(segv), full(seg64), full(bmask)],
        out_specs=tok(256),
        out_shape=jax.ShapeDtypeStruct((t, GLA_HEADS * GLA_DV), BF16),
        scratch_shapes=[pltpu.VMEM((2 * LANE, LANE), F32),
                        pltpu.VMEM((GLA_CHUNK + tg, LANE), F32),
                        pltpu.VMEM((GLA_CHUNK + tg, LANE), F32),
                        pltpu.VMEM((GLA_CHUNK + tg, 2 * LANE), F32)],
        compiler_params=_cparams(("arbitrary", "arbitrary")),
        name="gla",
    )(gq, gk, gv, gg, gr, ng, segv, seg64, bmask)


def _merge_kernel(x_ref, g_ref, wg_ref, o0_ref, o1_ref, o2_ref, o3_ref, wb_ref, wo_ref, out_ref):
    x = x_ref[...]
    h = _rms(x, g_ref[...]).astype(BF16)
    merged = None
    for n, o_ref in enumerate((o0_ref, o1_ref, o2_ref, o3_ref)):
        gate = _dot(h, wg_ref[:, n * D_MODEL:(n + 1) * D_MODEL])
        gate = 1.0 / (1.0 + jnp.exp(-gate))
        term = gate * _dot(o_ref[...], wb_ref[n])
        merged = term if merged is None else merged + term
    out_ref[...] = x + _dot(merged.astype(BF16), wo_ref[...])


def _merge(x2, g, wg, branches, wb, wo, tm):
    t, d = x2.shape

    def tok(w):
        return pl.BlockSpec((tm, w), lambda i: (i, 0))

    def full(a):
        return pl.BlockSpec(a.shape, lambda i: (0,) * a.ndim, pipeline_mode=pl.Buffered(1))

    return pl.pallas_call(
        _merge_kernel,
        grid=(t // tm,),
        in_specs=[tok(d), full(g), full(wg)] + [tok(BRANCH_WIDTH)] * N_BRANCH + [full(wb), full(wo)],
        out_specs=tok(d),
        out_shape=jax.ShapeDtypeStruct((t, d), F32),
        compiler_params=_cparams(("arbitrary",)),
        name="merge",
    )(x2, g, wg, *branches, wb, wo)


def _ffn_kernel(x_ref, g_ref, wg_ref, wu_ref, wd_ref, fg_ref, out_ref, *, fc, final):
    x = x_ref[...]
    h = _rms(x, g_ref[...]).astype(BF16)
    acc = x
    for c in range(D_FF // fc):
        gate = _dot(h, wg_ref[:, c * fc:(c + 1) * fc])
        up = _dot(h, wu_ref[:, c * fc:(c + 1) * fc])
        act = (gate / (1.0 + jnp.exp(-gate)) * up).astype(BF16)
        acc = acc + _dot(act, wd_ref[c * fc:(c + 1) * fc, :])
    if final:
        acc = _rms(acc, fg_ref[...])
    out_ref[...] = acc


def _ffn(x2, g, wg, wu, wd, fg, tm, final):
    t, d = x2.shape

    def tok(w):
        return pl.BlockSpec((tm, w), lambda i: (i, 0))

    def full(a):
        return pl.BlockSpec(a.shape, lambda i: (0,) * a.ndim, pipeline_mode=pl.Buffered(1))

    return pl.pallas_call(
        functools.partial(_ffn_kernel, fc=256, final=final),
        grid=(t // tm,),
        in_specs=[tok(d), full(g), full(wg), full(wu), full(wd), full(fg)],
        out_specs=tok(d),
        out_shape=jax.ShapeDtypeStruct((t, d), F32),
        compiler_params=_cparams(("arbitrary",)),
        name="ffn",
    )(x2, g, wg, wu, wd, fg)


def _tile(n, pref):
    while n % pref:
        pref //= 2
    return pref


def kernel(x, positions, attn_norm_g, w_in, mla_q_norm_g, mla_w_q_up, mla_kv_norm_g, mla_w_kv_up, conv_w, gla_w_gate_up, gla_b_gate, gla_norm_g, w_branch, w_out, ffn_norm_g, w_ffn_gate, w_ffn_up, w_ffn_down, final_norm_g):
    b, s, d = x.shape
    depth = w_in.shape[0]
    t = b * s
    assert d == D_MODEL and s % GLA_CHUNK == 0
    tm_prep = _tile(s, 512)
    tm_wide = _tile(t, 1024)
    tq_mla = _tile(s, 256)
    tk_att = _tile(s, 512)
    tq_dsa = _tile(s, 512)
    th_dsa = _tile(tq_dsa, 256)
    tg = _tile(s, 512)

    x2 = x.reshape(t, d)
    ctab, stab = _rope_tables(positions, _tile(t, 512))
    row = lambda a: a.reshape(1, -1).astype(F32)

    w_small = _pack_small_weight(w_in)
    wq, wqs, wk, wv = _pack_mla_weights(mla_w_q_up, mla_w_kv_up)
    wgate = jnp.concatenate([gla_w_gate_up, jnp.zeros((depth, LANE - GLA_GATE_RANK, GLA_HEADS * GLA_DK), F32)],
                            axis=1).astype(BF16)
    w_gates = w_in[:, :, O_GATE:].astype(BF16)
    w_branch_b, w_out_b = w_branch.astype(BF16), w_out.astype(BF16)
    w_gate_b, w_up_b = w_ffn_gate.astype(BF16), w_ffn_up.astype(BF16)
    w_down_b = w_ffn_down.astype(BF16)

    for l in range(depth):
        (mq, mk, mv, o_conv, dq, dkk, dvv, iq, ikr, iw, gq, gk, gv, gg, gr) = _prep(
            x2, b, row(attn_norm_g[l]), w_small[l], row(mla_q_norm_g[l]), wq[l], wqs[l], row(mla_kv_norm_g[l]),
            wk[l], wv[l], ctab, stab, conv_w[l].astype(F32), wgate[l], row(gla_b_gate[l]), tm_prep)
        o_mla = _mla(mq, mk, mv, b, tq_mla, tk_att)
        o_dsa = _dsa(iq, iw, ikr, dq, dkk, dvv, b, tq_dsa, th_dsa, tk_att)
        o_gla = _gla(gq, gk, gv, gg, gr, gla_norm_g[l], b, tg)
        x2 = _merge(x2, row(attn_norm_g[l]), w_gates[l], (o_mla, o_conv, o_dsa, o_gla), w_branch_b[l], w_out_b[l],
                    tm_wide)
        x2 = _ffn(x2, row(ffn_norm_g[l]), w_gate_b[l], w_up_b[l], w_down_b[l], row(final_norm_g), tm_wide,
                  final=(l == depth - 1))
    return x2.reshape(b, s, d)
```

```python
import functools

import jax
import jax.numpy as jnp
import numpy as np
from jax import lax
from jax.experimental import pallas as pl
from jax.experimental.pallas import tpu as pltpu

F32 = jnp.float32
BF16 = jnp.bfloat16
I32 = jnp.int32

D_MODEL = 1024
MLA_HEADS = 4
MLA_Q_RANK = 256
MLA_KV_RANK = 128
MLA_NOPE = 64
MLA_ROPE = 32
MLA_V = 64
ROPE_THETA = 10000.0
CONV_WIDTH = 256
CONV_K = 3
DSA_HEADS = 4
DSA_HEAD_DIM = 64
IDX_HEADS = 8
IDX_DIM = 32
DSA_TOPK_MAX = 256
GLA_HEADS = 4
GLA_DK = 32
GLA_DV = 64
GLA_GATE_RANK = 16
GLA_TAU = 16.0
GLA_CHUNK = 64
N_BRANCH = 4
BRANCH_WIDTH = 256
D_FF = ((8 * D_MODEL + 3 * 256 - 1) // (3 * 256)) * 256
EPS = 1e-6
NEG = -1e30
LOG2E = 1.4426950408889634

LANE = 128
MXU_WIDTH = 256
HEAD_PAD = 128
W_MLA = MLA_HEADS * HEAD_PAD
W_DQ = DSA_HEADS * DSA_HEAD_DIM
W_IQ = IDX_HEADS * IDX_DIM
W_GV = GLA_HEADS * GLA_DV

IN_SIZES = (
    MLA_Q_RANK, MLA_KV_RANK, MLA_ROPE,
    CONV_WIDTH, CONV_WIDTH, CONV_WIDTH,
    DSA_HEADS * DSA_HEAD_DIM, DSA_HEAD_DIM, DSA_HEAD_DIM,
    IDX_HEADS * IDX_DIM, IDX_DIM, IDX_HEADS,
    GLA_HEADS * GLA_DK, GLA_HEADS * GLA_DK, GLA_HEADS * GLA_DV,
    GLA_GATE_RANK, GLA_HEADS * GLA_DV,
    N_BRANCH * D_MODEL,
)
_OFF = np.concatenate([[0], np.cumsum(IN_SIZES)]).astype(int)
(O_CQ, O_CKV, O_KR, O_CB, O_CC, O_CX, O_DQ, O_DK, O_DV, O_IQ, O_IK, O_IW,
 O_GQ, O_GK, O_GV, O_GLR, O_GR, O_GATE, O_END) = [int(v) for v in _OFF]

P_CQ, P_CKV, P_KRP, P_KRS = 0, 256, 384, 512
P_CB, P_CC, P_CX = 640, 896, 1152
P_DQ, P_DKK, P_DVV = 1408, 1664, 1792
P_IQ, P_IKR, P_IW = 1920, 2176, 2304
P_GQ, P_GK, P_GV, P_GLR, P_GR = 2432, 2560, 2688, 2944, 3072
P_TOTAL = 3328

VMEM_LIMIT = 56 * 1024 * 1024


def _cparams(sem):
    return pltpu.CompilerParams(dimension_semantics=sem, vmem_limit_bytes=VMEM_LIMIT)


def _rms(x, g):
    return x * lax.rsqrt(jnp.mean(x * x, axis=-1, keepdims=True) + EPS) * g


def _dot(a, b):
    return jnp.dot(a, b, preferred_element_type=F32)


def _dot_nt(a, b):
    return lax.dot_general(a, b, (((1,), (1,)), ((), ())), preferred_element_type=F32)


def _rope_kernel(pos_ref, pat_ref, c_ref, s_ref):
    pos = pos_ref[...].astype(F32)
    ang = pos * pat_ref[0:1, :]
    c_ref[...] = pat_ref[1:2, :] + pat_ref[2:3, :] * jnp.cos(ang)
    s_ref[...] = pat_ref[3:4, :] * jnp.sin(ang)


def _rope_tables(positions, tm):
    t = positions.size
    half = MLA_ROPE // 2
    inv_freq = ROPE_THETA ** (-jnp.arange(half, dtype=F32) / half)
    z64, z32 = jnp.zeros((MLA_NOPE,), F32), jnp.zeros((HEAD_PAD - MLA_NOPE - MLA_ROPE,), F32)
    o16 = jnp.ones((half,), F32)
    pat = jnp.stack([
        jnp.concatenate([z64, inv_freq, inv_freq, z32]),
        jnp.concatenate([z64 + 1.0, 0 * o16, 0 * o16, z32]),
        jnp.concatenate([z64, o16, o16, z32]),
        jnp.concatenate([z64, -o16, o16, z32]),
    ])
    pat = jnp.concatenate([pat, jnp.zeros((4, HEAD_PAD), F32)], axis=0)
    return pl.pallas_call(
        _rope_kernel,
        grid=(t // tm,),
        in_specs=[pl.BlockSpec((tm, 1), lambda i: (i, 0)), pl.BlockSpec((8, HEAD_PAD), lambda i: (0, 0))],
        out_specs=[pl.BlockSpec((tm, HEAD_PAD), lambda i: (i, 0))] * 2,
        out_shape=[jax.ShapeDtypeStruct((t, HEAD_PAD), F32)] * 2,
        compiler_params=_cparams(("arbitrary",)),
        name="rope_tables",
    )(positions.reshape(t, 1), pat)


def _prep_kernel(x_ref, g_ref, w_ref, qg_ref, wq_ref, wqs_ref, kvg_ref, wk_ref, wv_ref, c_ref, s_ref,
                 convw_ref, wgate_ref, bgate_ref,
                 mq_ref, mk_ref, mv_ref, oconv_ref, dq_ref, dkk_ref, dvv_ref, iq_ref, ikr_ref, iw_ref,
                 gq_ref, gk_ref, gv_ref, gg_ref, gr_ref, u_scr, *, tm):
    j = pl.program_id(1)
    h = _rms(x_ref[...], g_ref[...]).astype(BF16)
    p = _dot(h, w_ref[...])

    c1, s1 = c_ref[...], s_ref[...]
    c4 = jnp.concatenate([c1] * MLA_HEADS, axis=1)
    s4 = jnp.concatenate([s1] * MLA_HEADS, axis=1)

    qn = _rms(p[:, P_CQ:P_CQ + MLA_Q_RANK], qg_ref[...]).astype(BF16)
    q = _dot(qn, wq_ref[...]) * c4 + _dot(qn, wqs_ref[...]) * s4
    mq_ref[...] = (q * ((MLA_NOPE + MLA_ROPE) ** -0.5 * LOG2E)).astype(BF16)
    kvn = _rms(p[:, P_CKV:P_CKV + MLA_KV_RANK], kvg_ref[...]).astype(BF16)
    kr = p[:, P_KRP:P_KRP + HEAD_PAD] * c1 + p[:, P_KRS:P_KRS + HEAD_PAD] * s1
    mk_ref[...] = (_dot(kvn, wk_ref[...]) + jnp.concatenate([kr] * MLA_HEADS, axis=1)).astype(BF16)
    vv = _dot(kvn, wv_ref[...])
    ones = jnp.ones((tm, LANE), F32)
    mv_ref[...] = jnp.concatenate([vv[:, 0:LANE], ones, vv[:, LANE:2 * LANE], ones], axis=1).astype(BF16)

    u = p[:, P_CC:P_CC + CONV_WIDTH] * p[:, P_CX:P_CX + CONV_WIDTH]

    @pl.when(j == 0)
    def _():
        u_scr[0:8, :] = jnp.zeros((8, CONV_WIDTH), F32)

    u_scr[8:8 + tm, :] = u
    cw = convw_ref[...]
    y = cw[0:1, :] * u_scr[6:6 + tm, :] + cw[1:2, :] * u_scr[7:7 + tm, :] + cw[2:3, :] * u
    oconv_ref[...] = (p[:, P_CB:P_CB + CONV_WIDTH] * y).astype(BF16)
    u_scr[0:8, :] = u_scr[tm:tm + 8, :]

    dq_ref[...] = (p[:, P_DQ:P_DQ + W_DQ] * (DSA_HEAD_DIM ** -0.5 * LOG2E)).astype(BF16)
    dkk_ref[...] = p[:, P_DKK:P_DKK + LANE].astype(BF16)
    hi_lanes = lax.broadcasted_iota(I32, (tm, LANE), 1) >= DSA_HEAD_DIM
    dvv_ref[...] = jnp.where(hi_lanes, 1.0, p[:, P_DVV:P_DVV + LANE]).astype(BF16)
    iq_ref[...] = p[:, P_IQ:P_IQ + W_IQ].astype(BF16)
    ikr_ref[...] = p[:, P_IKR:P_IKR + LANE].astype(BF16)
    iw_ref[...] = p[:, P_IW:P_IW + LANE] * ((IDX_DIM ** -0.5) * (IDX_HEADS ** -0.5))

    gq_ref[...] = p[:, P_GQ:P_GQ + LANE] * (GLA_DK ** -0.5)
    gk_ref[...] = p[:, P_GK:P_GK + LANE]
    gv_ref[...] = p[:, P_GV:P_GV + W_GV]
    gate = _dot(p[:, P_GLR:P_GLR + LANE].astype(BF16), wgate_ref[...]) + bgate_ref[...]
    log_sig = jnp.minimum(gate, 0.0) - jnp.log1p(jnp.exp(-jnp.abs(gate)))
    gg_ref[...] = log_sig * (1.0 / GLA_TAU)
    gr = p[:, P_GR:P_GR + W_GV]
    gr_ref[...] = gr / (1.0 + jnp.exp(-gr))


def _pack_small_weight(w_in):
    def cols(a, b):
        return w_in[..., a:b]

    def zeros(n):
        return jnp.zeros(w_in.shape[:-1] + (n,), w_in.dtype)

    half = MLA_ROPE // 2
    kr1, kr2 = cols(O_KR, O_KR + half), cols(O_KR + half, O_KR + MLA_ROPE)
    pad_tail = HEAD_PAD - MLA_NOPE - MLA_ROPE
    pieces = [
        cols(O_CQ, O_CKV), cols(O_CKV, O_KR),
        zeros(MLA_NOPE), kr1, kr2, zeros(pad_tail),
        zeros(MLA_NOPE), kr2, kr1, zeros(pad_tail),
        cols(O_CB, O_CC), cols(O_CC, O_CX), cols(O_CX, O_DQ),
        cols(O_DQ, O_DK), cols(O_DK, O_DV), cols(O_DK, O_DV), cols(O_DV, O_IQ), zeros(DSA_HEAD_DIM),
        cols(O_IQ, O_IK), cols(O_IK, O_IW), cols(O_IK, O_IW), cols(O_IK, O_IW), cols(O_IK, O_IW),
        cols(O_IW, O_GQ), zeros(LANE - IDX_HEADS),
        cols(O_GQ, O_GK), cols(O_GK, O_GV), cols(O_GV, O_GLR),
        cols(O_GLR, O_GR), zeros(LANE - GLA_GATE_RANK),
        cols(O_GR, O_GATE),
    ]
    out = jnp.concatenate(pieces, axis=-1)
    assert out.shape[-1] == P_TOTAL, out.shape
    return out.astype(BF16)


def _pack_mla_weights(w_q_up, w_kv_up):
    half = MLA_ROPE // 2
    lq = w_q_up.shape[:-1]
    lk = w_kv_up.shape[:-1]
    wq = w_q_up.reshape(lq + (MLA_HEADS, MLA_NOPE + MLA_ROPE))
    nope, x1, x2 = wq[..., :MLA_NOPE], wq[..., MLA_NOPE:MLA_NOPE + half], wq[..., MLA_NOPE + half:]
    zq = jnp.zeros(lq + (MLA_HEADS, HEAD_PAD - MLA_NOPE - MLA_ROPE), w_q_up.dtype)
    wq_pad = jnp.concatenate([nope, x1, x2, zq], axis=-1).reshape(lq + (MLA_HEADS * HEAD_PAD,))
    wq_swap = jnp.concatenate([jnp.zeros_like(nope), x2, x1, zq], axis=-1).reshape(lq + (MLA_HEADS * HEAD_PAD,))
    wkv = w_kv_up.reshape(lk + (MLA_HEADS, MLA_NOPE + MLA_V))
    zk = jnp.zeros(lk + (MLA_HEADS, HEAD_PAD - MLA_NOPE), w_kv_up.dtype)
    wk_pad = jnp.concatenate([wkv[..., :MLA_NOPE], zk], axis=-1).reshape(lk + (MLA_HEADS * HEAD_PAD,))
    wv = wkv[..., MLA_NOPE:].reshape(lk + (MLA_HEADS * MLA_V,))
    return wq_pad.astype(BF16), wq_swap.astype(BF16), wk_pad.astype(BF16), wv.astype(BF16)


def _prep(x2, nb, g, w_small, qg, wq, wqs, kvg, wk, wv, ctab, stab, conv_w, wgate, bgate, tm):
    t, d = x2.shape
    nj = t // nb // tm

    def tok(width):
        return pl.BlockSpec((tm, width), lambda b, j: (b * nj + j, 0))

    def full(a):
        return pl.BlockSpec(a.shape, lambda b, j: (0,) * a.ndim, pipeline_mode=pl.Buffered(1))

    outs = [(W_MLA, BF16), (W_MLA, BF16), (W_MLA, BF16), (CONV_WIDTH, BF16), (W_DQ, BF16), (LANE, BF16), (LANE, BF16),
            (W_IQ, BF16), (LANE, BF16), (LANE, F32), (LANE, F32), (LANE, F32), (W_GV, F32), (LANE, F32), (W_GV, F32)]
    return pl.pallas_call(
        functools.partial(_prep_kernel, tm=tm),
        grid=(nb, nj),
        in_specs=[tok(d), full(g), full(w_small), full(qg), full(wq), full(wqs), full(kvg), full(wk), full(wv),
                  tok(HEAD_PAD), tok(HEAD_PAD), full(conv_w), full(wgate), full(bgate)],
        out_specs=[tok(w) for w, _ in outs],
        out_shape=[jax.ShapeDtypeStruct((t, w), dt) for w, dt in outs],
        scratch_shapes=[pltpu.VMEM((tm + 8, CONV_WIDTH), F32)],
        compiler_params=_cparams(("arbitrary", "arbitrary")),
        name="prep",
    )(x2, g, w_small, qg, wq, wqs, kvg, wk, wv, ctab, stab, conv_w, wgate, bgate)


def _for_chunks(n, body):
    def group(i, carry):
        for u in range(4):
            body(4 * i + u, carry)
        return carry

    lax.fori_loop(0, n // 4, group, 0)
    rem = n % 4

    @pl.when(rem >= 2)
    def _():
        body(n - rem, 0)
        body(n - rem + 1, 0)

    @pl.when(rem % 2 == 1)
    def _():
        body(n - 1, 0)


def _fold_lanes(x, op):
    out = x[:, 0:LANE]
    for g in range(1, x.shape[1] // LANE):
        out = op(out, x[:, g * LANE:(g + 1) * LANE])
    return out


def _mla_kernel(q_ref, k_ref, v_ref, o_ref, s_scr, m_scr, acc_scr, *, tq, tk):
    i = pl.program_id(1)
    n_chunks = ((i + 1) * tq + tk - 1) // tk
    row = i * tq + lax.broadcasted_iota(I32, (tq, tk), 0)
    col = lax.broadcasted_iota(I32, (tq, tk), 1)
    lane = lax.broadcasted_iota(I32, (tq, LANE), 1)
    heads = range(MLA_HEADS)
    hs = [slice(h * HEAD_PAD, (h + 1) * HEAD_PAD) for h in heads]
    vs = [slice((h // 2) * 2 * LANE, (h // 2 + 1) * 2 * LANE) for h in heads]

    m_scr[...] = jnp.full(m_scr.shape, NEG, F32)
    acc_scr[...] = jnp.zeros(acc_scr.shape, F32)

    def phase1(c, carry):
        off = pl.multiple_of(c * tk, tk)
        causal = c * tk + col <= row
        for h in heads:
            s = jnp.where(causal, _dot_nt(q_ref[:, hs[h]], k_ref[pl.ds(off, tk), hs[h]]), NEG)
            s_scr[h, c] = s
            m_scr[h] = jnp.maximum(m_scr[h], _fold_lanes(s, jnp.maximum))
        return carry

    _for_chunks(n_chunks, phase1)
    for h in heads:
        m_scr[h] = jnp.broadcast_to(jnp.max(m_scr[h], axis=1, keepdims=True), (tq, LANE))

    def phase2(c, carry):
        off = pl.multiple_of(c * tk, tk)
        for h in heads:
            p = jnp.exp2(s_scr[h, c] - jnp.concatenate([m_scr[h]] * (tk // LANE), axis=1))
            acc_scr[h] = acc_scr[h] + _dot(p.astype(BF16), v_ref[pl.ds(off, tk), vs[h]])
        return carry

    _for_chunks(n_chunks, phase2)
    res = [acc_scr[h, :, 0:LANE] / acc_scr[h, :, LANE:2 * LANE] for h in heads]
    for pair in range(MLA_HEADS // 2):
        o_ref[:, pair * LANE:(pair + 1) * LANE] = jnp.where(lane < MLA_V, res[2 * pair], res[2 * pair + 1]).astype(BF16)


def _mla(mq, mk, mv, nb, tq, tk):
    t = mq.shape[0]
    s = t // nb
    nq = s // tq
    return pl.pallas_call(
        functools.partial(_mla_kernel, tq=tq, tk=tk),
        grid=(nb, nq),
        in_specs=[pl.BlockSpec((tq, MLA_HEADS * HEAD_PAD), lambda b, i: (b * nq + i, 0)),
                  pl.BlockSpec((s, MLA_HEADS * HEAD_PAD), lambda b, i: (b, 0)),
                  pl.BlockSpec((s, MLA_HEADS * HEAD_PAD), lambda b, i: (b, 0))],
        out_specs=pl.BlockSpec((tq, MLA_HEADS * MLA_V), lambda b, i: (b * nq + i, 0)),
        out_shape=jax.ShapeDtypeStruct((t, MLA_HEADS * MLA_V), BF16),
        scratch_shapes=[pltpu.VMEM((MLA_HEADS, s // tk, tq, tk), F32),
                        pltpu.VMEM((MLA_HEADS, tq, LANE), F32),
                        pltpu.VMEM((MLA_HEADS, tq, 2 * LANE), F32)],
        compiler_params=_cparams(("arbitrary", "arbitrary")),
        name="mla_attn",
    )(mq, mk, mv)


def _dsa_kernel(iq_ref, iw_ref, ikr_ref, dq_ref, dkk_ref, dvv_ref, tri_ref, o_ref,
                key_scr, plane_scr, eq_scr, gt_scr, thr_scr, s_scr, m_scr, acc_scr,
                *, tq, th, tk, topk):
    i = pl.program_id(1)
    q_end = (i + 1) * tq
    n_chunks = (q_end + tk - 1) // tk
    lane_q = lax.broadcasted_iota(I32, (th, LANE), 1)
    int_min = jnp.int32(-2 ** 31)
    rb = 64
    lane_rb = lax.broadcasted_iota(I32, (rb, tk), 1)
    row_rb = lax.broadcasted_iota(I32, (rb, 1), 0)

    for r0 in range(0, tq, th):
        iq = iq_ref[r0:r0 + th, :].astype(F32)
        iw = iw_ref[r0:r0 + th, :]
        q_heads = []
        w_heads = []
        for h in range(IDX_HEADS):
            grp = h // 4
            sub = iq[:, grp * LANE:(grp + 1) * LANE]
            lo = (h % 4) * IDX_DIM
            in_head = (lane_q >= lo) & (lane_q < lo + IDX_DIM)
            q_heads.append(jnp.where(in_head, sub, 0.0).astype(BF16))
            w_heads.append(iw[:, h:h + 1])
        q_stacks = [jnp.concatenate([qh[r:r + rb] for qh in q_heads], axis=0) for r in range(0, th, rb)]

        def score_chunk(c, _, r0=r0, q_stacks=q_stacks, w_heads=w_heads):
            off = pl.multiple_of(c * tk, tk)
            kc = ikr_ref[pl.ds(off, tk), :]
            for bi, r in enumerate(range(0, th, rb)):
                z = _dot_nt(q_stacks[bi], kc)
                acc = w_heads[0][r:r + rb] * jnp.maximum(z[0:rb], 0.0)
                for h in range(1, IDX_HEADS):
                    acc = acc + w_heads[h][r:r + rb] * jnp.maximum(z[h * rb:(h + 1) * rb], 0.0)
                bits = pltpu.bitcast(acc, I32)
                key = jnp.where(bits < 0, bits ^ jnp.int32(0x7FFFFFFF), bits)
                key = jnp.where(c * tk + lane_rb <= i * tq + r0 + r + row_rb, key, int_min)
                key_scr[c, r0 + r:r0 + r + rb, :] = key
            return 0

        _for_chunks(n_chunks, score_chunk)

    kf = jnp.float32(topk)
    n_groups = key_scr.shape[0] * (tk // LANE)
    cb = 128

    def row_total(cnt):
        return _dot(cnt.astype(F32).astype(BF16), jnp.ones((LANE, LANE), BF16))

    def select():
        def fill(c, _):
            key_scr[c] = jnp.full((tq, tk), int_min, I32)
            return 0
        few_groups = n_chunks * (tk // LANE) <= 16
        lax.fori_loop(n_chunks, jnp.where(few_groups, min(16 // (tk // LANE), key_scr.shape[0]), key_scr.shape[0]),
                      fill, 0)

        stage_mask = {16: 0x0000FFFF, 8: 0x00FF00FF, 4: 0x0F0F0F0F, 2: 0x33333333, 1: 0x55555555}

        def build_planes(blk, _, nw):
            r = pl.multiple_of(blk * 8, 8)
            w = []
            for g in range(nw):
                if g < n_groups:
                    gc, gl = divmod(g, tk // LANE)
                    w.append(key_scr[gc, pl.ds(r, 8), gl * LANE:(gl + 1) * LANE] ^ int_min)
                else:
                    w.append(jnp.zeros((8, LANE), I32))
            j = nw // 2
            while j:
                k = 0
                while k < nw:
                    t = (w[k] ^ lax.shift_right_logical(w[k + j], jnp.int32(j))) & jnp.int32(stage_mask[j])
                    w[k] = w[k] ^ t
                    w[k + j] = w[k + j] ^ lax.shift_left(t, jnp.int32(j))
                    k = (k + j + 1) & ~j
                j >>= 1
            if nw == 16:
                w = [lax.shift_right_logical(u, jnp.int32(16)) for u in w] + [u & jnp.int32(0xFFFF) for u in w]
            for b in range(32):
                plane_scr[b, pl.ds(r, 8), :] = w[b]
            return 0

        @pl.when(few_groups)
        def _():
            lax.fori_loop(0, tq // 8, functools.partial(build_planes, nw=16), 0)

        if n_groups > 16:
            @pl.when(jnp.logical_not(few_groups))
            def _():
                lax.fori_loop(0, tq // 8, functools.partial(build_planes, nw=32), 0)

        eq_scr[...] = jnp.full((tq, LANE), -1, I32)
        gt_scr[...] = jnp.zeros((tq, LANE), I32)
        thr_scr[...] = jnp.zeros((tq, LANE), I32)

        def bit_pass(it, _):
            bit = lax.shift_left(jnp.int32(1), 31 - it)
            for r in range(0, tq, cb):
                rows = slice(r, r + cb)
                eq, gtm = eq_scr[rows, :], gt_scr[rows, :]
                t = eq & plane_scr[it, rows, :]
                u = gtm | t
                take = jnp.sum(lax.population_count(u).astype(F32), axis=1, keepdims=True) >= kf
                eq_scr[rows, :] = jnp.where(take, t, eq ^ t)
                gt_scr[rows, :] = jnp.where(take, gtm, u)
                thr_scr[rows, :] = jnp.where(take, thr_scr[rows, :] | bit, thr_scr[rows, :])
            return 0
        lax.fori_loop(0, 32, bit_pass, 0)

        thr_u = thr_scr[:, 0:1]
        thr = jnp.maximum(thr_u ^ int_min, int_min + 1)
        c_gt = row_total(lax.population_count(gt_scr[...]))[:, 0:1]
        c_ge = c_gt + row_total(lax.population_count(eq_scr[...]))[:, 0:1]
        excess = (c_ge > kf) & (thr_u != 0)

        need = kf - c_gt

        def tie_break(r):
            rows = slice(r, r + cb)
            thr_b, need_b = thr[rows], need[rows]

            def demote(c, seen):
                kk = key_scr[c, rows, :]
                tie = kk == thr_b
                rank = seen + _dot(jnp.where(tie, 1.0, 0.0).astype(BF16), tri_ref[...])
                key_scr[c, rows, :] = jnp.where(tie & (rank > need_b), thr_b - 1, kk)
                return rank[:, tk - 1:tk]
            lax.fori_loop(0, n_chunks, demote, jnp.zeros((cb, 1), F32))

        for r in range(0, tq, cb):
            pl.when(jnp.max(jnp.where(excess[r:r + cb], 1.0, 0.0)) > 0.0)(functools.partial(tie_break, r))
        return thr

    thr = lax.cond(q_end > topk, select, lambda: jnp.full((tq, 1), int_min + 1, I32))

    heads = range(DSA_HEADS)
    for r0 in range(0, tq, th):
        rows = slice(r0, r0 + th)
        thr_h = thr[rows]
        dq = dq_ref[rows, :].astype(F32)
        qm = []
        for h in heads:
            sub = dq[:, (h // 2) * LANE:(h // 2 + 1) * LANE]
            lo = (h % 2) * DSA_HEAD_DIM
            in_head = (lane_q >= lo) & (lane_q < lo + DSA_HEAD_DIM)
            qm.append(jnp.where(in_head, sub, 0.0).astype(BF16))
        qm_stack = jnp.concatenate(qm, axis=0)

        m_scr[...] = jnp.full(m_scr.shape, NEG, F32)
        acc_scr[...] = jnp.zeros(acc_scr.shape, F32)

        def phase1(c, carry, rows=rows, thr_h=thr_h, qm_stack=qm_stack):
            off = pl.multiple_of(c * tk, tk)
            sel = key_scr[c, rows, :] >= thr_h
            s = _dot_nt(qm_stack, dkk_ref[pl.ds(off, tk), :])
            for h in heads:
                sm = jnp.where(sel, s[h * th:(h + 1) * th], NEG)
                s_scr[h, c] = sm
                m_scr[h] = jnp.maximum(m_scr[h], _fold_lanes(sm, jnp.maximum))
            return carry

        _for_chunks(n_chunks, phase1)
        for h in heads:
            m_scr[h] = jnp.broadcast_to(jnp.max(m_scr[h], axis=1, keepdims=True), (th, LANE))

        def phase2(c, carry):
            off = pl.multiple_of(c * tk, tk)
            ps = []
            for h in heads:
                p = jnp.exp2(s_scr[h, c] - jnp.concatenate([m_scr[h]] * (tk // LANE), axis=1))
                ps.append(p.astype(BF16))
            pv = _dot(jnp.concatenate(ps, axis=0), dvv_ref[pl.ds(off, tk), :])
            for h in heads:
                acc_scr[h] = acc_scr[h] + pv[h * th:(h + 1) * th]
            return carry

        _for_chunks(n_chunks, phase2)
        outs = [acc_scr[h] / pltpu.roll(acc_scr[h], DSA_HEAD_DIM, 1) for h in heads]
        lo_half = lane_q < DSA_HEAD_DIM
        o_ref[rows, :] = jnp.concatenate(
            [jnp.where(lo_half, outs[0], pltpu.roll(outs[1], DSA_HEAD_DIM, 1)),
             jnp.where(lo_half, outs[2], pltpu.roll(outs[3], DSA_HEAD_DIM, 1))], axis=1).astype(BF16)


def _dsa(iq, iw, ikr, dq, dkk, dvv, nb, tq, th, tk):
    t = iq.shape[0]
    s = t // nb
    nq = s // tq
    topk = min(DSA_TOPK_MAX, s // 4)
    assert s // LANE <= 32, "the bit-sliced selection packs one key per 128-lane group into a 32-bit word"
    tri = jnp.asarray(np.triu(np.ones((tk, tk), np.float32)), BF16)

    def qspec(w):
        return pl.BlockSpec((tq, w), lambda b, i: (b * nq + i, 0))

    def kspec(w):
        return pl.BlockSpec((s, w), lambda b, i: (b, 0))

    return pl.pallas_call(
        functools.partial(_dsa_kernel, tq=tq, th=th, tk=tk, topk=topk),
        grid=(nb, nq),
        in_specs=[qspec(W_IQ), qspec(LANE), kspec(LANE), qspec(W_DQ), kspec(LANE), kspec(LANE),
                  pl.BlockSpec((tk, tk), lambda b, i: (0, 0), pipeline_mode=pl.Buffered(1))],
        out_specs=qspec(W_DQ),
        out_shape=jax.ShapeDtypeStruct((t, DSA_HEADS * DSA_HEAD_DIM), BF16),
        scratch_shapes=[pltpu.VMEM((s // tk, tq, tk), I32),
                        pltpu.VMEM((32, tq, LANE), I32),
                        pltpu.VMEM((tq, LANE), I32),
                        pltpu.VMEM((tq, LANE), I32),
                        pltpu.VMEM((tq, LANE), I32),
                        pltpu.VMEM((DSA_HEADS, s // tk, th, tk), F32)] + [pltpu.VMEM((DSA_HEADS, th, LANE), F32)] * 2,
        compiler_params=_cparams(("arbitrary", "arbitrary")),
        name="dsa",
    )(iq, iw, ikr, dq, dkk, dvv, tri)


def _gla_kernel(gq_ref, gk_ref, gv_ref, gg_ref, gr_ref, ng_ref, segv_ref, seg64_ref, bmask_ref, o_ref,
                st_scr, kbuf, cbuf, vbuf, *, tg):
    j = pl.program_id(1)
    ch = GLA_CHUNK
    rowc = lax.broadcasted_iota(I32, (tg, 1), 0) % ch

    @pl.when(j == 0)
    def _():
        st_scr[...] = jnp.zeros_like(st_scr)
        kbuf[0:ch, :] = jnp.zeros((ch, LANE), F32)
        cbuf[0:ch, :] = jnp.zeros((ch, LANE), F32)
        vbuf[0:ch, :] = jnp.zeros((ch, 2 * LANE), F32)

    cbuf[ch:ch + tg, :] = gg_ref[...]
    sh = 1
    while sh < ch:
        cur = cbuf[ch:ch + tg, :]
        prev = cbuf[ch - sh:ch - sh + tg, :]
        cbuf[ch:ch + tg, :] = cur + jnp.where(rowc >= sh, prev, 0.0)
        sh *= 2
    cum = cbuf[ch:ch + tg, :]
    q = gq_ref[...]
    k = gk_ref[...]
    v = gv_ref[...]
    kbuf[ch:ch + tg, :] = k
    vbuf[ch:ch + tg, :] = v

    qd = (q * jnp.exp(cum)).astype(BF16)
    bmask = bmask_ref[...]
    inter = []
    for c in range(tg // ch):
        rs = slice(c * ch, (c + 1) * ch)
        st = st_scr[...]
        inter.append(_dot_nt(qd[rs], st.astype(BF16)))
        last = cum[c * ch + ch - 1:c * ch + ch, :]
        kp = k[rs] * jnp.exp(last - cum[rs])
        upd = lax.dot_general(v[rs].astype(BF16), kp.astype(BF16), (((0,), (0,)), ((), ())),
                              preferred_element_type=F32)
        st_scr[...] = jnp.exp(last) * st + upd * bmask
    o = jnp.concatenate(inter, axis=0)

    segv = segv_ref[...]
    cum2 = cum * LOG2E
    cbuf[ch:ch + tg, :] = cum2
    sub = 16
    rows_sub = rowc % sub
    for d in range(sub):
        kd = kbuf[ch - d:ch - d + tg, :]
        cd = cbuf[ch - d:ch - d + tg, :]
        vd = vbuf[ch - d:ch - d + tg, :]
        e = jnp.exp2(jnp.where(rows_sub >= d, cum2 - cd, NEG))
        o = o + _dot((q * kd * e).astype(BF16), segv) * vd

    lane_k3 = lax.broadcasted_iota(I32, (1, 3 * LANE), 1) % LANE // GLA_DK
    lane_v = lax.broadcasted_iota(I32, (1, 2 * LANE), 1) // GLA_DV
    far = []
    for c in range(tg // ch):
        rs = slice(c * ch, (c + 1) * ch)
        qc, kc, cc, sb = q[rs], k[rs], cum2[rs], rowc[rs] // sub
        qh, kh = [], []
        for blk in range(1, ch // sub):
            ref_row = cc[blk * sub - 1:blk * sub, :]
            qh.append(qc * jnp.exp2(jnp.where(sb == blk, cc - ref_row, NEG)))
            kh.append(kc * jnp.exp2(jnp.where(sb < blk, ref_row - cc, NEG)))
        qh = jnp.concatenate(qh, axis=1)
        kh = jnp.concatenate(kh, axis=1).astype(BF16)
        q_heads = jnp.concatenate([jnp.where(lane_k3 == h, qh, 0.0) for h in range(GLA_HEADS)], axis=0)
        att = _dot_nt(q_heads.astype(BF16), kh)
        ov = _dot(att.astype(BF16), v[rs].astype(BF16))
        oc = jnp.where(lane_v == 0, ov[0:ch], 0.0)
        for h in range(1, GLA_HEADS):
            oc = oc + jnp.where(lane_v == h, ov[h * ch:(h + 1) * ch], 0.0)
        far.append(oc)
    o = o + jnp.concatenate(far, axis=0)

    sq = o * o
    hi = sq.astype(BF16)
    lo = (sq - hi.astype(F32)).astype(BF16)
    var = _dot(hi, seg64_ref[...]) + _dot(lo, seg64_ref[...])
    o_ref[...] = (gr_ref[...] * (o * lax.rsqrt(var + EPS) * ng_ref[...])).astype(BF16)


def _gla(gq, gk, gv, gg, gr, norm_g, nb, tg):
    t = gq.shape[0]
    nj = t // nb // tg
    ng = jnp.tile(norm_g.reshape(1, GLA_DV), (1, GLA_HEADS)).astype(F32)
    hk = np.arange(GLA_HEADS * GLA_DK) // GLA_DK
    hv = np.arange(GLA_HEADS * GLA_DV) // GLA_DV
    same = (hk[:, None] == hv[None, :])
    segv = jnp.asarray(same, BF16)
    bmask = jnp.asarray(same.T, F32)
    seg64 = jnp.asarray((hv[:, None] == hv[None, :]) / GLA_DV, BF16)

    def tok(w):
        return pl.BlockSpec((tg, w), lambda b, j: (b * nj + j, 0))

    def full(a):
        return pl.BlockSpec(a.shape, lambda b, j: (0,) * a.ndim, pipeline_mode=pl.Buffered(1))

    return pl.pallas_call(
        functools.partial(_gla_kernel, tg=tg),
        grid=(nb, nj),
        in_specs=[tok(LANE), tok(LANE), tok(W_GV), tok(LANE), tok(W_GV), full(ng), full(segv), full(seg64),
                  full(bmask)],
        out_specs=tok(W_GV),
        out_shape=jax.ShapeDtypeStruct((t, GLA_HEADS * GLA_DV), BF16),
        scratch_shapes=[pltpu.VMEM((2 * LANE, LANE), F32),
                        pltpu.VMEM((GLA_CHUNK + tg, LANE), F32),
                        pltpu.VMEM((GLA_CHUNK + tg, LANE), F32),
                        pltpu.VMEM((GLA_CHUNK + tg, 2 * LANE), F32)],
        compiler_params=_cparams(("arbitrary", "arbitrary")),
        name="gla",
    )(gq, gk, gv, gg, gr, ng, segv, seg64, bmask)


def _merge_kernel(x_ref, g_ref, wg_ref, o0_ref, o1_ref, o2_ref, o3_ref, wb_ref, wo_ref, out_ref):
    x = x_ref[...]
    h = _rms(x, g_ref[...]).astype(BF16)
    merged = None
    for n, o_ref in enumerate((o0_ref, o1_ref, o2_ref, o3_ref)):
        gate = _dot(h, wg_ref[:, n * D_MODEL:(n + 1) * D_MODEL])
        gate = 1.0 / (1.0 + jnp.exp(-gate))
        term = gate * _dot(o_ref[...], wb_ref[n])
        merged = term if merged is None else merged + term
    out_ref[...] = x + _dot(merged.astype(BF16), wo_ref[...])


def _merge(x2, g, wg, branches, wb, wo, tm):
    t, d = x2.shape

    def tok(w):
        return pl.BlockSpec((tm, w), lambda i: (i, 0))

    def full(a):
        return pl.BlockSpec(a.shape, lambda i: (0,) * a.ndim, pipeline_mode=pl.Buffered(1))

    return pl.pallas_call(
        _merge_kernel,
        grid=(t // tm,),
        in_specs=[tok(d), full(g), full(wg)] + [tok(BRANCH_WIDTH)] * N_BRANCH + [full(wb), full(wo)],
        out_specs=tok(d),
        out_shape=jax.ShapeDtypeStruct((t, d), F32),
        compiler_params=_cparams(("arbitrary",)),
        name="merge",
    )(x2, g, wg, *branches, wb, wo)


def _ffn_kernel(x_ref, g_ref, wg_ref, wu_ref, wd_ref, fg_ref, out_ref, *, fc, final):
    x = x_ref[...]
    h = _rms(x, g_ref[...]).astype(BF16)
    acc = x
    for c in range(D_FF // fc):
        gate = _dot(h, wg_ref[:, c * fc:(c + 1) * fc])
        up = _dot(h, wu_ref[:, c * fc:(c + 1) * fc])
        act = (gate / (1.0 + jnp.exp(-gate)) * up).astype(BF16)
        acc = acc + _dot(act, wd_ref[c * fc:(c + 1) * fc, :])
    if final:
        acc = _rms(acc, fg_ref[...])
    out_ref[...] = acc


def _ffn(x2, g, wg, wu, wd, fg, tm, final):
    t, d = x2.shape

    def tok(w):
        return pl.BlockSpec((tm, w), lambda i: (i, 0))

    def full(a):
        return pl.BlockSpec(a.shape, lambda i: (0,) * a.ndim, pipeline_mode=pl.Buffered(1))

    return pl.pallas_call(
        functools.partial(_ffn_kernel, fc=MXU_WIDTH, final=final),
        grid=(t // tm,),
        in_specs=[tok(d), full(g), full(wg), full(wu), full(wd), full(fg)],
        out_specs=tok(d),
        out_shape=jax.ShapeDtypeStruct((t, d), F32),
        compiler_params=_cparams(("arbitrary",)),
        name="ffn",
    )(x2, g, wg, wu, wd, fg)


def _tile(n, pref):
    while n % pref:
        pref //= 2
    return pref


def kernel(x, positions, attn_norm_g, w_in, mla_q_norm_g, mla_w_q_up, mla_kv_norm_g, mla_w_kv_up, conv_w, gla_w_gate_up, gla_b_gate, gla_norm_g, w_branch, w_out, ffn_norm_g, w_ffn_gate, w_ffn_up, w_ffn_down, final_norm_g):
    b, s, d = x.shape
    depth = w_in.shape[0]
    t = b * s
    assert d == D_MODEL and s % GLA_CHUNK == 0 and conv_w.shape[1] == CONV_K and w_in.shape[2] == O_END
    tm_prep = _tile(s, 512)
    tm_wide = _tile(t, 1024)
    tq_mla = _tile(s, 256)
    tk_att = _tile(s, 512)
    tq_dsa = _tile(s, 512)
    th_dsa = _tile(tq_dsa, 256)
    tg = _tile(s, 512)

    x2 = x.reshape(t, d)
    ctab, stab = _rope_tables(positions, _tile(t, 512))
    row = lambda a: a.reshape(1, -1).astype(F32)

    w_small = _pack_small_weight(w_in)
    wq, wqs, wk, wv = _pack_mla_weights(mla_w_q_up, mla_w_kv_up)
    wgate = jnp.concatenate([gla_w_gate_up, jnp.zeros((depth, LANE - GLA_GATE_RANK, GLA_HEADS * GLA_DK), F32)],
                            axis=1).astype(BF16)
    w_gates = w_in[:, :, O_GATE:].astype(BF16)
    w_branch_b, w_out_b = w_branch.astype(BF16), w_out.astype(BF16)
    w_gate_b, w_up_b = w_ffn_gate.astype(BF16), w_ffn_up.astype(BF16)
    w_down_b = w_ffn_down.astype(BF16)

    for l in range(depth):
        (mq, mk, mv, o_conv, dq, dkk, dvv, iq, ikr, iw, gq, gk, gv, gg, gr) = _prep(
            x2, b, row(attn_norm_g[l]), w_small[l], row(mla_q_norm_g[l]), wq[l], wqs[l], row(mla_kv_norm_g[l]),
            wk[l], wv[l], ctab, stab, conv_w[l].astype(F32), wgate[l], row(gla_b_gate[l]), tm_prep)
        o_mla = _mla(mq, mk, mv, b, tq_mla, tk_att)
        o_dsa = _dsa(iq, iw, ikr, dq, dkk, dvv, b, tq_dsa, th_dsa, tk_att)
        o_gla = _gla(gq, gk, gv, gg, gr, gla_norm_g[l], b, tg)
        x2 = _merge(x2, row(attn_norm_g[l]), w_gates[l], (o_mla, o_conv, o_dsa, o_gla), w_branch_b[l], w_out_b[l],
                    tm_wide)
        x2 = _ffn(x2, row(ffn_norm_g[l]), w_gate_b[l], w_up_b[l], w_down_b[l], row(final_norm_g), tm_wide,
                  final=(l == depth - 1))
    return x2.reshape(b, s, d)
```

```python
import functools

import jax
import jax.numpy as jnp
import numpy as np
from jax import lax
from jax.experimental import pallas as pl
from jax.experimental.pallas import tpu as pltpu

F32 = jnp.float32
BF16 = jnp.bfloat16
I32 = jnp.int32

D_MODEL = 1024
MLA_HEADS = 4
MLA_Q_RANK = 256
MLA_KV_RANK = 128
MLA_NOPE = 64
MLA_ROPE = 32
MLA_V = 64
ROPE_THETA = 10000.0
CONV_WIDTH = 256
CONV_K = 3
DSA_HEADS = 4
DSA_HEAD_DIM = 64
IDX_HEADS = 8
IDX_DIM = 32
DSA_TOPK_MAX = 256
GLA_HEADS = 4
GLA_DK = 32
GLA_DV = 64
GLA_GATE_RANK = 16
GLA_TAU = 16.0
GLA_CHUNK = 64
N_BRANCH = 4
BRANCH_WIDTH = 256
D_FF = ((8 * D_MODEL + 3 * 256 - 1) // (3 * 256)) * 256
EPS = 1e-6
NEG = -1e30
LOG2E = 1.4426950408889634

LANE = 128
MXU_WIDTH = 256
HEAD_PAD = 128
W_MLA = MLA_HEADS * HEAD_PAD
W_DQ = DSA_HEADS * DSA_HEAD_DIM
W_IQ = IDX_HEADS * IDX_DIM
W_GV = GLA_HEADS * GLA_DV

IN_SIZES = (
    MLA_Q_RANK, MLA_KV_RANK, MLA_ROPE,
    CONV_WIDTH, CONV_WIDTH, CONV_WIDTH,
    DSA_HEADS * DSA_HEAD_DIM, DSA_HEAD_DIM, DSA_HEAD_DIM,
    IDX_HEADS * IDX_DIM, IDX_DIM, IDX_HEADS,
    GLA_HEADS * GLA_DK, GLA_HEADS * GLA_DK, GLA_HEADS * GLA_DV,
    GLA_GATE_RANK, GLA_HEADS * GLA_DV,
    N_BRANCH * D_MODEL,
)
_OFF = np.concatenate([[0], np.cumsum(IN_SIZES)]).astype(int)
(O_CQ, O_CKV, O_KR, O_CB, O_CC, O_CX, O_DQ, O_DK, O_DV, O_IQ, O_IK, O_IW,
 O_GQ, O_GK, O_GV, O_GLR, O_GR, O_GATE, O_END) = [int(v) for v in _OFF]

P_CQ, P_CKV, P_KRP, P_KRS = 0, 256, 384, 512
P_CB, P_CC, P_CX = 640, 896, 1152
P_DQ, P_DKK, P_DVV = 1408, 1664, 1792
P_IQ, P_IKR, P_IW = 1920, 2176, 2304
P_GQ, P_GK, P_GV, P_GLR, P_GR = 2432, 2560, 2688, 2944, 3072
P_TOTAL = 3328

VMEM_LIMIT = 56 * 1024 * 1024


def _cparams(sem):
    return pltpu.CompilerParams(dimension_semantics=sem, vmem_limit_bytes=VMEM_LIMIT)


def _rms(x, g):
    return x * lax.rsqrt(jnp.mean(x * x, axis=-1, keepdims=True) + EPS) * g


def _dot(a, b):
    return jnp.dot(a, b, preferred_element_type=F32)


def _dot_nt(a, b):
    return lax.dot_general(a, b, (((1,), (1,)), ((), ())), preferred_element_type=F32)


def _rope_kernel(pos_ref, pat_ref, c_ref, s_ref):
    pos = pos_ref[...].astype(F32)
    ang = pos * pat_ref[0:1, :]
    c_ref[...] = pat_ref[1:2, :] + pat_ref[2:3, :] * jnp.cos(ang)
    s_ref[...] = pat_ref[3:4, :] * jnp.sin(ang)


def _rope_tables(positions, tm):
    t = positions.size
    half = MLA_ROPE // 2
    inv_freq = ROPE_THETA ** (-jnp.arange(half, dtype=F32) / half)
    z64, z32 = jnp.zeros((MLA_NOPE,), F32), jnp.zeros((HEAD_PAD - MLA_NOPE - MLA_ROPE,), F32)
    o16 = jnp.ones((half,), F32)
    pat = jnp.stack([
        jnp.concatenate([z64, inv_freq, inv_freq, z32]),
        jnp.concatenate([z64 + 1.0, 0 * o16, 0 * o16, z32]),
        jnp.concatenate([z64, o16, o16, z32]),
        jnp.concatenate([z64, -o16, o16, z32]),
    ])
    pat = jnp.concatenate([pat, jnp.zeros((4, HEAD_PAD), F32)], axis=0)
    return pl.pallas_call(
        _rope_kernel,
        grid=(t // tm,),
        in_specs=[pl.BlockSpec((tm, 1), lambda i: (i, 0)), pl.BlockSpec((8, HEAD_PAD), lambda i: (0, 0))],
        out_specs=[pl.BlockSpec((tm, HEAD_PAD), lambda i: (i, 0))] * 2,
        out_shape=[jax.ShapeDtypeStruct((t, HEAD_PAD), F32)] * 2,
        compiler_params=_cparams(("arbitrary",)),
        name="rope_tables",
    )(positions.reshape(t, 1), pat)


def _prep_kernel(x_ref, g_ref, w_ref, qg_ref, wq_ref, wqs_ref, kvg_ref, wk_ref, wv_ref, c_ref, s_ref,
                 convw_ref, wgate_ref, bgate_ref,
                 mq_ref, mk_ref, mv_ref, oconv_ref, dq_ref, dkk_ref, dvv_ref, iq_ref, ikr_ref, iw_ref,
                 gq_ref, gk_ref, gv_ref, gg_ref, gr_ref, u_scr, *, tm):
    j = pl.program_id(1)
    h = _rms(x_ref[...], g_ref[...]).astype(BF16)
    p = _dot(h, w_ref[...])

    c1, s1 = c_ref[...], s_ref[...]
    c4 = jnp.concatenate([c1] * MLA_HEADS, axis=1)
    s4 = jnp.concatenate([s1] * MLA_HEADS, axis=1)

    qn = _rms(p[:, P_CQ:P_CQ + MLA_Q_RANK], qg_ref[...]).astype(BF16)
    q = _dot(qn, wq_ref[...]) * c4 + _dot(qn, wqs_ref[...]) * s4
    mq_ref[...] = (q * ((MLA_NOPE + MLA_ROPE) ** -0.5 * LOG2E)).astype(BF16)
    kvn = _rms(p[:, P_CKV:P_CKV + MLA_KV_RANK], kvg_ref[...]).astype(BF16)
    kr = p[:, P_KRP:P_KRP + HEAD_PAD] * c1 + p[:, P_KRS:P_KRS + HEAD_PAD] * s1
    mk_ref[...] = (_dot(kvn, wk_ref[...]) + jnp.concatenate([kr] * MLA_HEADS, axis=1)).astype(BF16)
    vv = _dot(kvn, wv_ref[...])
    ones = jnp.ones((tm, LANE), F32)
    mv_ref[...] = jnp.concatenate([vv[:, 0:LANE], ones, vv[:, LANE:2 * LANE], ones], axis=1).astype(BF16)

    u = p[:, P_CC:P_CC + CONV_WIDTH] * p[:, P_CX:P_CX + CONV_WIDTH]

    @pl.when(j == 0)
    def _():
        u_scr[0:8, :] = jnp.zeros((8, CONV_WIDTH), F32)

    u_scr[8:8 + tm, :] = u
    cw = convw_ref[...]
    y = cw[0:1, :] * u_scr[6:6 + tm, :] + cw[1:2, :] * u_scr[7:7 + tm, :] + cw[2:3, :] * u
    oconv_ref[...] = (p[:, P_CB:P_CB + CONV_WIDTH] * y).astype(BF16)
    u_scr[0:8, :] = u_scr[tm:tm + 8, :]

    dq_ref[...] = (p[:, P_DQ:P_DQ + W_DQ] * (DSA_HEAD_DIM ** -0.5 * LOG2E)).astype(BF16)
    dkk_ref[...] = p[:, P_DKK:P_DKK + LANE].astype(BF16)
    hi_lanes = lax.broadcasted_iota(I32, (tm, LANE), 1) >= DSA_HEAD_DIM
    dvv_ref[...] = jnp.where(hi_lanes, 1.0, p[:, P_DVV:P_DVV + LANE]).astype(BF16)
    iq_ref[...] = p[:, P_IQ:P_IQ + W_IQ].astype(BF16)
    ikr_ref[...] = p[:, P_IKR:P_IKR + LANE].astype(BF16)
    iw_ref[...] = p[:, P_IW:P_IW + LANE] * ((IDX_DIM ** -0.5) * (IDX_HEADS ** -0.5))

    gq_ref[...] = p[:, P_GQ:P_GQ + LANE] * (GLA_DK ** -0.5)
    gk_ref[...] = p[:, P_GK:P_GK + LANE]
    gv_ref[...] = p[:, P_GV:P_GV + W_GV]
    gate = _dot(p[:, P_GLR:P_GLR + LANE].astype(BF16), wgate_ref[...]) + bgate_ref[...]
    log_sig = jnp.minimum(gate, 0.0) - jnp.log1p(jnp.exp(-jnp.abs(gate)))
    gg_ref[...] = log_sig * (1.0 / GLA_TAU)
    gr = p[:, P_GR:P_GR + W_GV]
    gr_ref[...] = gr / (1.0 + jnp.exp(-gr))


def _pack_small_weight(w_in):
    def cols(a, b):
        return w_in[..., a:b]

    def zeros(n):
        return jnp.zeros(w_in.shape[:-1] + (n,), w_in.dtype)

    half = MLA_ROPE // 2
    kr1, kr2 = cols(O_KR, O_KR + half), cols(O_KR + half, O_KR + MLA_ROPE)
    pad_tail = HEAD_PAD - MLA_NOPE - MLA_ROPE
    pieces = [
        cols(O_CQ, O_CKV), cols(O_CKV, O_KR),
        zeros(MLA_NOPE), kr1, kr2, zeros(pad_tail),
        zeros(MLA_NOPE), kr2, kr1, zeros(pad_tail),
        cols(O_CB, O_CC), cols(O_CC, O_CX), cols(O_CX, O_DQ),
        cols(O_DQ, O_DK), cols(O_DK, O_DV), cols(O_DK, O_DV), cols(O_DV, O_IQ), zeros(DSA_HEAD_DIM),
        cols(O_IQ, O_IK), cols(O_IK, O_IW), cols(O_IK, O_IW), cols(O_IK, O_IW), cols(O_IK, O_IW),
        cols(O_IW, O_GQ), zeros(LANE - IDX_HEADS),
        cols(O_GQ, O_GK), cols(O_GK, O_GV), cols(O_GV, O_GLR),
        cols(O_GLR, O_GR), zeros(LANE - GLA_GATE_RANK),
        cols(O_GR, O_GATE),
    ]
    out = jnp.concatenate(pieces, axis=-1)
    assert out.shape[-1] == P_TOTAL, out.shape
    return out.astype(BF16)


def _pack_mla_weights(w_q_up, w_kv_up):
    half = MLA_ROPE // 2
    lq = w_q_up.shape[:-1]
    lk = w_kv_up.shape[:-1]
    wq = w_q_up.reshape(lq + (MLA_HEADS, MLA_NOPE + MLA_ROPE))
    nope, x1, x2 = wq[..., :MLA_NOPE], wq[..., MLA_NOPE:MLA_NOPE + half], wq[..., MLA_NOPE + half:]
    zq = jnp.zeros(lq + (MLA_HEADS, HEAD_PAD - MLA_NOPE - MLA_ROPE), w_q_up.dtype)
    wq_pad = jnp.concatenate([nope, x1, x2, zq], axis=-1).reshape(lq + (MLA_HEADS * HEAD_PAD,))
    wq_swap = jnp.concatenate([jnp.zeros_like(nope), x2, x1, zq], axis=-1).reshape(lq + (MLA_HEADS * HEAD_PAD,))
    wkv = w_kv_up.reshape(lk + (MLA_HEADS, MLA_NOPE + MLA_V))
    zk = jnp.zeros(lk + (MLA_HEADS, HEAD_PAD - MLA_NOPE), w_kv_up.dtype)
    wk_pad = jnp.concatenate([wkv[..., :MLA_NOPE], zk], axis=-1).reshape(lk + (MLA_HEADS * HEAD_PAD,))
    wv = wkv[..., MLA_NOPE:].reshape(lk + (MLA_HEADS * MLA_V,))
    return wq_pad.astype(BF16), wq_swap.astype(BF16), wk_pad.astype(BF16), wv.astype(BF16)


def _prep(x2, nb, g, w_small, qg, wq, wqs, kvg, wk, wv, ctab, stab, conv_w, wgate, bgate, tm):
    t, d = x2.shape
    nj = t // nb // tm

    def tok(width):
        return pl.BlockSpec((tm, width), lambda b, j: (b * nj + j, 0))

    def full(a):
        return pl.BlockSpec(a.shape, lambda b, j: (0,) * a.ndim, pipeline_mode=pl.Buffered(1))

    outs = [(W_MLA, BF16), (W_MLA, BF16), (W_MLA, BF16), (CONV_WIDTH, BF16), (W_DQ, BF16), (LANE, BF16), (LANE, BF16),
            (W_IQ, BF16), (LANE, BF16), (LANE, F32), (LANE, F32), (LANE, F32), (W_GV, F32), (LANE, F32), (W_GV, F32)]
    return pl.pallas_call(
        functools.partial(_prep_kernel, tm=tm),
        grid=(nb, nj),
        in_specs=[tok(d), full(g), full(w_small), full(qg), full(wq), full(wqs), full(kvg), full(wk), full(wv),
                  tok(HEAD_PAD), tok(HEAD_PAD), full(conv_w), full(wgate), full(bgate)],
        out_specs=[tok(w) for w, _ in outs],
        out_shape=[jax.ShapeDtypeStruct((t, w), dt) for w, dt in outs],
        scratch_shapes=[pltpu.VMEM((tm + 8, CONV_WIDTH), F32)],
        compiler_params=_cparams(("arbitrary", "arbitrary")),
        name="prep",
    )(x2, g, w_small, qg, wq, wqs, kvg, wk, wv, ctab, stab, conv_w, wgate, bgate)


def _for_chunks(n, body):
    def group(i, carry):
        for u in range(4):
            body(4 * i + u, carry)
        return carry

    lax.fori_loop(0, n // 4, group, 0)
    rem = n % 4

    @pl.when(rem >= 2)
    def _():
        body(n - rem, 0)
        body(n - rem + 1, 0)

    @pl.when(rem % 2 == 1)
    def _():
        body(n - 1, 0)


def _fold_lanes(x, op):
    out = x[:, 0:LANE]
    for g in range(1, x.shape[1] // LANE):
        out = op(out, x[:, g * LANE:(g + 1) * LANE])
    return out


def _mla_kernel(q_ref, k_ref, v_ref, o_ref, s_scr, m_scr, acc_scr, *, tq, tk):
    i = pl.program_id(1)
    n_chunks = ((i + 1) * tq + tk - 1) // tk
    row = i * tq + lax.broadcasted_iota(I32, (tq, tk), 0)
    col = lax.broadcasted_iota(I32, (tq, tk), 1)
    lane = lax.broadcasted_iota(I32, (tq, LANE), 1)
    heads = range(MLA_HEADS)
    hs = [slice(h * HEAD_PAD, (h + 1) * HEAD_PAD) for h in heads]
    vs = [slice((h // 2) * 2 * LANE, (h // 2 + 1) * 2 * LANE) for h in heads]

    m_scr[...] = jnp.full(m_scr.shape, NEG, F32)
    acc_scr[...] = jnp.zeros(acc_scr.shape, F32)

    def phase1(c, carry):
        off = pl.multiple_of(c * tk, tk)
        causal = c * tk + col <= row
        for h in heads:
            s = jnp.where(causal, _dot_nt(q_ref[:, hs[h]], k_ref[pl.ds(off, tk), hs[h]]), NEG)
            s_scr[h, c] = s
            m_scr[h] = jnp.maximum(m_scr[h], _fold_lanes(s, jnp.maximum))
        return carry

    _for_chunks(n_chunks, phase1)
    for h in heads:
        m_scr[h] = jnp.broadcast_to(jnp.max(m_scr[h], axis=1, keepdims=True), (tq, LANE))

    def phase2(c, carry):
        off = pl.multiple_of(c * tk, tk)
        for h in heads:
            p = jnp.exp2(s_scr[h, c] - jnp.concatenate([m_scr[h]] * (tk // LANE), axis=1))
            acc_scr[h] = acc_scr[h] + _dot(p.astype(BF16), v_ref[pl.ds(off, tk), vs[h]])
        return carry

    _for_chunks(n_chunks, phase2)
    res = [acc_scr[h, :, 0:LANE] / acc_scr[h, :, LANE:2 * LANE] for h in heads]
    for pair in range(MLA_HEADS // 2):
        o_ref[:, pair * LANE:(pair + 1) * LANE] = jnp.where(lane < MLA_V, res[2 * pair], res[2 * pair + 1]).astype(BF16)


def _mla(mq, mk, mv, nb, tq, tk):
    t = mq.shape[0]
    s = t // nb
    nq = s // tq
    return pl.pallas_call(
        functools.partial(_mla_kernel, tq=tq, tk=tk),
        grid=(nb, nq),
        in_specs=[pl.BlockSpec((tq, MLA_HEADS * HEAD_PAD), lambda b, i: (b * nq + i, 0)),
                  pl.BlockSpec((s, MLA_HEADS * HEAD_PAD), lambda b, i: (b, 0)),
                  pl.BlockSpec((s, MLA_HEADS * HEAD_PAD), lambda b, i: (b, 0))],
        out_specs=pl.BlockSpec((tq, MLA_HEADS * MLA_V), lambda b, i: (b * nq + i, 0)),
        out_shape=jax.ShapeDtypeStruct((t, MLA_HEADS * MLA_V), BF16),
        scratch_shapes=[pltpu.VMEM((MLA_HEADS, s // tk, tq, tk), F32),
                        pltpu.VMEM((MLA_HEADS, tq, LANE), F32),
                        pltpu.VMEM((MLA_HEADS, tq, 2 * LANE), F32)],
        compiler_params=_cparams(("arbitrary", "arbitrary")),
        name="mla_attn",
    )(mq, mk, mv)


def _dsa_kernel(iq_ref, iw_ref, ikr_ref, dq_ref, dkk_ref, dvv_ref, tri_ref, o_ref,
                key_scr, plane_scr, eq_scr, gt_scr, thr_scr, s_scr, m_scr, acc_scr,
                *, tq, th, tk, topk):
    i = pl.program_id(1)
    q_end = (i + 1) * tq
    n_chunks = (q_end + tk - 1) // tk
    lane_q = lax.broadcasted_iota(I32, (th, LANE), 1)
    int_min = jnp.int32(-2 ** 31)
    rb = 64
    lane_rb = lax.broadcasted_iota(I32, (rb, tk), 1)
    row_rb = lax.broadcasted_iota(I32, (rb, 1), 0)

    for r0 in range(0, tq, th):
        iq = iq_ref[r0:r0 + th, :].astype(F32)
        iw = iw_ref[r0:r0 + th, :]
        q_heads = []
        w_heads = []
        for h in range(IDX_HEADS):
            grp = h // 4
            sub = iq[:, grp * LANE:(grp + 1) * LANE]
            lo = (h % 4) * IDX_DIM
            in_head = (lane_q >= lo) & (lane_q < lo + IDX_DIM)
            q_heads.append(jnp.where(in_head, sub, 0.0).astype(BF16))
            w_heads.append(iw[:, h:h + 1])
        q_stacks = [jnp.concatenate([qh[r:r + rb] for qh in q_heads], axis=0) for r in range(0, th, rb)]

        def score_chunk(c, _, r0=r0, q_stacks=q_stacks, w_heads=w_heads):
            off = pl.multiple_of(c * tk, tk)
            kc = ikr_ref[pl.ds(off, tk), :]
            for bi, r in enumerate(range(0, th, rb)):
                z = _dot_nt(q_stacks[bi], kc)
                acc = w_heads[0][r:r + rb] * jnp.maximum(z[0:rb], 0.0)
                for h in range(1, IDX_HEADS):
                    acc = acc + w_heads[h][r:r + rb] * jnp.maximum(z[h * rb:(h + 1) * rb], 0.0)
                bits = pltpu.bitcast(acc, I32)
                key = jnp.where(bits < 0, bits ^ jnp.int32(0x7FFFFFFF), bits)
                key = jnp.where(c * tk + lane_rb <= i * tq + r0 + r + row_rb, key, int_min)
                key_scr[c, r0 + r:r0 + r + rb, :] = key
            return 0

        _for_chunks(n_chunks, score_chunk)

    kf = jnp.float32(topk)
    n_groups = key_scr.shape[0] * (tk // LANE)
    cb = 128

    def row_total(cnt):
        return _dot(cnt.astype(F32).astype(BF16), jnp.ones((LANE, LANE), BF16))

    def select():
        def fill(c, _):
            key_scr[c] = jnp.full((tq, tk), int_min, I32)
            return 0
        few_groups = n_chunks * (tk // LANE) <= 16
        lax.fori_loop(n_chunks, jnp.where(few_groups, min(16 // (tk // LANE), key_scr.shape[0]), key_scr.shape[0]),
                      fill, 0)

        stage_mask = {16: 0x0000FFFF, 8: 0x00FF00FF, 4: 0x0F0F0F0F, 2: 0x33333333, 1: 0x55555555}

        def build_planes(blk, _, nw):
            r = pl.multiple_of(blk * 8, 8)
            w = []
            for g in range(nw):
                if g < n_groups:
                    gc, gl = divmod(g, tk // LANE)
                    w.append(key_scr[gc, pl.ds(r, 8), gl * LANE:(gl + 1) * LANE] ^ int_min)
                else:
                    w.append(jnp.zeros((8, LANE), I32))
            j = nw // 2
            while j:
                k = 0
                while k < nw:
                    t = (w[k] ^ lax.shift_right_logical(w[k + j], jnp.int32(j))) & jnp.int32(stage_mask[j])
                    w[k] = w[k] ^ t
                    w[k + j] = w[k + j] ^ lax.shift_left(t, jnp.int32(j))
                    k = (k + j + 1) & ~j
                j >>= 1
            if nw == 16:
                w = [lax.shift_right_logical(u, jnp.int32(16)) for u in w] + [u & jnp.int32(0xFFFF) for u in w]
            for b in range(32):
                plane_scr[b, pl.ds(r, 8), :] = w[b]
            return 0

        @pl.when(few_groups)
        def _():
            lax.fori_loop(0, tq // 8, functools.partial(build_planes, nw=16), 0)

        if n_groups > 16:
            @pl.when(jnp.logical_not(few_groups))
            def _():
                lax.fori_loop(0, tq // 8, functools.partial(build_planes, nw=32), 0)

        eq_scr[...] = jnp.full((tq, LANE), -1, I32)
        gt_scr[...] = jnp.zeros((tq, LANE), I32)
        thr_scr[...] = jnp.zeros((tq, LANE), I32)

        def bit_pass(it, _):
            bit = lax.shift_left(jnp.int32(1), 31 - it)
            for r in range(0, tq, cb):
                rows = slice(r, r + cb)
                eq, gtm = eq_scr[rows, :], gt_scr[rows, :]
                t = eq & plane_scr[it, rows, :]
                u = gtm | t
                take = jnp.sum(lax.population_count(u).astype(F32), axis=1, keepdims=True) >= kf
                eq_scr[rows, :] = jnp.where(take, t, eq ^ t)
                gt_scr[rows, :] = jnp.where(take, gtm, u)
                thr_scr[rows, :] = jnp.where(take, thr_scr[rows, :] | bit, thr_scr[rows, :])
            return 0
        lax.fori_loop(0, 32, bit_pass, 0)

        thr_u = thr_scr[:, 0:1]
        thr = jnp.maximum(thr_u ^ int_min, int_min + 1)
        c_gt = row_total(lax.population_count(gt_scr[...]))[:, 0:1]
        c_ge = c_gt + row_total(lax.population_count(eq_scr[...]))[:, 0:1]
        excess = (c_ge > kf) & (thr_u != 0)

        need = kf - c_gt

        def tie_break(r):
            rows = slice(r, r + cb)
            thr_b, need_b = thr[rows], need[rows]

            def demote(c, seen):
                kk = key_scr[c, rows, :]
                tie = kk == thr_b
                rank = seen + _dot(jnp.where(tie, 1.0, 0.0).astype(BF16), tri_ref[...])
                key_scr[c, rows, :] = jnp.where(tie & (rank > need_b), thr_b - 1, kk)
                return rank[:, tk - 1:tk]
            lax.fori_loop(0, n_chunks, demote, jnp.zeros((cb, 1), F32))

        for r in range(0, tq, cb):
            pl.when(jnp.max(jnp.where(excess[r:r + cb], 1.0, 0.0)) > 0.0)(functools.partial(tie_break, r))
        return thr

    thr = lax.cond(q_end > topk, select, lambda: jnp.full((tq, 1), int_min + 1, I32))

    heads = range(DSA_HEADS)
    for r0 in range(0, tq, th):
        rows = slice(r0, r0 + th)
        thr_h = thr[rows]
        dq = dq_ref[rows, :].astype(F32)
        qm = []
        for h in heads:
            sub = dq[:, (h // 2) * LANE:(h // 2 + 1) * LANE]
            lo = (h % 2) * DSA_HEAD_DIM
            in_head = (lane_q >= lo) & (lane_q < lo + DSA_HEAD_DIM)
            qm.append(jnp.where(in_head, sub, 0.0).astype(BF16))
        qm_stack = jnp.concatenate(qm, axis=0)

        m_scr[...] = jnp.full(m_scr.shape, NEG, F32)
        acc_scr[...] = jnp.zeros(acc_scr.shape, F32)

        def phase1(c, carry, rows=rows, thr_h=thr_h, qm_stack=qm_stack):
            off = pl.multiple_of(c * tk, tk)
            sel = key_scr[c, rows, :] >= thr_h
            s = _dot_nt(qm_stack, dkk_ref[pl.ds(off, tk), :])
            for h in heads:
                sm = jnp.where(sel, s[h * th:(h + 1) * th], NEG)
                s_scr[h, c] = sm
                m_scr[h] = jnp.maximum(m_scr[h], _fold_lanes(sm, jnp.maximum))
            return carry

        _for_chunks(n_chunks, phase1)
        for h in heads:
            m_scr[h] = jnp.broadcast_to(jnp.max(m_scr[h], axis=1, keepdims=True), (th, LANE))

        def phase2(c, carry):
            off = pl.multiple_of(c * tk, tk)
            ps = []
            for h in heads:
                p = jnp.exp2(s_scr[h, c] - jnp.concatenate([m_scr[h]] * (tk // LANE), axis=1))
                ps.append(p.astype(BF16))
            pv = _dot(jnp.concatenate(ps, axis=0), dvv_ref[pl.ds(off, tk), :])
            for h in heads:
                acc_scr[h] = acc_scr[h] + pv[h * th:(h + 1) * th]
            return carry

        _for_chunks(n_chunks, phase2)
        outs = [acc_scr[h] / pltpu.roll(acc_scr[h], DSA_HEAD_DIM, 1) for h in heads]
        lo_half = lane_q < DSA_HEAD_DIM
        o_ref[rows, :] = jnp.concatenate(
            [jnp.where(lo_half, outs[0], pltpu.roll(outs[1], DSA_HEAD_DIM, 1)),
             jnp.where(lo_half, outs[2], pltpu.roll(outs[3], DSA_HEAD_DIM, 1))], axis=1).astype(BF16)


def _dsa(iq, iw, ikr, dq, dkk, dvv, nb, tq, th, tk):
    t = iq.shape[0]
    s = t // nb
    nq = s // tq
    topk = min(DSA_TOPK_MAX, s // 4)
    assert s // LANE <= 32, "the bit-sliced selection packs one key per 128-lane group into a 32-bit word"
    tri = jnp.asarray(np.triu(np.ones((tk, tk), np.float32)), BF16)

    def qspec(w):
        return pl.BlockSpec((tq, w), lambda b, i: (b * nq + i, 0))

    def kspec(w):
        return pl.BlockSpec((s, w), lambda b, i: (b, 0))

    return pl.pallas_call(
        functools.partial(_dsa_kernel, tq=tq, th=th, tk=tk, topk=topk),
        grid=(nb, nq),
        in_specs=[qspec(W_IQ), qspec(LANE), kspec(LANE), qspec(W_DQ), kspec(LANE), kspec(LANE),
                  pl.BlockSpec((tk, tk), lambda b, i: (0, 0), pipeline_mode=pl.Buffered(1))],
        out_specs=qspec(W_DQ),
        out_shape=jax.ShapeDtypeStruct((t, DSA_HEADS * DSA_HEAD_DIM), BF16),
        scratch_shapes=[pltpu.VMEM((s // tk, tq, tk), I32),
                        pltpu.VMEM((32, tq, LANE), I32),
                        pltpu.VMEM((tq, LANE), I32),
                        pltpu.VMEM((tq, LANE), I32),
                        pltpu.VMEM((tq, LANE), I32),
                        pltpu.VMEM((DSA_HEADS, s // tk, th, tk), F32)] + [pltpu.VMEM((DSA_HEADS, th, LANE), F32)] * 2,
        compiler_params=_cparams(("arbitrary", "arbitrary")),
        name="dsa",
    )(iq, iw, ikr, dq, dkk, dvv, tri)


def _gla_kernel(gq_ref, gk_ref, gv_ref, gg_ref, gr_ref, ng_ref, segv_ref, seg64_ref, bmask_ref, o_ref,
                st_scr, kbuf, cbuf, vbuf, *, tg):
    j = pl.program_id(1)
    ch = GLA_CHUNK
    rowc = lax.broadcasted_iota(I32, (tg, 1), 0) % ch

    @pl.when(j == 0)
    def _():
        st_scr[...] = jnp.zeros_like(st_scr)
        kbuf[0:ch, :] = jnp.zeros((ch, LANE), F32)
        cbuf[0:ch, :] = jnp.zeros((ch, LANE), F32)
        vbuf[0:ch, :] = jnp.zeros((ch, 2 * LANE), F32)

    cbuf[ch:ch + tg, :] = gg_ref[...]
    sh = 1
    while sh < ch:
        cur = cbuf[ch:ch + tg, :]
        prev = cbuf[ch - sh:ch - sh + tg, :]
        cbuf[ch:ch + tg, :] = cur + jnp.where(rowc >= sh, prev, 0.0)
        sh *= 2
    cum = cbuf[ch:ch + tg, :]
    q = gq_ref[...]
    k = gk_ref[...]
    v = gv_ref[...]
    kbuf[ch:ch + tg, :] = k
    vbuf[ch:ch + tg, :] = v

    qd = (q * jnp.exp(cum)).astype(BF16)
    bmask = bmask_ref[...]
    inter = []
    for c in range(tg // ch):
        rs = slice(c * ch, (c + 1) * ch)
        st = st_scr[...]
        inter.append(_dot_nt(qd[rs], st.astype(BF16)))
        last = cum[c * ch + ch - 1:c * ch + ch, :]
        kp = k[rs] * jnp.exp(last - cum[rs])
        upd = lax.dot_general(v[rs].astype(BF16), kp.astype(BF16), (((0,), (0,)), ((), ())),
                              preferred_element_type=F32)
        st_scr[...] = jnp.exp(last) * st + upd * bmask
    o = jnp.concatenate(inter, axis=0)

    segv = segv_ref[...]
    cum2 = cum * LOG2E
    cbuf[ch:ch + tg, :] = cum2
    sub = 8
    rows_sub = rowc % sub
    for d in range(sub):
        kd = kbuf[ch - d:ch - d + tg, :]
        cd = cbuf[ch - d:ch - d + tg, :]
        vd = vbuf[ch - d:ch - d + tg, :]
        e = jnp.exp2(jnp.where(rows_sub >= d, cum2 - cd, NEG))
        o = o + _dot((q * kd * e).astype(BF16), segv) * vd

    lane_k3 = lax.broadcasted_iota(I32, (1, (ch // sub - 1) * LANE), 1) % LANE // GLA_DK
    lane_v = lax.broadcasted_iota(I32, (1, 2 * LANE), 1) // GLA_DV
    far = []
    for c in range(tg // ch):
        rs = slice(c * ch, (c + 1) * ch)
        qc, kc, cc, sb = q[rs], k[rs], cum2[rs], rowc[rs] // sub
        qh, kh = [], []
        for blk in range(1, ch // sub):
            ref_row = cc[blk * sub - 1:blk * sub, :]
            qh.append(qc * jnp.exp2(jnp.where(sb == blk, cc - ref_row, NEG)))
            kh.append(kc * jnp.exp2(jnp.where(sb < blk, ref_row - cc, NEG)))
        qh = jnp.concatenate(qh, axis=1)
        kh = jnp.concatenate(kh, axis=1).astype(BF16)
        q_heads = jnp.concatenate([jnp.where(lane_k3 == h, qh, 0.0) for h in range(GLA_HEADS)], axis=0)
        att = _dot_nt(q_heads.astype(BF16), kh)
        ov = _dot(att.astype(BF16), v[rs].astype(BF16))
        oc = jnp.where(lane_v == 0, ov[0:ch], 0.0)
        for h in range(1, GLA_HEADS):
            oc = oc + jnp.where(lane_v == h, ov[h * ch:(h + 1) * ch], 0.0)
        far.append(oc)
    o = o + jnp.concatenate(far, axis=0)

    sq = o * o
    hi = sq.astype(BF16)
    lo = (sq - hi.astype(F32)).astype(BF16)
    var = _dot(hi, seg64_ref[...]) + _dot(lo, seg64_ref[...])
    o_ref[...] = (gr_ref[...] * (o * lax.rsqrt(var + EPS) * ng_ref[...])).astype(BF16)


def _gla(gq, gk, gv, gg, gr, norm_g, nb, tg):
    t = gq.shape[0]
    nj = t // nb // tg
    ng = jnp.tile(norm_g.reshape(1, GLA_DV), (1, GLA_HEADS)).astype(F32)
    hk = np.arange(GLA_HEADS * GLA_DK) // GLA_DK
    hv = np.arange(GLA_HEADS * GLA_DV) // GLA_DV
    same = (hk[:, None] == hv[None, :])
    segv = jnp.asarray(same, BF16)
    bmask = jnp.asarray(same.T, F32)
    seg64 = jnp.asarray((hv[:, None] == hv[None, :]) / GLA_DV, BF16)

    def tok(w):
        return pl.BlockSpec((tg, w), lambda b, j: (b * nj + j, 0))

    def full(a):
        return pl.BlockSpec(a.shape, lambda b, j: (0,) * a.ndim, pipeline_mode=pl.Buffered(1))

    return pl.pallas_call(
        functools.partial(_gla_kernel, tg=tg),
        grid=(nb, nj),
        in_specs=[tok(LANE), tok(LANE), tok(W_GV), tok(LANE), tok(W_GV), full(ng), full(segv), full(seg64),
                  full(bmask)],
        out_specs=tok(W_GV),
        out_shape=jax.ShapeDtypeStruct((t, GLA_HEADS * GLA_DV), BF16),
        scratch_shapes=[pltpu.VMEM((2 * LANE, LANE), F32),
                        pltpu.VMEM((GLA_CHUNK + tg, LANE), F32),
                        pltpu.VMEM((GLA_CHUNK + tg, LANE), F32),
                        pltpu.VMEM((GLA_CHUNK + tg, 2 * LANE), F32)],
        compiler_params=_cparams(("arbitrary", "arbitrary")),
        name="gla",
    )(gq, gk, gv, gg, gr, ng, segv, seg64, bmask)


def _merge_kernel(x_ref, g_ref, wg_ref, o0_ref, o1_ref, o2_ref, o3_ref, wb_ref, wo_ref, out_ref):
    x = x_ref[...]
    h = _rms(x, g_ref[...]).astype(BF16)
    merged = None
    for n, o_ref in enumerate((o0_ref, o1_ref, o2_ref, o3_ref)):
        gate = _dot(h, wg_ref[:, n * D_MODEL:(n + 1) * D_MODEL])
        gate = 1.0 / (1.0 + jnp.exp(-gate))
        term = gate * _dot(o_ref[...], wb_ref[n])
        merged = term if merged is None else merged + term
    out_ref[...] = x + _dot(merged.astype(BF16), wo_ref[...])


def _merge(x2, g, wg, branches, wb, wo, tm):
    t, d = x2.shape

    def tok(w):
        return pl.BlockSpec((tm, w), lambda i: (i, 0))

    def full(a):
        return pl.BlockSpec(a.shape, lambda i: (0,) * a.ndim, pipeline_mode=pl.Buffered(1))

    return pl.pallas_call(
        _merge_kernel,
        grid=(t // tm,),
        in_specs=[tok(d), full(g), full(wg)] + [tok(BRANCH_WIDTH)] * N_BRANCH + [full(wb), full(wo)],
        out_specs=tok(d),
        out_shape=jax.ShapeDtypeStruct((t, d), F32),
        compiler_params=_cparams(("arbitrary",)),
        name="merge",
    )(x2, g, wg, *branches, wb, wo)


def _ffn_kernel(x_ref, g_ref, wg_ref, wu_ref, wd_ref, fg_ref, out_ref, *, fc, final):
    x = x_ref[...]
    h = _rms(x, g_ref[...]).astype(BF16)
    acc = x
    for c in range(D_FF // fc):
        gate = _dot(h, wg_ref[:, c * fc:(c + 1) * fc])
        up = _dot(h, wu_ref[:, c * fc:(c + 1) * fc])
        act = (gate / (1.0 + jnp.exp(-gate)) * up).astype(BF16)
        acc = acc + _dot(act, wd_ref[c * fc:(c + 1) * fc, :])
    if final:
        acc = _rms(acc, fg_ref[...])
    out_ref[...] = acc


def _ffn(x2, g, wg, wu, wd, fg, tm, final):
    t, d = x2.shape

    def tok(w):
        return pl.BlockSpec((tm, w), lambda i: (i, 0))

    def full(a):
        return pl.BlockSpec(a.shape, lambda i: (0,) * a.ndim, pipeline_mode=pl.Buffered(1))

    return pl.pallas_call(
        functools.partial(_ffn_kernel, fc=MXU_WIDTH, final=final),
        grid=(t // tm,),
        in_specs=[tok(d), full(g), full(wg), full(wu), full(wd), full(fg)],
        out_specs=tok(d),
        out_shape=jax.ShapeDtypeStruct((t, d), F32),
        compiler_params=_cparams(("arbitrary",)),
        name="ffn",
    )(x2, g, wg, wu, wd, fg)


def _tile(n, pref):
    while n % pref:
        pref //= 2
    return pref


def kernel(x, positions, attn_norm_g, w_in, mla_q_norm_g, mla_w_q_up, mla_kv_norm_g, mla_w_kv_up, conv_w, gla_w_gate_up, gla_b_gate, gla_norm_g, w_branch, w_out, ffn_norm_g, w_ffn_gate, w_ffn_up, w_ffn_down, final_norm_g):
    b, s, d = x.shape
    depth = w_in.shape[0]
    t = b * s
    assert d == D_MODEL and s % GLA_CHUNK == 0 and conv_w.shape[1] == CONV_K and w_in.shape[2] == O_END
    tm_prep = _tile(s, 512)
    tm_wide = _tile(t, 1024)
    tq_mla = _tile(s, 256)
    tk_att = _tile(s, 512)
    tq_dsa = _tile(s, 512)
    th_dsa = _tile(tq_dsa, 256)
    tg = _tile(s, 512)

    x2 = x.reshape(t, d)
    ctab, stab = _rope_tables(positions, _tile(t, 2048))
    row = lambda a: a.reshape(1, -1).astype(F32)

    w_small = _pack_small_weight(w_in)
    wq, wqs, wk, wv = _pack_mla_weights(mla_w_q_up, mla_w_kv_up)
    wgate = jnp.concatenate([gla_w_gate_up, jnp.zeros((depth, LANE - GLA_GATE_RANK, GLA_HEADS * GLA_DK), F32)],
                            axis=1).astype(BF16)
    w_gates = w_in[:, :, O_GATE:].astype(BF16)
    w_branch_b, w_out_b = w_branch.astype(BF16), w_out.astype(BF16)
    w_gate_b, w_up_b = w_ffn_gate.astype(BF16), w_ffn_up.astype(BF16)
    w_down_b = w_ffn_down.astype(BF16)

    for l in range(depth):
        (mq, mk, mv, o_conv, dq, dkk, dvv, iq, ikr, iw, gq, gk, gv, gg, gr) = _prep(
            x2, b, row(attn_norm_g[l]), w_small[l], row(mla_q_norm_g[l]), wq[l], wqs[l], row(mla_kv_norm_g[l]),
            wk[l], wv[l], ctab, stab, conv_w[l].astype(F32), wgate[l], row(gla_b_gate[l]), tm_prep)
        o_mla = _mla(mq, mk, mv, b, tq_mla, tk_att)
        o_dsa = _dsa(iq, iw, ikr, dq, dkk, dvv, b, tq_dsa, th_dsa, tk_att)
        o_gla = _gla(gq, gk, gv, gg, gr, gla_norm_g[l], b, tg)
        x2 = _merge(x2, row(attn_norm_g[l]), w_gates[l], (o_mla, o_conv, o_dsa, o_gla), w_branch_b[l], w_out_b[l],
                    tm_wide)
        x2 = _ffn(x2, row(ffn_norm_g[l]), w_gate_b[l], w_up_b[l], w_down_b[l], row(final_norm_g), tm_wide,
                  final=(l == depth - 1))
    return x2.reshape(b, s, d)
```

```python
import functools

import jax
import jax.numpy as jnp
import numpy as np
from jax import lax
from jax.experimental import pallas as pl
from jax.experimental.pallas import tpu as pltpu

F32 = jnp.float32
BF16 = jnp.bfloat16
I32 = jnp.int32

D_MODEL = 1024
MLA_HEADS = 4
MLA_Q_RANK = 256
MLA_KV_RANK = 128
MLA_NOPE = 64
MLA_ROPE = 32
MLA_V = 64
ROPE_THETA = 10000.0
CONV_WIDTH = 256
CONV_K = 3
DSA_HEADS = 4
DSA_HEAD_DIM = 64
IDX_HEADS = 8
IDX_DIM = 32
DSA_TOPK_MAX = 256
GLA_HEADS = 4
GLA_DK = 32
GLA_DV = 64
GLA_GATE_RANK = 16
GLA_TAU = 16.0
GLA_CHUNK = 64
N_BRANCH = 4
BRANCH_WIDTH = 256
D_FF = ((8 * D_MODEL + 3 * 256 - 1) // (3 * 256)) * 256
EPS = 1e-6
NEG = -1e30
LOG2E = 1.4426950408889634

LANE = 128
MXU_WIDTH = 256
HEAD_PAD = 128
W_MLA = MLA_HEADS * HEAD_PAD
W_DQ = DSA_HEADS * DSA_HEAD_DIM
W_IQ = IDX_HEADS * IDX_DIM
W_GV = GLA_HEADS * GLA_DV

IN_SIZES = (
    MLA_Q_RANK, MLA_KV_RANK, MLA_ROPE,
    CONV_WIDTH, CONV_WIDTH, CONV_WIDTH,
    DSA_HEADS * DSA_HEAD_DIM, DSA_HEAD_DIM, DSA_HEAD_DIM,
    IDX_HEADS * IDX_DIM, IDX_DIM, IDX_HEADS,
    GLA_HEADS * GLA_DK, GLA_HEADS * GLA_DK, GLA_HEADS * GLA_DV,
    GLA_GATE_RANK, GLA_HEADS * GLA_DV,
    N_BRANCH * D_MODEL,
)
_OFF = np.concatenate([[0], np.cumsum(IN_SIZES)]).astype(int)
(O_CQ, O_CKV, O_KR, O_CB, O_CC, O_CX, O_DQ, O_DK, O_DV, O_IQ, O_IK, O_IW,
 O_GQ, O_GK, O_GV, O_GLR, O_GR, O_GATE, O_END) = [int(v) for v in _OFF]

P_CQ, P_CKV, P_KRP, P_KRS = 0, 256, 384, 512
P_CB, P_CC, P_CX = 640, 896, 1152
P_DQ, P_DKK, P_DVV = 1408, 1664, 1792
P_IQ, P_IKR, P_IW = 1920, 2176, 2304
P_GQ, P_GK, P_GV, P_GLR, P_GR = 2432, 2560, 2688, 2944, 3072
P_TOTAL = 3328

VMEM_LIMIT = 56 * 1024 * 1024


def _cparams(sem):
    return pltpu.CompilerParams(dimension_semantics=sem, vmem_limit_bytes=VMEM_LIMIT)


def _rms(x, g):
    return x * lax.rsqrt(jnp.mean(x * x, axis=-1, keepdims=True) + EPS) * g


def _dot(a, b):
    return jnp.dot(a, b, preferred_element_type=F32)


def _dot_nt(a, b):
    return lax.dot_general(a, b, (((1,), (1,)), ((), ())), preferred_element_type=F32)


def _rope_kernel(pos_ref, pat_ref, c_ref, s_ref):
    pos = pos_ref[...].astype(F32)
    ang = pos * pat_ref[0:1, :]
    c_ref[...] = pat_ref[1:2, :] + pat_ref[2:3, :] * jnp.cos(ang)
    s_ref[...] = pat_ref[3:4, :] * jnp.sin(ang)


def _rope_tables(positions, tm):
    t = positions.size
    half = MLA_ROPE // 2
    inv_freq = ROPE_THETA ** (-jnp.arange(half, dtype=F32) / half)
    z64, z32 = jnp.zeros((MLA_NOPE,), F32), jnp.zeros((HEAD_PAD - MLA_NOPE - MLA_ROPE,), F32)
    o16 = jnp.ones((half,), F32)
    pat = jnp.stack([
        jnp.concatenate([z64, inv_freq, inv_freq, z32]),
        jnp.concatenate([z64 + 1.0, 0 * o16, 0 * o16, z32]),
        jnp.concatenate([z64, o16, o16, z32]),
        jnp.concatenate([z64, -o16, o16, z32]),
    ])
    pat = jnp.concatenate([pat, jnp.zeros((4, HEAD_PAD), F32)], axis=0)
    return pl.pallas_call(
        _rope_kernel,
        grid=(t // tm,),
        in_specs=[pl.BlockSpec((tm, 1), lambda i: (i, 0)), pl.BlockSpec((8, HEAD_PAD), lambda i: (0, 0))],
        out_specs=[pl.BlockSpec((tm, HEAD_PAD), lambda i: (i, 0))] * 2,
        out_shape=[jax.ShapeDtypeStruct((t, HEAD_PAD), F32)] * 2,
        compiler_params=_cparams(("arbitrary",)),
        name="rope_tables",
    )(positions.reshape(t, 1), pat)


def _prep_kernel(x_ref, g_ref, w_ref, qg_ref, wq_ref, wqs_ref, kvg_ref, wk_ref, wv_ref, c_ref, s_ref,
                 convw_ref, wgate_ref, bgate_ref,
                 mq_ref, mk_ref, mv_ref, oconv_ref, dq_ref, dkk_ref, dvv_ref, iq_ref, ikr_ref, iw_ref,
                 gq_ref, gk_ref, gv_ref, gg_ref, gr_ref, u_scr, *, tm):
    j = pl.program_id(1)
    h = _rms(x_ref[...], g_ref[...]).astype(BF16)
    p = _dot(h, w_ref[...])

    c1, s1 = c_ref[...], s_ref[...]
    c4 = jnp.concatenate([c1] * MLA_HEADS, axis=1)
    s4 = jnp.concatenate([s1] * MLA_HEADS, axis=1)

    qn = _rms(p[:, P_CQ:P_CQ + MLA_Q_RANK], qg_ref[...]).astype(BF16)
    q = _dot(qn, wq_ref[...]) * c4 + _dot(qn, wqs_ref[...]) * s4
    mq_ref[...] = (q * ((MLA_NOPE + MLA_ROPE) ** -0.5 * LOG2E)).astype(BF16)
    kvn = _rms(p[:, P_CKV:P_CKV + MLA_KV_RANK], kvg_ref[...]).astype(BF16)
    kr = p[:, P_KRP:P_KRP + HEAD_PAD] * c1 + p[:, P_KRS:P_KRS + HEAD_PAD] * s1
    mk_ref[...] = (_dot(kvn, wk_ref[...]) + jnp.concatenate([kr] * MLA_HEADS, axis=1)).astype(BF16)
    vv = _dot(kvn, wv_ref[...])
    ones = jnp.ones((tm, LANE), F32)
    mv_ref[...] = jnp.concatenate([vv[:, 0:LANE], ones, vv[:, LANE:2 * LANE], ones], axis=1).astype(BF16)

    u = p[:, P_CC:P_CC + CONV_WIDTH] * p[:, P_CX:P_CX + CONV_WIDTH]

    @pl.when(j == 0)
    def _():
        u_scr[0:8, :] = jnp.zeros((8, CONV_WIDTH), F32)

    u_scr[8:8 + tm, :] = u
    cw = convw_ref[...]
    y = cw[0:1, :] * u_scr[6:6 + tm, :] + cw[1:2, :] * u_scr[7:7 + tm, :] + cw[2:3, :] * u
    oconv_ref[...] = (p[:, P_CB:P_CB + CONV_WIDTH] * y).astype(BF16)
    u_scr[0:8, :] = u_scr[tm:tm + 8, :]

    dq_ref[...] = (p[:, P_DQ:P_DQ + W_DQ] * (DSA_HEAD_DIM ** -0.5 * LOG2E)).astype(BF16)
    dkk_ref[...] = p[:, P_DKK:P_DKK + LANE].astype(BF16)
    hi_lanes = lax.broadcasted_iota(I32, (tm, LANE), 1) >= DSA_HEAD_DIM
    dvv_ref[...] = jnp.where(hi_lanes, 1.0, p[:, P_DVV:P_DVV + LANE]).astype(BF16)
    iq_ref[...] = p[:, P_IQ:P_IQ + W_IQ].astype(BF16)
    ikr_ref[...] = p[:, P_IKR:P_IKR + LANE].astype(BF16)
    iw_ref[...] = p[:, P_IW:P_IW + LANE] * ((IDX_DIM ** -0.5) * (IDX_HEADS ** -0.5))

    gq_ref[...] = p[:, P_GQ:P_GQ + LANE] * (GLA_DK ** -0.5)
    gk_ref[...] = p[:, P_GK:P_GK + LANE]
    gv_ref[...] = p[:, P_GV:P_GV + W_GV]
    gate = _dot(p[:, P_GLR:P_GLR + LANE].astype(BF16), wgate_ref[...]) + bgate_ref[...]
    log_sig = jnp.minimum(gate, 0.0) - jnp.log1p(jnp.exp(-jnp.abs(gate)))
    gg_ref[...] = log_sig * (1.0 / GLA_TAU)
    gr = p[:, P_GR:P_GR + W_GV]
    gr_ref[...] = gr / (1.0 + jnp.exp(-gr))


def _pack_small_weight(w_in):
    def cols(a, b):
        return w_in[..., a:b]

    def zeros(n):
        return jnp.zeros(w_in.shape[:-1] + (n,), w_in.dtype)

    half = MLA_ROPE // 2
    kr1, kr2 = cols(O_KR, O_KR + half), cols(O_KR + half, O_KR + MLA_ROPE)
    pad_tail = HEAD_PAD - MLA_NOPE - MLA_ROPE
    pieces = [
        cols(O_CQ, O_CKV), cols(O_CKV, O_KR),
        zeros(MLA_NOPE), kr1, kr2, zeros(pad_tail),
        zeros(MLA_NOPE), kr2, kr1, zeros(pad_tail),
        cols(O_CB, O_CC), cols(O_CC, O_CX), cols(O_CX, O_DQ),
        cols(O_DQ, O_DK), cols(O_DK, O_DV), cols(O_DK, O_DV), cols(O_DV, O_IQ), zeros(DSA_HEAD_DIM),
        cols(O_IQ, O_IK), cols(O_IK, O_IW), cols(O_IK, O_IW), cols(O_IK, O_IW), cols(O_IK, O_IW),
        cols(O_IW, O_GQ), zeros(LANE - IDX_HEADS),
        cols(O_GQ, O_GK), cols(O_GK, O_GV), cols(O_GV, O_GLR),
        cols(O_GLR, O_GR), zeros(LANE - GLA_GATE_RANK),
        cols(O_GR, O_GATE),
    ]
    out = jnp.concatenate(pieces, axis=-1)
    assert out.shape[-1] == P_TOTAL, out.shape
    return out.astype(BF16)


def _pack_mla_weights(w_q_up, w_kv_up):
    half = MLA_ROPE // 2
    lq = w_q_up.shape[:-1]
    lk = w_kv_up.shape[:-1]
    wq = w_q_up.reshape(lq + (MLA_HEADS, MLA_NOPE + MLA_ROPE))
    nope, x1, x2 = wq[..., :MLA_NOPE], wq[..., MLA_NOPE:MLA_NOPE + half], wq[..., MLA_NOPE + half:]
    zq = jnp.zeros(lq + (MLA_HEADS, HEAD_PAD - MLA_NOPE - MLA_ROPE), w_q_up.dtype)
    wq_pad = jnp.concatenate([nope, x1, x2, zq], axis=-1).reshape(lq + (MLA_HEADS * HEAD_PAD,))
    wq_swap = jnp.concatenate([jnp.zeros_like(nope), x2, x1, zq], axis=-1).reshape(lq + (MLA_HEADS * HEAD_PAD,))
    wkv = w_kv_up.reshape(lk + (MLA_HEADS, MLA_NOPE + MLA_V))
    zk = jnp.zeros(lk + (MLA_HEADS, HEAD_PAD - MLA_NOPE), w_kv_up.dtype)
    wk_pad = jnp.concatenate([wkv[..., :MLA_NOPE], zk], axis=-1).reshape(lk + (MLA_HEADS * HEAD_PAD,))
    wv = wkv[..., MLA_NOPE:].reshape(lk + (MLA_HEADS * MLA_V,))
    return wq_pad.astype(BF16), wq_swap.astype(BF16), wk_pad.astype(BF16), wv.astype(BF16)


def _prep(x2, nb, g, w_small, qg, wq, wqs, kvg, wk, wv, ctab, stab, conv_w, wgate, bgate, tm):
    t, d = x2.shape
    nj = t // nb // tm

    def tok(width):
        return pl.BlockSpec((tm, width), lambda b, j: (b * nj + j, 0))

    def full(a):
        return pl.BlockSpec(a.shape, lambda b, j: (0,) * a.ndim, pipeline_mode=pl.Buffered(1))

    outs = [(W_MLA, BF16), (W_MLA, BF16), (W_MLA, BF16), (CONV_WIDTH, BF16), (W_DQ, BF16), (LANE, BF16), (LANE, BF16),
            (W_IQ, BF16), (LANE, BF16), (LANE, F32), (LANE, F32), (LANE, F32), (W_GV, F32), (LANE, F32), (W_GV, F32)]
    return pl.pallas_call(
        functools.partial(_prep_kernel, tm=tm),
        grid=(nb, nj),
        in_specs=[tok(d), full(g), full(w_small), full(qg), full(wq), full(wqs), full(kvg), full(wk), full(wv),
                  tok(HEAD_PAD), tok(HEAD_PAD), full(conv_w), full(wgate), full(bgate)],
        out_specs=[tok(w) for w, _ in outs],
        out_shape=[jax.ShapeDtypeStruct((t, w), dt) for w, dt in outs],
        scratch_shapes=[pltpu.VMEM((tm + 8, CONV_WIDTH), F32)],
        compiler_params=_cparams(("arbitrary", "arbitrary")),
        name="prep",
    )(x2, g, w_small, qg, wq, wqs, kvg, wk, wv, ctab, stab, conv_w, wgate, bgate)


def _for_chunks(n, body):
    def group(i, carry):
        for u in range(4):
            body(4 * i + u, carry)
        return carry

    lax.fori_loop(0, n // 4, group, 0)
    rem = n % 4

    @pl.when(rem >= 2)
    def _():
        body(n - rem, 0)
        body(n - rem + 1, 0)

    @pl.when(rem % 2 == 1)
    def _():
        body(n - 1, 0)


def _fold_lanes(x, op):
    out = x[:, 0:LANE]
    for g in range(1, x.shape[1] // LANE):
        out = op(out, x[:, g * LANE:(g + 1) * LANE])
    return out


def _mla_kernel(q_ref, k_ref, v_ref, o_ref, s_scr, m_scr, acc_scr, *, tq, tk):
    i = pl.program_id(1)
    n_chunks = ((i + 1) * tq + tk - 1) // tk
    row = i * tq + lax.broadcasted_iota(I32, (tq, tk), 0)
    col = lax.broadcasted_iota(I32, (tq, tk), 1)
    lane = lax.broadcasted_iota(I32, (tq, LANE), 1)
    heads = range(MLA_HEADS)
    hs = [slice(h * HEAD_PAD, (h + 1) * HEAD_PAD) for h in heads]
    vs = [slice((h // 2) * 2 * LANE, (h // 2 + 1) * 2 * LANE) for h in heads]

    m_scr[...] = jnp.full(m_scr.shape, NEG, F32)
    acc_scr[...] = jnp.zeros(acc_scr.shape, F32)

    def phase1(c, carry):
        off = pl.multiple_of(c * tk, tk)
        causal = c * tk + col <= row
        for h in heads:
            s = jnp.where(causal, _dot_nt(q_ref[:, hs[h]], k_ref[pl.ds(off, tk), hs[h]]), NEG)
            s_scr[h, c] = s
            m_scr[h] = jnp.maximum(m_scr[h], _fold_lanes(s, jnp.maximum))
        return carry

    _for_chunks(n_chunks, phase1)
    for h in heads:
        m_scr[h] = jnp.broadcast_to(jnp.max(m_scr[h], axis=1, keepdims=True), (tq, LANE))

    def phase2(c, carry):
        off = pl.multiple_of(c * tk, tk)
        for h in heads:
            p = jnp.exp2(s_scr[h, c] - jnp.concatenate([m_scr[h]] * (tk // LANE), axis=1))
            acc_scr[h] = acc_scr[h] + _dot(p.astype(BF16), v_ref[pl.ds(off, tk), vs[h]])
        return carry

    _for_chunks(n_chunks, phase2)
    res = [acc_scr[h, :, 0:LANE] / acc_scr[h, :, LANE:2 * LANE] for h in heads]
    for pair in range(MLA_HEADS // 2):
        o_ref[:, pair * LANE:(pair + 1) * LANE] = jnp.where(lane < MLA_V, res[2 * pair], res[2 * pair + 1]).astype(BF16)


def _mla(mq, mk, mv, nb, tq, tk):
    t = mq.shape[0]
    s = t // nb
    nq = s // tq
    return pl.pallas_call(
        functools.partial(_mla_kernel, tq=tq, tk=tk),
        grid=(nb, nq),
        in_specs=[pl.BlockSpec((tq, MLA_HEADS * HEAD_PAD), lambda b, i: (b * nq + i, 0)),
                  pl.BlockSpec((s, MLA_HEADS * HEAD_PAD), lambda b, i: (b, 0)),
                  pl.BlockSpec((s, MLA_HEADS * HEAD_PAD), lambda b, i: (b, 0))],
        out_specs=pl.BlockSpec((tq, MLA_HEADS * MLA_V), lambda b, i: (b * nq + i, 0)),
        out_shape=jax.ShapeDtypeStruct((t, MLA_HEADS * MLA_V), BF16),
        scratch_shapes=[pltpu.VMEM((MLA_HEADS, s // tk, tq, tk), F32),
                        pltpu.VMEM((MLA_HEADS, tq, LANE), F32),
                        pltpu.VMEM((MLA_HEADS, tq, 2 * LANE), F32)],
        compiler_params=_cparams(("arbitrary", "arbitrary")),
        name="mla_attn",
    )(mq, mk, mv)


def _dsa_kernel(iq_ref, iw_ref, ikr_ref, dq_ref, dkk_ref, dvv_ref, tri_ref, o_ref,
                key_scr, plane_scr, eq_scr, gt_scr, thr_scr, s_scr, m_scr, acc_scr,
                *, tq, th, tk, topk):
    i = pl.program_id(1)
    q_end = (i + 1) * tq
    n_chunks = (q_end + tk - 1) // tk
    lane_q = lax.broadcasted_iota(I32, (th, LANE), 1)
    int_min = jnp.int32(-2 ** 31)
    rb = 64
    lane_rb = lax.broadcasted_iota(I32, (rb, tk), 1)
    row_rb = lax.broadcasted_iota(I32, (rb, 1), 0)

    for r0 in range(0, tq, th):
        iq = iq_ref[r0:r0 + th, :].astype(F32)
        iw = iw_ref[r0:r0 + th, :]
        q_heads = []
        w_heads = []
        for h in range(IDX_HEADS):
            grp = h // 4
            sub = iq[:, grp * LANE:(grp + 1) * LANE]
            lo = (h % 4) * IDX_DIM
            in_head = (lane_q >= lo) & (lane_q < lo + IDX_DIM)
            q_heads.append(jnp.where(in_head, sub, 0.0).astype(BF16))
            w_heads.append(iw[:, h:h + 1])
        q_stacks = [jnp.concatenate([qh[r:r + rb] for qh in q_heads], axis=0) for r in range(0, th, rb)]

        def score_chunk(c, _, r0=r0, q_stacks=q_stacks, w_heads=w_heads):
            off = pl.multiple_of(c * tk, tk)
            kc = ikr_ref[pl.ds(off, tk), :]
            for bi, r in enumerate(range(0, th, rb)):
                z = _dot_nt(q_stacks[bi], kc)
                acc = w_heads[0][r:r + rb] * jnp.maximum(z[0:rb], 0.0)
                for h in range(1, IDX_HEADS):
                    acc = acc + w_heads[h][r:r + rb] * jnp.maximum(z[h * rb:(h + 1) * rb], 0.0)
                bits = pltpu.bitcast(acc, I32)
                key = jnp.where(bits < 0, bits ^ jnp.int32(0x7FFFFFFF), bits)
                key = jnp.where(c * tk + lane_rb <= i * tq + r0 + r + row_rb, key, int_min)
                key_scr[c, r0 + r:r0 + r + rb, :] = key
            return 0

        _for_chunks(n_chunks, score_chunk)

    kf = jnp.float32(topk)
    n_groups = key_scr.shape[0] * (tk // LANE)
    cb = 128

    def row_total(cnt):
        return _dot(cnt.astype(F32).astype(BF16), jnp.ones((LANE, LANE), BF16))

    def select():
        def fill(c, _):
            key_scr[c] = jnp.full((tq, tk), int_min, I32)
            return 0
        few_groups = n_chunks * (tk // LANE) <= 16
        lax.fori_loop(n_chunks, jnp.where(few_groups, min(16 // (tk // LANE), key_scr.shape[0]), key_scr.shape[0]),
                      fill, 0)

        stage_mask = {16: 0x0000FFFF, 8: 0x00FF00FF, 4: 0x0F0F0F0F, 2: 0x33333333, 1: 0x55555555}

        def build_planes(blk, _, nw):
            r = pl.multiple_of(blk * 8, 8)
            w = []
            for g in range(nw):
                if g < n_groups:
                    gc, gl = divmod(g, tk // LANE)
                    w.append(key_scr[gc, pl.ds(r, 8), gl * LANE:(gl + 1) * LANE] ^ int_min)
                else:
                    w.append(jnp.zeros((8, LANE), I32))
            j = nw // 2
            while j:
                k = 0
                while k < nw:
                    t = (w[k] ^ lax.shift_right_logical(w[k + j], jnp.int32(j))) & jnp.int32(stage_mask[j])
                    w[k] = w[k] ^ t
                    w[k + j] = w[k + j] ^ lax.shift_left(t, jnp.int32(j))
                    k = (k + j + 1) & ~j
                j >>= 1
            if nw == 16:
                w = [lax.shift_right_logical(u, jnp.int32(16)) for u in w] + [u & jnp.int32(0xFFFF) for u in w]
            for b in range(32):
                plane_scr[b, pl.ds(r, 8), :] = w[b]
            return 0

        @pl.when(few_groups)
        def _():
            lax.fori_loop(0, tq // 8, functools.partial(build_planes, nw=16), 0)

        if n_groups > 16:
            @pl.when(jnp.logical_not(few_groups))
            def _():
                lax.fori_loop(0, tq // 8, functools.partial(build_planes, nw=32), 0)

        eq_scr[...] = jnp.full((tq, LANE), -1, I32)
        gt_scr[...] = jnp.zeros((tq, LANE), I32)
        thr_scr[...] = jnp.zeros((tq, LANE), I32)

        def bit_pass(it, _):
            bit = lax.shift_left(jnp.int32(1), 31 - it)
            for r in range(0, tq, cb):
                rows = slice(r, r + cb)
                eq, gtm = eq_scr[rows, :], gt_scr[rows, :]
                t = eq & plane_scr[it, rows, :]
                u = gtm | t
                take = jnp.sum(lax.population_count(u).astype(F32), axis=1, keepdims=True) >= kf
                eq_scr[rows, :] = jnp.where(take, t, eq ^ t)
                gt_scr[rows, :] = jnp.where(take, gtm, u)
                thr_scr[rows, :] = jnp.where(take, thr_scr[rows, :] | bit, thr_scr[rows, :])
            return 0
        lax.fori_loop(0, 32, bit_pass, 0)

        thr_u = thr_scr[:, 0:1]
        thr = jnp.maximum(thr_u ^ int_min, int_min + 1)
        c_gt = row_total(lax.population_count(gt_scr[...]))[:, 0:1]
        c_ge = c_gt + row_total(lax.population_count(eq_scr[...]))[:, 0:1]
        excess = (c_ge > kf) & (thr_u != 0)

        need = kf - c_gt

        def tie_break(r):
            rows = slice(r, r + cb)
            thr_b, need_b = thr[rows], need[rows]

            def demote(c, seen):
                kk = key_scr[c, rows, :]
                tie = kk == thr_b
                rank = seen + _dot(jnp.where(tie, 1.0, 0.0).astype(BF16), tri_ref[...])
                key_scr[c, rows, :] = jnp.where(tie & (rank > need_b), thr_b - 1, kk)
                return rank[:, tk - 1:tk]
            lax.fori_loop(0, n_chunks, demote, jnp.zeros((cb, 1), F32))

        for r in range(0, tq, cb):
            pl.when(jnp.max(jnp.where(excess[r:r + cb], 1.0, 0.0)) > 0.0)(functools.partial(tie_break, r))
        return thr

    thr = lax.cond(q_end > topk, select, lambda: jnp.full((tq, 1), int_min + 1, I32))

    heads = range(DSA_HEADS)
    for r0 in range(0, tq, th):
        rows = slice(r0, r0 + th)
        thr_h = thr[rows]
        dq = dq_ref[rows, :].astype(F32)
        qm = []
        for h in heads:
            sub = dq[:, (h // 2) * LANE:(h // 2 + 1) * LANE]
            lo = (h % 2) * DSA_HEAD_DIM
            in_head = (lane_q >= lo) & (lane_q < lo + DSA_HEAD_DIM)
            qm.append(jnp.where(in_head, sub, 0.0).astype(BF16))
        qm_stack = jnp.concatenate(qm, axis=0)

        m_scr[...] = jnp.full(m_scr.shape, NEG, F32)
        acc_scr[...] = jnp.zeros(acc_scr.shape, F32)

        def phase1(c, carry, rows=rows, thr_h=thr_h, qm_stack=qm_stack):
            off = pl.multiple_of(c * tk, tk)
            sel = key_scr[c, rows, :] >= thr_h
            s = _dot_nt(qm_stack, dkk_ref[pl.ds(off, tk), :])
            for h in heads:
                sm = jnp.where(sel, s[h * th:(h + 1) * th], NEG)
                s_scr[h, c] = sm
                m_scr[h] = jnp.maximum(m_scr[h], _fold_lanes(sm, jnp.maximum))
            return carry

        _for_chunks(n_chunks, phase1)
        for h in heads:
            m_scr[h] = jnp.broadcast_to(jnp.max(m_scr[h], axis=1, keepdims=True), (th, LANE))

        def phase2(c, carry):
            off = pl.multiple_of(c * tk, tk)
            ps = []
            for h in heads:
                p = jnp.exp2(s_scr[h, c] - jnp.concatenate([m_scr[h]] * (tk // LANE), axis=1))
                ps.append(p.astype(BF16))
            pv = _dot(jnp.concatenate(ps, axis=0), dvv_ref[pl.ds(off, tk), :])
            for h in heads:
                acc_scr[h] = acc_scr[h] + pv[h * th:(h + 1) * th]
            return carry

        _for_chunks(n_chunks, phase2)
        outs = [acc_scr[h] / pltpu.roll(acc_scr[h], DSA_HEAD_DIM, 1) for h in heads]
        lo_half = lane_q < DSA_HEAD_DIM
        o_ref[rows, :] = jnp.concatenate(
            [jnp.where(lo_half, outs[0], pltpu.roll(outs[1], DSA_HEAD_DIM, 1)),
             jnp.where(lo_half, outs[2], pltpu.roll(outs[3], DSA_HEAD_DIM, 1))], axis=1).astype(BF16)


def _dsa(iq, iw, ikr, dq, dkk, dvv, nb, tq, th, tk):
    t = iq.shape[0]
    s = t // nb
    nq = s // tq
    topk = min(DSA_TOPK_MAX, s // 4)
    assert s // LANE <= 32, "the bit-sliced selection packs one key per 128-lane group into a 32-bit word"
    tri = jnp.asarray(np.triu(np.ones((tk, tk), np.float32)), BF16)

    def qspec(w):
        return pl.BlockSpec((tq, w), lambda b, i: (b * nq + i, 0))

    def kspec(w):
        return pl.BlockSpec((s, w), lambda b, i: (b, 0))

    return pl.pallas_call(
        functools.partial(_dsa_kernel, tq=tq, th=th, tk=tk, topk=topk),
        grid=(nb, nq),
        in_specs=[qspec(W_IQ), qspec(LANE), kspec(LANE), qspec(W_DQ), kspec(LANE), kspec(LANE),
                  pl.BlockSpec((tk, tk), lambda b, i: (0, 0), pipeline_mode=pl.Buffered(1))],
        out_specs=qspec(W_DQ),
        out_shape=jax.ShapeDtypeStruct((t, DSA_HEADS * DSA_HEAD_DIM), BF16),
        scratch_shapes=[pltpu.VMEM((s // tk, tq, tk), I32),
                        pltpu.VMEM((32, tq, LANE), I32),
                        pltpu.VMEM((tq, LANE), I32),
                        pltpu.VMEM((tq, LANE), I32),
                        pltpu.VMEM((tq, LANE), I32),
                        pltpu.VMEM((DSA_HEADS, s // tk, th, tk), F32)] + [pltpu.VMEM((DSA_HEADS, th, LANE), F32)] * 2,
        compiler_params=_cparams(("arbitrary", "arbitrary")),
        name="dsa",
    )(iq, iw, ikr, dq, dkk, dvv, tri)


def _gla_kernel(gq_ref, gk_ref, gv_ref, gg_ref, gr_ref, ng_ref, segv_ref, seg64_ref, bmask_ref, o_ref,
                st_scr, kbuf, cbuf, vbuf, *, tg):
    j = pl.program_id(1)
    ch = GLA_CHUNK
    rowc = lax.broadcasted_iota(I32, (tg, 1), 0) % ch

    @pl.when(j == 0)
    def _():
        st_scr[...] = jnp.zeros_like(st_scr)
        kbuf[0:ch, :] = jnp.zeros((ch, LANE), F32)
        cbuf[0:ch, :] = jnp.zeros((ch, LANE), F32)
        vbuf[0:ch, :] = jnp.zeros((ch, 2 * LANE), F32)

    cbuf[ch:ch + tg, :] = gg_ref[...]
    sh = 1
    while sh < ch:
        cur = cbuf[ch:ch + tg, :]
        prev = cbuf[ch - sh:ch - sh + tg, :]
        cbuf[ch:ch + tg, :] = cur + jnp.where(rowc >= sh, prev, 0.0)
        sh *= 2
    cum = cbuf[ch:ch + tg, :]
    q = gq_ref[...]
    k = gk_ref[...]
    v = gv_ref[...]
    kbuf[ch:ch + tg, :] = k
    vbuf[ch:ch + tg, :] = v

    qd = (q * jnp.exp(cum)).astype(BF16)
    bmask = bmask_ref[...]
    inter = []
    for c in range(tg // ch):
        rs = slice(c * ch, (c + 1) * ch)
        st = st_scr[...]
        inter.append(_dot_nt(qd[rs], st.astype(BF16)))
        last = cum[c * ch + ch - 1:c * ch + ch, :]
        kp = k[rs] * jnp.exp(last - cum[rs])
        upd = lax.dot_general(v[rs].astype(BF16), kp.astype(BF16), (((0,), (0,)), ((), ())),
                              preferred_element_type=F32)
        st_scr[...] = jnp.exp(last) * st + upd * bmask
    o = jnp.concatenate(inter, axis=0)

    segv = segv_ref[...]
    cum2 = cum * LOG2E
    cbuf[ch:ch + tg, :] = cum2
    sub = 8
    rows_sub = rowc % sub
    for d in range(sub):
        kd = kbuf[ch - d:ch - d + tg, :]
        cd = cbuf[ch - d:ch - d + tg, :]
        vd = vbuf[ch - d:ch - d + tg, :]
        e = jnp.exp2(jnp.where(rows_sub >= d, cum2 - cd, NEG))
        o = o + _dot((q * kd * e).astype(BF16), segv) * vd

    lane_k3 = lax.broadcasted_iota(I32, (1, (ch // sub - 1) * LANE), 1) % LANE // GLA_DK
    lane_v = lax.broadcasted_iota(I32, (1, 2 * LANE), 1) // GLA_DV
    far = []
    for c in range(tg // ch):
        rs = slice(c * ch, (c + 1) * ch)
        qc, kc, cc, sb = q[rs], k[rs], cum2[rs], rowc[rs] // sub
        qh, kh = [], []
        for blk in range(1, ch // sub):
            ref_row = cc[blk * sub - 1:blk * sub, :]
            qh.append(qc * jnp.exp2(jnp.where(sb == blk, cc - ref_row, NEG)))
            kh.append(kc * jnp.exp2(jnp.where(sb < blk, ref_row - cc, NEG)))
        qh = jnp.concatenate(qh, axis=1)
        kh = jnp.concatenate(kh, axis=1).astype(BF16)
        q_heads = jnp.concatenate([jnp.where(lane_k3 == h, qh, 0.0) for h in range(GLA_HEADS)], axis=0)
        att = _dot_nt(q_heads.astype(BF16), kh)
        ov = _dot(att.astype(BF16), v[rs].astype(BF16))
        oc = jnp.where(lane_v == 0, ov[0:ch], 0.0)
        for h in range(1, GLA_HEADS):
            oc = oc + jnp.where(lane_v == h, ov[h * ch:(h + 1) * ch], 0.0)
        far.append(oc)
    o = o + jnp.concatenate(far, axis=0)

    sq = o * o
    hi = sq.astype(BF16)
    lo = (sq - hi.astype(F32)).astype(BF16)
    var = _dot(hi, seg64_ref[...]) + _dot(lo, seg64_ref[...])
    o_ref[...] = (gr_ref[...] * (o * lax.rsqrt(var + EPS) * ng_ref[...])).astype(BF16)


def _gla(gq, gk, gv, gg, gr, norm_g, nb, tg):
    t = gq.shape[0]
    nj = t // nb // tg
    ng = jnp.tile(norm_g.reshape(1, GLA_DV), (1, GLA_HEADS)).astype(F32)
    hk = np.arange(GLA_HEADS * GLA_DK) // GLA_DK
    hv = np.arange(GLA_HEADS * GLA_DV) // GLA_DV
    same = (hk[:, None] == hv[None, :])
    segv = jnp.asarray(same, BF16)
    bmask = jnp.asarray(same.T, F32)
    seg64 = jnp.asarray((hv[:, None] == hv[None, :]) / GLA_DV, BF16)

    def tok(w):
        return pl.BlockSpec((tg, w), lambda b, j: (b * nj + j, 0))

    def full(a):
        return pl.BlockSpec(a.shape, lambda b, j: (0,) * a.ndim, pipeline_mode=pl.Buffered(1))

    return pl.pallas_call(
        functools.partial(_gla_kernel, tg=tg),
        grid=(nb, nj),
        in_specs=[tok(LANE), tok(LANE), tok(W_GV), tok(LANE), tok(W_GV), full(ng), full(segv), full(seg64),
                  full(bmask)],
        out_specs=tok(W_GV),
        out_shape=jax.ShapeDtypeStruct((t, GLA_HEADS * GLA_DV), BF16),
        scratch_shapes=[pltpu.VMEM((2 * LANE, LANE), F32),
                        pltpu.VMEM((GLA_CHUNK + tg, LANE), F32),
                        pltpu.VMEM((GLA_CHUNK + tg, LANE), F32),
                        pltpu.VMEM((GLA_CHUNK + tg, 2 * LANE), F32)],
        compiler_params=_cparams(("arbitrary", "arbitrary")),
        name="gla",
    )(gq, gk, gv, gg, gr, ng, segv, seg64, bmask)


def _merge_kernel(x_ref, g_ref, wg_ref, o0_ref, o1_ref, o2_ref, o3_ref, wb_ref, wo_ref, out_ref):
    x = x_ref[...]
    h = _rms(x, g_ref[...]).astype(BF16)
    merged = None
    for n, o_ref in enumerate((o0_ref, o1_ref, o2_ref, o3_ref)):
        gate = _dot(h, wg_ref[:, n * D_MODEL:(n + 1) * D_MODEL])
        gate = 1.0 / (1.0 + jnp.exp(-gate))
        term = gate * _dot(o_ref[...], wb_ref[n])
        merged = term if merged is None else merged + term
    out_ref[...] = x + _dot(merged.astype(BF16), wo_ref[...])


def _merge(x2, g, wg, branches, wb, wo, tm):
    t, d = x2.shape

    def tok(w):
        return pl.BlockSpec((tm, w), lambda i: (i, 0))

    def full(a):
        return pl.BlockSpec(a.shape, lambda i: (0,) * a.ndim, pipeline_mode=pl.Buffered(1))

    return pl.pallas_call(
        _merge_kernel,
        grid=(t // tm,),
        in_specs=[tok(d), full(g), full(wg)] + [tok(BRANCH_WIDTH)] * N_BRANCH + [full(wb), full(wo)],
        out_specs=tok(d),
        out_shape=jax.ShapeDtypeStruct((t, d), F32),
        compiler_params=_cparams(("arbitrary",)),
        name="merge",
    )(x2, g, wg, *branches, wb, wo)


def _ffn_kernel(x_ref, g_ref, wg_ref, wu_ref, wd_ref, fg_ref, out_ref, *, fc, final):
    x = x_ref[...]
    h = _rms(x, g_ref[...]).astype(BF16)
    acc = x
    for c in range(D_FF // fc):
        gate = _dot(h, wg_ref[:, c * fc:(c + 1) * fc])
        up = _dot(h, wu_ref[:, c * fc:(c + 1) * fc])
        act = (gate / (1.0 + jnp.exp(-gate)) * up).astype(BF16)
        acc = acc + _dot(act, wd_ref[c * fc:(c + 1) * fc, :])
    if final:
        acc = _rms(acc, fg_ref[...])
    out_ref[...] = acc


def _ffn(x2, g, wg, wu, wd, fg, tm, final):
    t, d = x2.shape

    def tok(w):
        return pl.BlockSpec((tm, w), lambda i: (i, 0))

    def full(a):
        return pl.BlockSpec(a.shape, lambda i: (0,) * a.ndim, pipeline_mode=pl.Buffered(1))

    return pl.pallas_call(
        functools.partial(_ffn_kernel, fc=MXU_WIDTH, final=final),
        grid=(t // tm,),
        in_specs=[tok(d), full(g), full(wg), full(wu), full(wd), full(fg)],
        out_specs=tok(d),
        out_shape=jax.ShapeDtypeStruct((t, d), F32),
        compiler_params=_cparams(("arbitrary",)),
        name="ffn",
    )(x2, g, wg, wu, wd, fg)


def _tile(n, pref):
    while n % pref:
        pref //= 2
    return pref


def kernel(x, positions, attn_norm_g, w_in, mla_q_norm_g, mla_w_q_up, mla_kv_norm_g, mla_w_kv_up, conv_w, gla_w_gate_up, gla_b_gate, gla_norm_g, w_branch, w_out, ffn_norm_g, w_ffn_gate, w_ffn_up, w_ffn_down, final_norm_g):
    b, s, d = x.shape
    depth = w_in.shape[0]
    t = b * s
    assert d == D_MODEL and s % GLA_CHUNK == 0 and conv_w.shape[1] == CONV_K and w_in.shape[2] == O_END
    tm_prep = _tile(s, 512)
    tm_wide = _tile(t, 1024)
    tq_mla = _tile(s, 512)
    tk_att = _tile(s, 512)
    tq_dsa = _tile(s, 512)
    th_dsa = _tile(tq_dsa, 256)
    tg = _tile(s, 512)

    x2 = x.reshape(t, d)
    ctab, stab = _rope_tables(positions, _tile(t, 2048))
    row = lambda a: a.reshape(1, -1).astype(F32)

    w_small = _pack_small_weight(w_in)
    wq, wqs, wk, wv = _pack_mla_weights(mla_w_q_up, mla_w_kv_up)
    wgate = jnp.concatenate([gla_w_gate_up, jnp.zeros((depth, LANE - GLA_GATE_RANK, GLA_HEADS * GLA_DK), F32)],
                            axis=1).astype(BF16)
    w_gates = w_in[:, :, O_GATE:].astype(BF16)
    w_branch_b, w_out_b = w_branch.astype(BF16), w_out.astype(BF16)
    w_gate_b, w_up_b = w_ffn_gate.astype(BF16), w_ffn_up.astype(BF16)
    w_down_b = w_ffn_down.astype(BF16)

    for l in range(depth):
        (mq, mk, mv, o_conv, dq, dkk, dvv, iq, ikr, iw, gq, gk, gv, gg, gr) = _prep(
            x2, b, row(attn_norm_g[l]), w_small[l], row(mla_q_norm_g[l]), wq[l], wqs[l], row(mla_kv_norm_g[l]),
            wk[l], wv[l], ctab, stab, conv_w[l].astype(F32), wgate[l], row(gla_b_gate[l]), tm_prep)
        o_mla = _mla(mq, mk, mv, b, tq_mla, tk_att)
        o_dsa = _dsa(iq, iw, ikr, dq, dkk, dvv, b, tq_dsa, th_dsa, tk_att)
        o_gla = _gla(gq, gk, gv, gg, gr, gla_norm_g[l], b, tg)
        x2 = _merge(x2, row(attn_norm_g[l]), w_gates[l], (o_mla, o_conv, o_dsa, o_gla), w_branch_b[l], w_out_b[l],
                    tm_wide)
        x2 = _ffn(x2, row(ffn_norm_g[l]), w_gate_b[l], w_up_b[l], w_down_b[l], row(final_norm_g), tm_wide,
                  final=(l == depth - 1))
    return x2.reshape(b, s, d)
```

```python
import functools

import jax
import jax.numpy as jnp
import numpy as np
from jax import lax
from jax.experimental import pallas as pl
from jax.experimental.pallas import tpu as pltpu

F32 = jnp.float32
BF16 = jnp.bfloat16
I32 = jnp.int32

D_MODEL = 1024
MLA_HEADS = 4
MLA_Q_RANK = 256
MLA_KV_RANK = 128
MLA_NOPE = 64
MLA_ROPE = 32
MLA_V = 64
ROPE_THETA = 10000.0
CONV_WIDTH = 256
CONV_K = 3
DSA_HEADS = 4
DSA_HEAD_DIM = 64
IDX_HEADS = 8
IDX_DIM = 32
DSA_TOPK_MAX = 256
GLA_HEADS = 4
GLA_DK = 32
GLA_DV = 64
GLA_GATE_RANK = 16
GLA_TAU = 16.0
GLA_CHUNK = 64
N_BRANCH = 4
BRANCH_WIDTH = 256
D_FF = ((8 * D_MODEL + 3 * 256 - 1) // (3 * 256)) * 256
EPS = 1e-6
NEG = -1e30
LOG2E = 1.4426950408889634

LANE = 128
MXU_WIDTH = 256
HEAD_PAD = 128
W_MLA = MLA_HEADS * HEAD_PAD
W_DQ = DSA_HEADS * DSA_HEAD_DIM
W_IQ = IDX_HEADS * IDX_DIM
W_GV = GLA_HEADS * GLA_DV

IN_SIZES = (
    MLA_Q_RANK, MLA_KV_RANK, MLA_ROPE,
    CONV_WIDTH, CONV_WIDTH, CONV_WIDTH,
    DSA_HEADS * DSA_HEAD_DIM, DSA_HEAD_DIM, DSA_HEAD_DIM,
    IDX_HEADS * IDX_DIM, IDX_DIM, IDX_HEADS,
    GLA_HEADS * GLA_DK, GLA_HEADS * GLA_DK, GLA_HEADS * GLA_DV,
    GLA_GATE_RANK, GLA_HEADS * GLA_DV,
    N_BRANCH * D_MODEL,
)
_OFF = np.concatenate([[0], np.cumsum(IN_SIZES)]).astype(int)
(O_CQ, O_CKV, O_KR, O_CB, O_CC, O_CX, O_DQ, O_DK, O_DV, O_IQ, O_IK, O_IW,
 O_GQ, O_GK, O_GV, O_GLR, O_GR, O_GATE, O_END) = [int(v) for v in _OFF]

P_CQ, P_CKV, P_KRP, P_KRS = 0, 256, 384, 512
P_CB, P_CC, P_CX = 640, 896, 1152
P_DQ, P_DKK, P_DVV = 1408, 1664, 1792
P_IQ, P_IKR, P_IW = 1920, 2176, 2304
P_GQ, P_GK, P_GV, P_GLR, P_GR = 2432, 2560, 2688, 2944, 3072
P_TOTAL = 3328

VMEM_LIMIT = 56 * 1024 * 1024


def _cparams(sem):
    return pltpu.CompilerParams(dimension_semantics=sem, vmem_limit_bytes=VMEM_LIMIT)


def _rms(x, g):
    return x * lax.rsqrt(jnp.mean(x * x, axis=-1, keepdims=True) + EPS) * g


def _dot(a, b):
    return jnp.dot(a, b, preferred_element_type=F32)


def _dot_nt(a, b):
    return lax.dot_general(a, b, (((1,), (1,)), ((), ())), preferred_element_type=F32)


def _rope_kernel(pos_ref, pat_ref, c_ref, s_ref):
    pos = pos_ref[...].astype(F32)
    ang = pos * pat_ref[0:1, :]
    c_ref[...] = pat_ref[1:2, :] + pat_ref[2:3, :] * jnp.cos(ang)
    s_ref[...] = pat_ref[3:4, :] * jnp.sin(ang)


def _rope_tables(positions, tm):
    t = positions.size
    half = MLA_ROPE // 2
    inv_freq = ROPE_THETA ** (-jnp.arange(half, dtype=F32) / half)
    z64, z32 = jnp.zeros((MLA_NOPE,), F32), jnp.zeros((HEAD_PAD - MLA_NOPE - MLA_ROPE,), F32)
    o16 = jnp.ones((half,), F32)
    pat = jnp.stack([
        jnp.concatenate([z64, inv_freq, inv_freq, z32]),
        jnp.concatenate([z64 + 1.0, 0 * o16, 0 * o16, z32]),
        jnp.concatenate([z64, o16, o16, z32]),
        jnp.concatenate([z64, -o16, o16, z32]),
    ])
    pat = jnp.concatenate([pat, jnp.zeros((4, HEAD_PAD), F32)], axis=0)
    return pl.pallas_call(
        _rope_kernel,
        grid=(t // tm,),
        in_specs=[pl.BlockSpec((tm, 1), lambda i: (i, 0)), pl.BlockSpec((8, HEAD_PAD), lambda i: (0, 0))],
        out_specs=[pl.BlockSpec((tm, HEAD_PAD), lambda i: (i, 0))] * 2,
        out_shape=[jax.ShapeDtypeStruct((t, HEAD_PAD), F32)] * 2,
        compiler_params=_cparams(("arbitrary",)),
        name="rope_tables",
    )(positions.reshape(t, 1), pat)


def _prep_kernel(x_ref, g_ref, w_ref, qg_ref, wq_ref, wqs_ref, kvg_ref, wk_ref, wv_ref, c_ref, s_ref,
                 convw_ref, wgate_ref, bgate_ref,
                 mq_ref, mk_ref, mv_ref, oconv_ref, dq_ref, dkk_ref, dvv_ref, iq_ref, ikr_ref, iw_ref,
                 gq_ref, gk_ref, gv_ref, gg_ref, gr_ref, u_scr, *, tm):
    j = pl.program_id(1)
    h = _rms(x_ref[...], g_ref[...]).astype(BF16)
    p = _dot(h, w_ref[...])

    c1, s1 = c_ref[...], s_ref[...]
    c4 = jnp.concatenate([c1] * MLA_HEADS, axis=1)
    s4 = jnp.concatenate([s1] * MLA_HEADS, axis=1)

    qn = _rms(p[:, P_CQ:P_CQ + MLA_Q_RANK], qg_ref[...]).astype(BF16)
    q = _dot(qn, wq_ref[...]) * c4 + _dot(qn, wqs_ref[...]) * s4
    mq_ref[...] = (q * ((MLA_NOPE + MLA_ROPE) ** -0.5 * LOG2E)).astype(BF16)
    kvn = _rms(p[:, P_CKV:P_CKV + MLA_KV_RANK], kvg_ref[...]).astype(BF16)
    kr = p[:, P_KRP:P_KRP + HEAD_PAD] * c1 + p[:, P_KRS:P_KRS + HEAD_PAD] * s1
    mk_ref[...] = (_dot(kvn, wk_ref[...]) + jnp.concatenate([kr] * MLA_HEADS, axis=1)).astype(BF16)
    vv = _dot(kvn, wv_ref[...])
    ones = jnp.ones((tm, LANE), F32)
    mv_ref[...] = jnp.concatenate([vv[:, 0:LANE], ones, vv[:, LANE:2 * LANE], ones], axis=1).astype(BF16)

    u = p[:, P_CC:P_CC + CONV_WIDTH] * p[:, P_CX:P_CX + CONV_WIDTH]

    @pl.when(j == 0)
    def _():
        u_scr[0:8, :] = jnp.zeros((8, CONV_WIDTH), F32)

    u_scr[8:8 + tm, :] = u
    cw = convw_ref[...]
    y = cw[0:1, :] * u_scr[6:6 + tm, :] + cw[1:2, :] * u_scr[7:7 + tm, :] + cw[2:3, :] * u
    oconv_ref[...] = (p[:, P_CB:P_CB + CONV_WIDTH] * y).astype(BF16)
    u_scr[0:8, :] = u_scr[tm:tm + 8, :]

    dq_ref[...] = (p[:, P_DQ:P_DQ + W_DQ] * (DSA_HEAD_DIM ** -0.5 * LOG2E)).astype(BF16)
    dkk_ref[...] = p[:, P_DKK:P_DKK + LANE].astype(BF16)
    hi_lanes = lax.broadcasted_iota(I32, (tm, LANE), 1) >= DSA_HEAD_DIM
    dvv_ref[...] = jnp.where(hi_lanes, 1.0, p[:, P_DVV:P_DVV + LANE]).astype(BF16)
    iq_ref[...] = p[:, P_IQ:P_IQ + W_IQ].astype(BF16)
    ikr_ref[...] = p[:, P_IKR:P_IKR + LANE].astype(BF16)
    iw_ref[...] = p[:, P_IW:P_IW + LANE] * ((IDX_DIM ** -0.5) * (IDX_HEADS ** -0.5))

    gq_ref[...] = p[:, P_GQ:P_GQ + LANE] * (GLA_DK ** -0.5)
    gk_ref[...] = p[:, P_GK:P_GK + LANE]
    gv_ref[...] = p[:, P_GV:P_GV + W_GV]
    gate = _dot(p[:, P_GLR:P_GLR + LANE].astype(BF16), wgate_ref[...]) + bgate_ref[...]
    log_sig = jnp.minimum(gate, 0.0) - jnp.log1p(jnp.exp(-jnp.abs(gate)))
    gg_ref[...] = log_sig * (1.0 / GLA_TAU)
    gr = p[:, P_GR:P_GR + W_GV]
    gr_ref[...] = gr / (1.0 + jnp.exp(-gr))


def _pack_small_weight(w_in):
    def cols(a, b):
        return w_in[..., a:b]

    def zeros(n):
        return jnp.zeros(w_in.shape[:-1] + (n,), w_in.dtype)

    half = MLA_ROPE // 2
    kr1, kr2 = cols(O_KR, O_KR + half), cols(O_KR + half, O_KR + MLA_ROPE)
    pad_tail = HEAD_PAD - MLA_NOPE - MLA_ROPE
    pieces = [
        cols(O_CQ, O_CKV), cols(O_CKV, O_KR),
        zeros(MLA_NOPE), kr1, kr2, zeros(pad_tail),
        zeros(MLA_NOPE), kr2, kr1, zeros(pad_tail),
        cols(O_CB, O_CC), cols(O_CC, O_CX), cols(O_CX, O_DQ),
        cols(O_DQ, O_DK), cols(O_DK, O_DV), cols(O_DK, O_DV), cols(O_DV, O_IQ), zeros(DSA_HEAD_DIM),
        cols(O_IQ, O_IK), cols(O_IK, O_IW), cols(O_IK, O_IW), cols(O_IK, O_IW), cols(O_IK, O_IW),
        cols(O_IW, O_GQ), zeros(LANE - IDX_HEADS),
        cols(O_GQ, O_GK), cols(O_GK, O_GV), cols(O_GV, O_GLR),
        cols(O_GLR, O_GR), zeros(LANE - GLA_GATE_RANK),
        cols(O_GR, O_GATE),
    ]
    out = jnp.concatenate(pieces, axis=-1)
    assert out.shape[-1] == P_TOTAL, out.shape
    return out.astype(BF16)


def _pack_mla_weights(w_q_up, w_kv_up):
    half = MLA_ROPE // 2
    lq = w_q_up.shape[:-1]
    lk = w_kv_up.shape[:-1]
    wq = w_q_up.reshape(lq + (MLA_HEADS, MLA_NOPE + MLA_ROPE))
    nope, x1, x2 = wq[..., :MLA_NOPE], wq[..., MLA_NOPE:MLA_NOPE + half], wq[..., MLA_NOPE + half:]
    zq = jnp.zeros(lq + (MLA_HEADS, HEAD_PAD - MLA_NOPE - MLA_ROPE), w_q_up.dtype)
    wq_pad = jnp.concatenate([nope, x1, x2, zq], axis=-1).reshape(lq + (MLA_HEADS * HEAD_PAD,))
    wq_swap = jnp.concatenate([jnp.zeros_like(nope), x2, x1, zq], axis=-1).reshape(lq + (MLA_HEADS * HEAD_PAD,))
    wkv = w_kv_up.reshape(lk + (MLA_HEADS, MLA_NOPE + MLA_V))
    zk = jnp.zeros(lk + (MLA_HEADS, HEAD_PAD - MLA_NOPE), w_kv_up.dtype)
    wk_pad = jnp.concatenate([wkv[..., :MLA_NOPE], zk], axis=-1).reshape(lk + (MLA_HEADS * HEAD_PAD,))
    wv = wkv[..., MLA_NOPE:].reshape(lk + (MLA_HEADS * MLA_V,))
    return wq_pad.astype(BF16), wq_swap.astype(BF16), wk_pad.astype(BF16), wv.astype(BF16)


def _prep(x2, nb, g, w_small, qg, wq, wqs, kvg, wk, wv, ctab, stab, conv_w, wgate, bgate, tm):
    t, d = x2.shape
    nj = t // nb // tm

    def tok(width):
        return pl.BlockSpec((tm, width), lambda b, j: (b * nj + j, 0))

    def full(a):
        return pl.BlockSpec(a.shape, lambda b, j: (0,) * a.ndim, pipeline_mode=pl.Buffered(1))

    outs = [(W_MLA, BF16), (W_MLA, BF16), (W_MLA, BF16), (CONV_WIDTH, BF16), (W_DQ, BF16), (LANE, BF16), (LANE, BF16),
            (W_IQ, BF16), (LANE, BF16), (LANE, F32), (LANE, F32), (LANE, F32), (W_GV, F32), (LANE, F32), (W_GV, F32)]
    return pl.pallas_call(
        functools.partial(_prep_kernel, tm=tm),
        grid=(nb, nj),
        in_specs=[tok(d), full(g), full(w_small), full(qg), full(wq), full(wqs), full(kvg), full(wk), full(wv),
                  tok(HEAD_PAD), tok(HEAD_PAD), full(conv_w), full(wgate), full(bgate)],
        out_specs=[tok(w) for w, _ in outs],
        out_shape=[jax.ShapeDtypeStruct((t, w), dt) for w, dt in outs],
        scratch_shapes=[pltpu.VMEM((tm + 8, CONV_WIDTH), F32)],
        compiler_params=_cparams(("arbitrary", "arbitrary")),
        name="prep",
    )(x2, g, w_small, qg, wq, wqs, kvg, wk, wv, ctab, stab, conv_w, wgate, bgate)


def _for_chunks(n, body):
    def group(i, carry):
        for u in range(4):
            body(4 * i + u, carry)
        return carry

    lax.fori_loop(0, n // 4, group, 0)
    rem = n % 4

    @pl.when(rem >= 2)
    def _():
        body(n - rem, 0)
        body(n - rem + 1, 0)

    @pl.when(rem % 2 == 1)
    def _():
        body(n - 1, 0)


def _fold_lanes(x, op):
    out = x[:, 0:LANE]
    for g in range(1, x.shape[1] // LANE):
        out = op(out, x[:, g * LANE:(g + 1) * LANE])
    return out


def _mla_kernel(q_ref, k_ref, v_ref, o_ref, s_scr, m_scr, acc_scr, *, tq, tk):
    i = pl.program_id(1)
    n_chunks = ((i + 1) * tq + tk - 1) // tk
    row = i * tq + lax.broadcasted_iota(I32, (tq, tk), 0)
    col = lax.broadcasted_iota(I32, (tq, tk), 1)
    lane = lax.broadcasted_iota(I32, (tq, LANE), 1)
    heads = range(MLA_HEADS)
    hs = [slice(h * HEAD_PAD, (h + 1) * HEAD_PAD) for h in heads]
    vs = [slice((h // 2) * 2 * LANE, (h // 2 + 1) * 2 * LANE) for h in heads]

    m_scr[...] = jnp.full(m_scr.shape, NEG, F32)
    acc_scr[...] = jnp.zeros(acc_scr.shape, F32)

    def phase1(c, carry):
        off = pl.multiple_of(c * tk, tk)
        causal = c * tk + col <= row
        for h in heads:
            s = jnp.where(causal, _dot_nt(q_ref[:, hs[h]], k_ref[pl.ds(off, tk), hs[h]]), NEG)
            s_scr[h, c] = s
            m_scr[h] = jnp.maximum(m_scr[h], _fold_lanes(s, jnp.maximum))
        return carry

    _for_chunks(n_chunks, phase1)
    for h in heads:
        m_scr[h] = jnp.broadcast_to(jnp.max(m_scr[h], axis=1, keepdims=True), (tq, LANE))

    def phase2(c, carry):
        off = pl.multiple_of(c * tk, tk)
        for pair in range(MLA_HEADS // 2):
            ps = [jnp.exp2(s_scr[h, c] - jnp.concatenate([m_scr[h]] * (tk // LANE), axis=1)).astype(BF16)
                  for h in (2 * pair, 2 * pair + 1)]
            pv = _dot(jnp.concatenate(ps, axis=0), v_ref[pl.ds(off, tk), vs[2 * pair]])
            acc_scr[2 * pair] = acc_scr[2 * pair] + pv[0:tq]
            acc_scr[2 * pair + 1] = acc_scr[2 * pair + 1] + pv[tq:2 * tq]
        return carry

    _for_chunks(n_chunks, phase2)
    res = [acc_scr[h, :, 0:LANE] / acc_scr[h, :, LANE:2 * LANE] for h in heads]
    for pair in range(MLA_HEADS // 2):
        o_ref[:, pair * LANE:(pair + 1) * LANE] = jnp.where(lane < MLA_V, res[2 * pair], res[2 * pair + 1]).astype(BF16)


def _mla(mq, mk, mv, nb, tq, tk):
    t = mq.shape[0]
    s = t // nb
    nq = s // tq
    return pl.pallas_call(
        functools.partial(_mla_kernel, tq=tq, tk=tk),
        grid=(nb, nq),
        in_specs=[pl.BlockSpec((tq, MLA_HEADS * HEAD_PAD), lambda b, i: (b * nq + i, 0)),
                  pl.BlockSpec((s, MLA_HEADS * HEAD_PAD), lambda b, i: (b, 0)),
                  pl.BlockSpec((s, MLA_HEADS * HEAD_PAD), lambda b, i: (b, 0))],
        out_specs=pl.BlockSpec((tq, MLA_HEADS * MLA_V), lambda b, i: (b * nq + i, 0)),
        out_shape=jax.ShapeDtypeStruct((t, MLA_HEADS * MLA_V), BF16),
        scratch_shapes=[pltpu.VMEM((MLA_HEADS, s // tk, tq, tk), F32),
                        pltpu.VMEM((MLA_HEADS, tq, LANE), F32),
                        pltpu.VMEM((MLA_HEADS, tq, 2 * LANE), F32)],
        compiler_params=_cparams(("arbitrary", "arbitrary")),
        name="mla_attn",
    )(mq, mk, mv)


def _dsa_kernel(iq_ref, iw_ref, ikr_ref, dq_ref, dkk_ref, dvv_ref, tri_ref, o_ref,
                key_scr, plane_scr, eq_scr, gt_scr, thr_scr, s_scr, m_scr, acc_scr,
                *, tq, th, tk, topk):
    i = pl.program_id(1)
    q_end = (i + 1) * tq
    n_chunks = (q_end + tk - 1) // tk
    lane_q = lax.broadcasted_iota(I32, (th, LANE), 1)
    int_min = jnp.int32(-2 ** 31)
    rb = 64
    lane_rb = lax.broadcasted_iota(I32, (rb, tk), 1)
    row_rb = lax.broadcasted_iota(I32, (rb, 1), 0)

    for r0 in range(0, tq, th):
        iq = iq_ref[r0:r0 + th, :].astype(F32)
        iw = iw_ref[r0:r0 + th, :]
        q_heads = []
        w_heads = []
        for h in range(IDX_HEADS):
            grp = h // 4
            sub = iq[:, grp * LANE:(grp + 1) * LANE]
            lo = (h % 4) * IDX_DIM
            in_head = (lane_q >= lo) & (lane_q < lo + IDX_DIM)
            q_heads.append(jnp.where(in_head, sub, 0.0).astype(BF16))
            w_heads.append(iw[:, h:h + 1])
        q_stacks = [jnp.concatenate([qh[r:r + rb] for qh in q_heads], axis=0) for r in range(0, th, rb)]

        def score_chunk(c, _, r0=r0, q_stacks=q_stacks, w_heads=w_heads):
            off = pl.multiple_of(c * tk, tk)
            kc = ikr_ref[pl.ds(off, tk), :]
            for bi, r in enumerate(range(0, th, rb)):
                z = _dot_nt(q_stacks[bi], kc)
                acc = w_heads[0][r:r + rb] * jnp.maximum(z[0:rb], 0.0)
                for h in range(1, IDX_HEADS):
                    acc = acc + w_heads[h][r:r + rb] * jnp.maximum(z[h * rb:(h + 1) * rb], 0.0)
                bits = pltpu.bitcast(acc, I32)
                key = jnp.where(bits < 0, bits ^ jnp.int32(0x7FFFFFFF), bits)
                key = jnp.where(c * tk + lane_rb <= i * tq + r0 + r + row_rb, key, int_min)
                key_scr[c, r0 + r:r0 + r + rb, :] = key
            return 0

        _for_chunks(n_chunks, score_chunk)

    kf = jnp.float32(topk)
    n_groups = key_scr.shape[0] * (tk // LANE)
    cb = 128

    def row_total(cnt):
        return _dot(cnt.astype(F32).astype(BF16), jnp.ones((LANE, LANE), BF16))

    def select():
        def fill(c, _):
            key_scr[c] = jnp.full((tq, tk), int_min, I32)
            return 0
        few_groups = n_chunks * (tk // LANE) <= 16
        lax.fori_loop(n_chunks, jnp.where(few_groups, min(16 // (tk // LANE), key_scr.shape[0]), key_scr.shape[0]),
                      fill, 0)

        stage_mask = {16: 0x0000FFFF, 8: 0x00FF00FF, 4: 0x0F0F0F0F, 2: 0x33333333, 1: 0x55555555}

        def build_planes(blk, _, nw):
            r = pl.multiple_of(blk * 8, 8)
            w = []
            for g in range(nw):
                if g < n_groups:
                    gc, gl = divmod(g, tk // LANE)
                    w.append(key_scr[gc, pl.ds(r, 8), gl * LANE:(gl + 1) * LANE] ^ int_min)
                else:
                    w.append(jnp.zeros((8, LANE), I32))
            j = nw // 2
            while j:
                k = 0
                while k < nw:
                    t = (w[k] ^ lax.shift_right_logical(w[k + j], jnp.int32(j))) & jnp.int32(stage_mask[j])
                    w[k] = w[k] ^ t
                    w[k + j] = w[k + j] ^ lax.shift_left(t, jnp.int32(j))
                    k = (k + j + 1) & ~j
                j >>= 1
            if nw == 16:
                w = [lax.shift_right_logical(u, jnp.int32(16)) for u in w] + [u & jnp.int32(0xFFFF) for u in w]
            for b in range(32):
                plane_scr[b, pl.ds(r, 8), :] = w[b]
            return 0

        @pl.when(few_groups)
        def _():
            lax.fori_loop(0, tq // 8, functools.partial(build_planes, nw=16), 0)

        if n_groups > 16:
            @pl.when(jnp.logical_not(few_groups))
            def _():
                lax.fori_loop(0, tq // 8, functools.partial(build_planes, nw=32), 0)

        eq_scr[...] = jnp.full((tq, LANE), -1, I32)
        gt_scr[...] = jnp.zeros((tq, LANE), I32)
        thr_scr[...] = jnp.zeros((tq, LANE), I32)

        def bit_pass(it, _):
            bit = lax.shift_left(jnp.int32(1), 31 - it)
            for r in range(0, tq, cb):
                rows = slice(r, r + cb)
                eq, gtm = eq_scr[rows, :], gt_scr[rows, :]
                t = eq & plane_scr[it, rows, :]
                u = gtm | t
                take = jnp.sum(lax.population_count(u).astype(F32), axis=1, keepdims=True) >= kf
                eq_scr[rows, :] = jnp.where(take, t, eq ^ t)
                gt_scr[rows, :] = jnp.where(take, gtm, u)
                thr_scr[rows, :] = jnp.where(take, thr_scr[rows, :] | bit, thr_scr[rows, :])
            return 0
        lax.fori_loop(0, 32, bit_pass, 0)

        thr_u = thr_scr[:, 0:1]
        thr = jnp.maximum(thr_u ^ int_min, int_min + 1)
        c_gt = row_total(lax.population_count(gt_scr[...]))[:, 0:1]
        c_ge = c_gt + row_total(lax.population_count(eq_scr[...]))[:, 0:1]
        excess = (c_ge > kf) & (thr_u != 0)

        need = kf - c_gt

        def tie_break(r):
            rows = slice(r, r + cb)
            thr_b, need_b = thr[rows], need[rows]

            def demote(c, seen):
                kk = key_scr[c, rows, :]
                tie = kk == thr_b
                rank = seen + _dot(jnp.where(tie, 1.0, 0.0).astype(BF16), tri_ref[...])
                key_scr[c, rows, :] = jnp.where(tie & (rank > need_b), thr_b - 1, kk)
                return rank[:, tk - 1:tk]
            lax.fori_loop(0, n_chunks, demote, jnp.zeros((cb, 1), F32))

        for r in range(0, tq, cb):
            pl.when(jnp.max(jnp.where(excess[r:r + cb], 1.0, 0.0)) > 0.0)(functools.partial(tie_break, r))
        return thr

    thr = lax.cond(q_end > topk, select, lambda: jnp.full((tq, 1), int_min + 1, I32))

    heads = range(DSA_HEADS)
    for r0 in range(0, tq, th):
        rows = slice(r0, r0 + th)
        thr_h = thr[rows]
        dq = dq_ref[rows, :].astype(F32)
        qm = []
        for h in heads:
            sub = dq[:, (h // 2) * LANE:(h // 2 + 1) * LANE]
            lo = (h % 2) * DSA_HEAD_DIM
            in_head = (lane_q >= lo) & (lane_q < lo + DSA_HEAD_DIM)
            qm.append(jnp.where(in_head, sub, 0.0).astype(BF16))
        qm_stack = jnp.concatenate(qm, axis=0)

        m_scr[...] = jnp.full(m_scr.shape, NEG, F32)
        acc_scr[...] = jnp.zeros(acc_scr.shape, F32)

        def phase1(c, carry, rows=rows, thr_h=thr_h, qm_stack=qm_stack):
            off = pl.multiple_of(c * tk, tk)
            sel = key_scr[c, rows, :] >= thr_h
            s = _dot_nt(qm_stack, dkk_ref[pl.ds(off, tk), :])
            for h in heads:
                sm = jnp.where(sel, s[h * th:(h + 1) * th], NEG)
                s_scr[h, c] = sm
                m_scr[h] = jnp.maximum(m_scr[h], _fold_lanes(sm, jnp.maximum))
            return carry

        _for_chunks(n_chunks, phase1)
        for h in heads:
            m_scr[h] = jnp.broadcast_to(jnp.max(m_scr[h], axis=1, keepdims=True), (th, LANE))

        def phase2(c, carry):
            off = pl.multiple_of(c * tk, tk)
            ps = []
            for h in heads:
                p = jnp.exp2(s_scr[h, c] - jnp.concatenate([m_scr[h]] * (tk // LANE), axis=1))
                ps.append(p.astype(BF16))
            pv = _dot(jnp.concatenate(ps, axis=0), dvv_ref[pl.ds(off, tk), :])
            for h in heads:
                acc_scr[h] = acc_scr[h] + pv[h * th:(h + 1) * th]
            return carry

        _for_chunks(n_chunks, phase2)
        outs = [acc_scr[h] / pltpu.roll(acc_scr[h], DSA_HEAD_DIM, 1) for h in heads]
        lo_half = lane_q < DSA_HEAD_DIM
        o_ref[rows, :] = jnp.concatenate(
            [jnp.where(lo_half, outs[0], pltpu.roll(outs[1], DSA_HEAD_DIM, 1)),
             jnp.where(lo_half, outs[2], pltpu.roll(outs[3], DSA_HEAD_DIM, 1))], axis=1).astype(BF16)


def _dsa(iq, iw, ikr, dq, dkk, dvv, nb, tq, th, tk):
    t = iq.shape[0]
    s = t // nb
    nq = s // tq
    topk = min(DSA_TOPK_MAX, s // 4)
    assert s // LANE <= 32, "the bit-sliced selection packs one key per 128-lane group into a 32-bit word"
    tri = jnp.asarray(np.triu(np.ones((tk, tk), np.float32)), BF16)

    def qspec(w):
        return pl.BlockSpec((tq, w), lambda b, i: (b * nq + i, 0))

    def kspec(w):
        return pl.BlockSpec((s, w), lambda b, i: (b, 0))

    return pl.pallas_call(
        functools.partial(_dsa_kernel, tq=tq, th=th, tk=tk, topk=topk),
        grid=(nb, nq),
        in_specs=[qspec(W_IQ), qspec(LANE), kspec(LANE), qspec(W_DQ), kspec(LANE), kspec(LANE),
                  pl.BlockSpec((tk, tk), lambda b, i: (0, 0), pipeline_mode=pl.Buffered(1))],
        out_specs=qspec(W_DQ),
        out_shape=jax.ShapeDtypeStruct((t, DSA_HEADS * DSA_HEAD_DIM), BF16),
        scratch_shapes=[pltpu.VMEM((s // tk, tq, tk), I32),
                        pltpu.VMEM((32, tq, LANE), I32),
                        pltpu.VMEM((tq, LANE), I32),
                        pltpu.VMEM((tq, LANE), I32),
                        pltpu.VMEM((tq, LANE), I32),
                        pltpu.VMEM((DSA_HEADS, s // tk, th, tk), F32)] + [pltpu.VMEM((DSA_HEADS, th, LANE), F32)] * 2,
        compiler_params=_cparams(("arbitrary", "arbitrary")),
        name="dsa",
    )(iq, iw, ikr, dq, dkk, dvv, tri)


def _gla_kernel(gq_ref, gk_ref, gv_ref, gg_ref, gr_ref, ng_ref, segv_ref, seg64_ref, bmask_ref, o_ref,
                st_scr, kbuf, cbuf, vbuf, *, tg):
    j = pl.program_id(1)
    ch = GLA_CHUNK
    rowc = lax.broadcasted_iota(I32, (tg, 1), 0) % ch

    @pl.when(j == 0)
    def _():
        st_scr[...] = jnp.zeros_like(st_scr)
        kbuf[0:ch, :] = jnp.zeros((ch, LANE), F32)
        cbuf[0:ch, :] = jnp.zeros((ch, LANE), F32)
        vbuf[0:ch, :] = jnp.zeros((ch, 2 * LANE), F32)

    cbuf[ch:ch + tg, :] = gg_ref[...]
    sh = 1
    while sh < ch:
        cur = cbuf[ch:ch + tg, :]
        prev = cbuf[ch - sh:ch - sh + tg, :]
        cbuf[ch:ch + tg, :] = cur + jnp.where(rowc >= sh, prev, 0.0)
        sh *= 2
    cum = cbuf[ch:ch + tg, :]
    q = gq_ref[...]
    k = gk_ref[...]
    v = gv_ref[...]
    kbuf[ch:ch + tg, :] = k
    vbuf[ch:ch + tg, :] = v

    qd = (q * jnp.exp(cum)).astype(BF16)
    bmask = bmask_ref[...]
    inter = []
    for c in range(tg // ch):
        rs = slice(c * ch, (c + 1) * ch)
        st = st_scr[...]
        inter.append(_dot_nt(qd[rs], st.astype(BF16)))
        last = cum[c * ch + ch - 1:c * ch + ch, :]
        kp = k[rs] * jnp.exp(last - cum[rs])
        upd = lax.dot_general(v[rs].astype(BF16), kp.astype(BF16), (((0,), (0,)), ((), ())),
                              preferred_element_type=F32)
        st_scr[...] = jnp.exp(last) * st + upd * bmask
    o = jnp.concatenate(inter, axis=0)

    segv = segv_ref[...]
    cum2 = cum * LOG2E
    cbuf[ch:ch + tg, :] = cum2
    sub = 8
    rows_sub = rowc % sub
    for d in range(sub):
        kd = kbuf[ch - d:ch - d + tg, :]
        cd = cbuf[ch - d:ch - d + tg, :]
        vd = vbuf[ch - d:ch - d + tg, :]
        e = jnp.exp2(jnp.where(rows_sub >= d, cum2 - cd, NEG))
        o = o + _dot((q * kd * e).astype(BF16), segv) * vd

    lane_k3 = lax.broadcasted_iota(I32, (1, (ch // sub - 1) * LANE), 1) % LANE // GLA_DK
    lane_v = lax.broadcasted_iota(I32, (1, 2 * LANE), 1) // GLA_DV
    far = []
    for c in range(tg // ch):
        rs = slice(c * ch, (c + 1) * ch)
        qc, kc, cc, sb = q[rs], k[rs], cum2[rs], rowc[rs] // sub
        qh, kh = [], []
        for blk in range(1, ch // sub):
            ref_row = cc[blk * sub - 1:blk * sub, :]
            qh.append(qc * jnp.exp2(jnp.where(sb == blk, cc - ref_row, NEG)))
            kh.append(kc * jnp.exp2(jnp.where(sb < blk, ref_row - cc, NEG)))
        qh = jnp.concatenate(qh, axis=1)
        kh = jnp.concatenate(kh, axis=1).astype(BF16)
        q_heads = jnp.concatenate([jnp.where(lane_k3 == h, qh, 0.0) for h in range(GLA_HEADS)], axis=0)
        att = _dot_nt(q_heads.astype(BF16), kh)
        ov = _dot(att.astype(BF16), v[rs].astype(BF16))
        oc = jnp.where(lane_v == 0, ov[0:ch], 0.0)
        for h in range(1, GLA_HEADS):
            oc = oc + jnp.where(lane_v == h, ov[h * ch:(h + 1) * ch], 0.0)
        far.append(oc)
    o = o + jnp.concatenate(far, axis=0)

    sq = o * o
    hi = sq.astype(BF16)
    lo = (sq - hi.astype(F32)).astype(BF16)
    var = _dot(hi, seg64_ref[...]) + _dot(lo, seg64_ref[...])
    o_ref[...] = (gr_ref[...] * (o * lax.rsqrt(var + EPS) * ng_ref[...])).astype(BF16)


def _gla(gq, gk, gv, gg, gr, norm_g, nb, tg):
    t = gq.shape[0]
    nj = t // nb // tg
    ng = jnp.tile(norm_g.reshape(1, GLA_DV), (1, GLA_HEADS)).astype(F32)
    hk = np.arange(GLA_HEADS * GLA_DK) // GLA_DK
    hv = np.arange(GLA_HEADS * GLA_DV) // GLA_DV
    same = (hk[:, None] == hv[None, :])
    segv = jnp.asarray(same, BF16)
    bmask = jnp.asarray(same.T, F32)
    seg64 = jnp.asarray((hv[:, None] == hv[None, :]) / GLA_DV, BF16)

    def tok(w):
        return pl.BlockSpec((tg, w), lambda b, j: (b * nj + j, 0))

    def full(a):
        return pl.BlockSpec(a.shape, lambda b, j: (0,) * a.ndim, pipeline_mode=pl.Buffered(1))

    return pl.pallas_call(
        functools.partial(_gla_kernel, tg=tg),
        grid=(nb, nj),
        in_specs=[tok(LANE), tok(LANE), tok(W_GV), tok(LANE), tok(W_GV), full(ng), full(segv), full(seg64),
                  full(bmask)],
        out_specs=tok(W_GV),
        out_shape=jax.ShapeDtypeStruct((t, GLA_HEADS * GLA_DV), BF16),
        scratch_shapes=[pltpu.VMEM((2 * LANE, LANE), F32),
                        pltpu.VMEM((GLA_CHUNK + tg, LANE), F32),
                        pltpu.VMEM((GLA_CHUNK + tg, LANE), F32),
                        pltpu.VMEM((GLA_CHUNK + tg, 2 * LANE), F32)],
        compiler_params=_cparams(("arbitrary", "arbitrary")),
        name="gla",
    )(gq, gk, gv, gg, gr, ng, segv, seg64, bmask)


def _merge_kernel(x_ref, g_ref, wg_ref, o0_ref, o1_ref, o2_ref, o3_ref, wb_ref, wo_ref, out_ref):
    x = x_ref[...]
    h = _rms(x, g_ref[...]).astype(BF16)
    merged = None
    for n, o_ref in enumerate((o0_ref, o1_ref, o2_ref, o3_ref)):
        gate = _dot(h, wg_ref[:, n * D_MODEL:(n + 1) * D_MODEL])
        gate = 1.0 / (1.0 + jnp.exp(-gate))
        term = gate * _dot(o_ref[...], wb_ref[n])
        merged = term if merged is None else merged + term
    out_ref[...] = x + _dot(merged.astype(BF16), wo_ref[...])


def _merge(x2, g, wg, branches, wb, wo, tm):
    t, d = x2.shape

    def tok(w):
        return pl.BlockSpec((tm, w), lambda i: (i, 0))

    def full(a):
        return pl.BlockSpec(a.shape, lambda i: (0,) * a.ndim, pipeline_mode=pl.Buffered(1))

    return pl.pallas_call(
        _merge_kernel,
        grid=(t // tm,),
        in_specs=[tok(d), full(g), full(wg)] + [tok(BRANCH_WIDTH)] * N_BRANCH + [full(wb), full(wo)],
        out_specs=tok(d),
        out_shape=jax.ShapeDtypeStruct((t, d), F32),
        compiler_params=_cparams(("arbitrary",)),
        name="merge",
    )(x2, g, wg, *branches, wb, wo)


def _ffn_kernel(x_ref, g_ref, wg_ref, wu_ref, wd_ref, fg_ref, out_ref, *, fc, final):
    x = x_ref[...]
    h = _rms(x, g_ref[...]).astype(BF16)
    acc = x
    for c in range(D_FF // fc):
        gate = _dot(h, wg_ref[:, c * fc:(c + 1) * fc])
        up = _dot(h, wu_ref[:, c * fc:(c + 1) * fc])
        act = (gate / (1.0 + jnp.exp(-gate)) * up).astype(BF16)
        acc = acc + _dot(act, wd_ref[c * fc:(c + 1) * fc, :])
    if final:
        acc = _rms(acc, fg_ref[...])
    out_ref[...] = acc


def _ffn(x2, g, wg, wu, wd, fg, tm, final):
    t, d = x2.shape

    def tok(w):
        return pl.BlockSpec((tm, w), lambda i: (i, 0))

    def full(a):
        return pl.BlockSpec(a.shape, lambda i: (0,) * a.ndim, pipeline_mode=pl.Buffered(1))

    return pl.pallas_call(
        functools.partial(_ffn_kernel, fc=MXU_WIDTH, final=final),
        grid=(t // tm,),
        in_specs=[tok(d), full(g), full(wg), full(wu), full(wd), full(fg)],
        out_specs=tok(d),
        out_shape=jax.ShapeDtypeStruct((t, d), F32),
        compiler_params=_cparams(("arbitrary",)),
        name="ffn",
    )(x2, g, wg, wu, wd, fg)


def _tile(n, pref):
    while n % pref:
        pref //= 2
    return pref


def kernel(x, positions, attn_norm_g, w_in, mla_q_norm_g, mla_w_q_up, mla_kv_norm_g, mla_w_kv_up, conv_w, gla_w_gate_up, gla_b_gate, gla_norm_g, w_branch, w_out, ffn_norm_g, w_ffn_gate, w_ffn_up, w_ffn_down, final_norm_g):
    b, s, d = x.shape
    depth = w_in.shape[0]
    t = b * s
    assert d == D_MODEL and s % GLA_CHUNK == 0 and conv_w.shape[1] == CONV_K and w_in.shape[2] == O_END
    tm_prep = _tile(s, 512)
    tm_wide = _tile(t, 1024)
    tq_mla = _tile(s, 512)
    tk_att = _tile(s, 512)
    tq_dsa = _tile(s, 512)
    th_dsa = _tile(tq_dsa, 256)
    tg = _tile(s, 1024)

    x2 = x.reshape(t, d)
    ctab, stab = _rope_tables(positions, _tile(t, 2048))
    row = lambda a: a.reshape(1, -1).astype(F32)

    w_small = _pack_small_weight(w_in)
    wq, wqs, wk, wv = _pack_mla_weights(mla_w_q_up, mla_w_kv_up)
    wgate = jnp.concatenate([gla_w_gate_up, jnp.zeros((depth, LANE - GLA_GATE_RANK, GLA_HEADS * GLA_DK), F32)],
                            axis=1).astype(BF16)
    w_gates = w_in[:, :, O_GATE:].astype(BF16)
    w_branch_b, w_out_b = w_branch.astype(BF16), w_out.astype(BF16)
    w_gate_b, w_up_b = w_ffn_gate.astype(BF16), w_ffn_up.astype(BF16)
    w_down_b = w_ffn_down.astype(BF16)

    for l in range(depth):
        (mq, mk, mv, o_conv, dq, dkk, dvv, iq, ikr, iw, gq, gk, gv, gg, gr) = _prep(
            x2, b, row(attn_norm_g[l]), w_small[l], row(mla_q_norm_g[l]), wq[l], wqs[l], row(mla_kv_norm_g[l]),
            wk[l], wv[l], ctab, stab, conv_w[l].astype(F32), wgate[l], row(gla_b_gate[l]), tm_prep)
        o_mla = _mla(mq, mk, mv, b, tq_mla, tk_att)
        o_dsa = _dsa(iq, iw, ikr, dq, dkk, dvv, b, tq_dsa, th_dsa, tk_att)
        o_gla = _gla(gq, gk, gv, gg, gr, gla_norm_g[l], b, tg)
        x2 = _merge(x2, row(attn_norm_g[l]), w_gates[l], (o_mla, o_conv, o_dsa, o_gla), w_branch_b[l], w_out_b[l],
                    tm_wide)
        x2 = _ffn(x2, row(ffn_norm_g[l]), w_gate_b[l], w_up_b[l], w_down_b[l], row(final_norm_g), tm_wide,
                  final=(l == depth - 1))
    return x2.reshape(b, s, d)
```

```python
import functools

import jax
import jax.numpy as jnp
import numpy as np
from jax import lax
from jax.experimental import pallas as pl
from jax.experimental.pallas import tpu as pltpu

F32 = jnp.float32
BF16 = jnp.bfloat16
I32 = jnp.int32

D_MODEL = 1024
MLA_HEADS = 4
MLA_Q_RANK = 256
MLA_KV_RANK = 128
MLA_NOPE = 64
MLA_ROPE = 32
MLA_V = 64
ROPE_THETA = 10000.0
CONV_WIDTH = 256
CONV_K = 3
DSA_HEADS = 4
DSA_HEAD_DIM = 64
IDX_HEADS = 8
IDX_DIM = 32
DSA_TOPK_MAX = 256
GLA_HEADS = 4
GLA_DK = 32
GLA_DV = 64
GLA_GATE_RANK = 16
GLA_TAU = 16.0
GLA_CHUNK = 64
N_BRANCH = 4
BRANCH_WIDTH = 256
D_FF = ((8 * D_MODEL + 3 * 256 - 1) // (3 * 256)) * 256
EPS = 1e-6
NEG = -1e30
LOG2E = 1.4426950408889634

LANE = 128
MXU_WIDTH = 256
HEAD_PAD = 128
W_MLA = MLA_HEADS * HEAD_PAD
W_DQ = DSA_HEADS * DSA_HEAD_DIM
W_IQ = IDX_HEADS * IDX_DIM
W_GV = GLA_HEADS * GLA_DV

IN_SIZES = (
    MLA_Q_RANK, MLA_KV_RANK, MLA_ROPE,
    CONV_WIDTH, CONV_WIDTH, CONV_WIDTH,
    DSA_HEADS * DSA_HEAD_DIM, DSA_HEAD_DIM, DSA_HEAD_DIM,
    IDX_HEADS * IDX_DIM, IDX_DIM, IDX_HEADS,
    GLA_HEADS * GLA_DK, GLA_HEADS * GLA_DK, GLA_HEADS * GLA_DV,
    GLA_GATE_RANK, GLA_HEADS * GLA_DV,
    N_BRANCH * D_MODEL,
)
_OFF = np.concatenate([[0], np.cumsum(IN_SIZES)]).astype(int)
(O_CQ, O_CKV, O_KR, O_CB, O_CC, O_CX, O_DQ, O_DK, O_DV, O_IQ, O_IK, O_IW,
 O_GQ, O_GK, O_GV, O_GLR, O_GR, O_GATE, O_END) = [int(v) for v in _OFF]

P_CQ, P_CKV, P_KRP, P_KRS = 0, 256, 384, 512
P_CB, P_CC, P_CX = 640, 896, 1152
P_DQ, P_DKK, P_DVV = 1408, 1664, 1792
P_IQ, P_IKR, P_IW = 1920, 2176, 2304
P_GQ, P_GK, P_GV, P_GLR, P_GR = 2432, 2560, 2688, 2944, 3072
P_TOTAL = 3328

VMEM_LIMIT = 56 * 1024 * 1024


def _cparams(sem):
    return pltpu.CompilerParams(dimension_semantics=sem, vmem_limit_bytes=VMEM_LIMIT)


def _rms(x, g):
    return x * lax.rsqrt(jnp.mean(x * x, axis=-1, keepdims=True) + EPS) * g


def _dot(a, b):
    return jnp.dot(a, b, preferred_element_type=F32)


def _dot_nt(a, b):
    return lax.dot_general(a, b, (((1,), (1,)), ((), ())), preferred_element_type=F32)


def _rope_kernel(pos_ref, pat_ref, c_ref, s_ref):
    pos = pos_ref[...].astype(F32)
    ang = pos * pat_ref[0:1, :]
    c_ref[...] = pat_ref[1:2, :] + pat_ref[2:3, :] * jnp.cos(ang)
    s_ref[...] = pat_ref[3:4, :] * jnp.sin(ang)


def _rope_tables(positions, tm):
    t = positions.size
    half = MLA_ROPE // 2
    inv_freq = ROPE_THETA ** (-jnp.arange(half, dtype=F32) / half)
    z64, z32 = jnp.zeros((MLA_NOPE,), F32), jnp.zeros((HEAD_PAD - MLA_NOPE - MLA_ROPE,), F32)
    o16 = jnp.ones((half,), F32)
    pat = jnp.stack([
        jnp.concatenate([z64, inv_freq, inv_freq, z32]),
        jnp.concatenate([z64 + 1.0, 0 * o16, 0 * o16, z32]),
        jnp.concatenate([z64, o16, o16, z32]),
        jnp.concatenate([z64, -o16, o16, z32]),
    ])
    pat = jnp.concatenate([pat, jnp.zeros((4, HEAD_PAD), F32)], axis=0)
    return pl.pallas_call(
        _rope_kernel,
        grid=(t // tm,),
        in_specs=[pl.BlockSpec((tm, 1), lambda i: (i, 0)), pl.BlockSpec((8, HEAD_PAD), lambda i: (0, 0))],
        out_specs=[pl.BlockSpec((tm, HEAD_PAD), lambda i: (i, 0))] * 2,
        out_shape=[jax.ShapeDtypeStruct((t, HEAD_PAD), F32)] * 2,
        compiler_params=_cparams(("arbitrary",)),
        name="rope_tables",
    )(positions.reshape(t, 1), pat)


def _prep_kernel(x_ref, g_ref, w_ref, qg_ref, wq_ref, wqs_ref, kvg_ref, wk_ref, wv_ref, c_ref, s_ref,
                 convw_ref, wgate_ref, bgate_ref,
                 mq_ref, mk_ref, mv_ref, oconv_ref, dq_ref, dkk_ref, dvv_ref, iq_ref, ikr_ref, iw_ref,
                 gq_ref, gk_ref, gv_ref, gg_ref, gr_ref, u_scr, *, tm):
    j = pl.program_id(1)
    h = _rms(x_ref[...], g_ref[...]).astype(BF16)
    p = _dot(h, w_ref[...])

    c1, s1 = c_ref[...], s_ref[...]
    c4 = jnp.concatenate([c1] * MLA_HEADS, axis=1)
    s4 = jnp.concatenate([s1] * MLA_HEADS, axis=1)

    qn = _rms(p[:, P_CQ:P_CQ + MLA_Q_RANK], qg_ref[...]).astype(BF16)
    q = _dot(qn, wq_ref[...]) * c4 + _dot(qn, wqs_ref[...]) * s4
    mq_ref[...] = (q * ((MLA_NOPE + MLA_ROPE) ** -0.5 * LOG2E)).astype(BF16)
    kvn = _rms(p[:, P_CKV:P_CKV + MLA_KV_RANK], kvg_ref[...]).astype(BF16)
    kr = p[:, P_KRP:P_KRP + HEAD_PAD] * c1 + p[:, P_KRS:P_KRS + HEAD_PAD] * s1
    mk_ref[...] = (_dot(kvn, wk_ref[...]) + jnp.concatenate([kr] * MLA_HEADS, axis=1)).astype(BF16)
    vv = _dot(kvn, wv_ref[...])
    ones = jnp.ones((tm, LANE), F32)
    mv_ref[...] = jnp.concatenate([vv[:, 0:LANE], ones, vv[:, LANE:2 * LANE], ones], axis=1).astype(BF16)

    u = p[:, P_CC:P_CC + CONV_WIDTH] * p[:, P_CX:P_CX + CONV_WIDTH]

    @pl.when(j == 0)
    def _():
        u_scr[0:8, :] = jnp.zeros((8, CONV_WIDTH), F32)

    u_scr[8:8 + tm, :] = u
    cw = convw_ref[...]
    y = cw[0:1, :] * u_scr[6:6 + tm, :] + cw[1:2, :] * u_scr[7:7 + tm, :] + cw[2:3, :] * u
    oconv_ref[...] = (p[:, P_CB:P_CB + CONV_WIDTH] * y).astype(BF16)
    u_scr[0:8, :] = u_scr[tm:tm + 8, :]

    dq_ref[...] = (p[:, P_DQ:P_DQ + W_DQ] * (DSA_HEAD_DIM ** -0.5 * LOG2E)).astype(BF16)
    dkk_ref[...] = p[:, P_DKK:P_DKK + LANE].astype(BF16)
    hi_lanes = lax.broadcasted_iota(I32, (tm, LANE), 1) >= DSA_HEAD_DIM
    dvv_ref[...] = jnp.where(hi_lanes, 1.0, p[:, P_DVV:P_DVV + LANE]).astype(BF16)
    iq_ref[...] = p[:, P_IQ:P_IQ + W_IQ].astype(BF16)
    ikr_ref[...] = p[:, P_IKR:P_IKR + LANE].astype(BF16)
    iw_ref[...] = p[:, P_IW:P_IW + LANE] * ((IDX_DIM ** -0.5) * (IDX_HEADS ** -0.5))

    gq_ref[...] = p[:, P_GQ:P_GQ + LANE] * (GLA_DK ** -0.5)
    gk_ref[...] = p[:, P_GK:P_GK + LANE]
    gv_ref[...] = p[:, P_GV:P_GV + W_GV]
    gate = _dot(p[:, P_GLR:P_GLR + LANE].astype(BF16), wgate_ref[...]) + bgate_ref[...]
    log_sig = jnp.minimum(gate, 0.0) - jnp.log1p(jnp.exp(-jnp.abs(gate)))
    gg_ref[...] = log_sig * (1.0 / GLA_TAU)
    gr = p[:, P_GR:P_GR + W_GV]
    gr_ref[...] = gr / (1.0 + jnp.exp(-gr))


def _pack_small_weight(w_in):
    def cols(a, b):
        return w_in[..., a:b]

    def zeros(n):
        return jnp.zeros(w_in.shape[:-1] + (n,), w_in.dtype)

    half = MLA_ROPE // 2
    kr1, kr2 = cols(O_KR, O_KR + half), cols(O_KR + half, O_KR + MLA_ROPE)
    pad_tail = HEAD_PAD - MLA_NOPE - MLA_ROPE
    pieces = [
        cols(O_CQ, O_CKV), cols(O_CKV, O_KR),
        zeros(MLA_NOPE), kr1, kr2, zeros(pad_tail),
        zeros(MLA_NOPE), kr2, kr1, zeros(pad_tail),
        cols(O_CB, O_CC), cols(O_CC, O_CX), cols(O_CX, O_DQ),
        cols(O_DQ, O_DK), cols(O_DK, O_DV), cols(O_DK, O_DV), cols(O_DV, O_IQ), zeros(DSA_HEAD_DIM),
        cols(O_IQ, O_IK), cols(O_IK, O_IW), cols(O_IK, O_IW), cols(O_IK, O_IW), cols(O_IK, O_IW),
        cols(O_IW, O_GQ), zeros(LANE - IDX_HEADS),
        cols(O_GQ, O_GK), cols(O_GK, O_GV), cols(O_GV, O_GLR),
        cols(O_GLR, O_GR), zeros(LANE - GLA_GATE_RANK),
        cols(O_GR, O_GATE),
    ]
    out = jnp.concatenate(pieces, axis=-1)
    assert out.shape[-1] == P_TOTAL, out.shape
    return out.astype(BF16)


def _pack_mla_weights(w_q_up, w_kv_up):
    half = MLA_ROPE // 2
    lq = w_q_up.shape[:-1]
    lk = w_kv_up.shape[:-1]
    wq = w_q_up.reshape(lq + (MLA_HEADS, MLA_NOPE + MLA_ROPE))
    nope, x1, x2 = wq[..., :MLA_NOPE], wq[..., MLA_NOPE:MLA_NOPE + half], wq[..., MLA_NOPE + half:]
    zq = jnp.zeros(lq + (MLA_HEADS, HEAD_PAD - MLA_NOPE - MLA_ROPE), w_q_up.dtype)
    wq_pad = jnp.concatenate([nope, x1, x2, zq], axis=-1).reshape(lq + (MLA_HEADS * HEAD_PAD,))
    wq_swap = jnp.concatenate([jnp.zeros_like(nope), x2, x1, zq], axis=-1).reshape(lq + (MLA_HEADS * HEAD_PAD,))
    wkv = w_kv_up.reshape(lk + (MLA_HEADS, MLA_NOPE + MLA_V))
    zk = jnp.zeros(lk + (MLA_HEADS, HEAD_PAD - MLA_NOPE), w_kv_up.dtype)
    wk_pad = jnp.concatenate([wkv[..., :MLA_NOPE], zk], axis=-1).reshape(lk + (MLA_HEADS * HEAD_PAD,))
    wv = wkv[..., MLA_NOPE:].reshape(lk + (MLA_HEADS * MLA_V,))
    return wq_pad.astype(BF16), wq_swap.astype(BF16), wk_pad.astype(BF16), wv.astype(BF16)


def _prep(x2, nb, g, w_small, qg, wq, wqs, kvg, wk, wv, ctab, stab, conv_w, wgate, bgate, tm):
    t, d = x2.shape
    nj = t // nb // tm

    def tok(width):
        return pl.BlockSpec((tm, width), lambda b, j: (b * nj + j, 0))

    def full(a):
        return pl.BlockSpec(a.shape, lambda b, j: (0,) * a.ndim, pipeline_mode=pl.Buffered(1))

    outs = [(W_MLA, BF16), (W_MLA, BF16), (W_MLA, BF16), (CONV_WIDTH, BF16), (W_DQ, BF16), (LANE, BF16), (LANE, BF16),
            (W_IQ, BF16), (LANE, BF16), (LANE, F32), (LANE, F32), (LANE, F32), (W_GV, F32), (LANE, F32), (W_GV, F32)]
    return pl.pallas_call(
        functools.partial(_prep_kernel, tm=tm),
        grid=(nb, nj),
        in_specs=[tok(d), full(g), full(w_small), full(qg), full(wq), full(wqs), full(kvg), full(wk), full(wv),
                  tok(HEAD_PAD), tok(HEAD_PAD), full(conv_w), full(wgate), full(bgate)],
        out_specs=[tok(w) for w, _ in outs],
        out_shape=[jax.ShapeDtypeStruct((t, w), dt) for w, dt in outs],
        scratch_shapes=[pltpu.VMEM((tm + 8, CONV_WIDTH), F32)],
        compiler_params=_cparams(("arbitrary", "arbitrary")),
        name="prep",
    )(x2, g, w_small, qg, wq, wqs, kvg, wk, wv, ctab, stab, conv_w, wgate, bgate)


def _for_chunks(n, body):
    def group(i, carry):
        for u in range(4):
            body(4 * i + u, carry)
        return carry

    lax.fori_loop(0, n // 4, group, 0)
    rem = n % 4

    @pl.when(rem >= 2)
    def _():
        body(n - rem, 0)
        body(n - rem + 1, 0)

    @pl.when(rem % 2 == 1)
    def _():
        body(n - 1, 0)


def _fold_lanes(x, op):
    out = x[:, 0:LANE]
    for g in range(1, x.shape[1] // LANE):
        out = op(out, x[:, g * LANE:(g + 1) * LANE])
    return out


def _mla_kernel(q_ref, k_ref, v_ref, o_ref, s_scr, m_scr, acc_scr, *, tq, tk):
    i = pl.program_id(1)
    n_chunks = ((i + 1) * tq + tk - 1) // tk
    row = i * tq + lax.broadcasted_iota(I32, (tq, tk), 0)
    col = lax.broadcasted_iota(I32, (tq, tk), 1)
    lane = lax.broadcasted_iota(I32, (tq, LANE), 1)
    heads = range(MLA_HEADS)
    hs = [slice(h * HEAD_PAD, (h + 1) * HEAD_PAD) for h in heads]
    vs = [slice((h // 2) * 2 * LANE, (h // 2 + 1) * 2 * LANE) for h in heads]

    m_scr[...] = jnp.full(m_scr.shape, NEG, F32)
    acc_scr[...] = jnp.zeros(acc_scr.shape, F32)

    def phase1(c, carry):
        off = pl.multiple_of(c * tk, tk)
        causal = c * tk + col <= row
        for h in heads:
            s = jnp.where(causal, _dot_nt(q_ref[:, hs[h]], k_ref[pl.ds(off, tk), hs[h]]), NEG)
            s_scr[h, c] = s
            m_scr[h] = jnp.maximum(m_scr[h], _fold_lanes(s, jnp.maximum))
        return carry

    _for_chunks(n_chunks, phase1)
    for h in heads:
        m_scr[h] = jnp.broadcast_to(jnp.max(m_scr[h], axis=1, keepdims=True), (tq, LANE))

    def phase2(c, carry):
        off = pl.multiple_of(c * tk, tk)
        for pair in range(MLA_HEADS // 2):
            ps = [jnp.exp2(s_scr[h, c] - jnp.concatenate([m_scr[h]] * (tk // LANE), axis=1)).astype(BF16)
                  for h in (2 * pair, 2 * pair + 1)]
            pv = _dot(jnp.concatenate(ps, axis=0), v_ref[pl.ds(off, tk), vs[2 * pair]])
            acc_scr[2 * pair] = acc_scr[2 * pair] + pv[0:tq]
            acc_scr[2 * pair + 1] = acc_scr[2 * pair + 1] + pv[tq:2 * tq]
        return carry

    _for_chunks(n_chunks, phase2)
    res = [acc_scr[h, :, 0:LANE] / acc_scr[h, :, LANE:2 * LANE] for h in heads]
    for pair in range(MLA_HEADS // 2):
        o_ref[:, pair * LANE:(pair + 1) * LANE] = jnp.where(lane < MLA_V, res[2 * pair], res[2 * pair + 1]).astype(BF16)


def _mla(mq, mk, mv, nb, tq, tk):
    t = mq.shape[0]
    s = t // nb
    nq = s // tq
    return pl.pallas_call(
        functools.partial(_mla_kernel, tq=tq, tk=tk),
        grid=(nb, nq),
        in_specs=[pl.BlockSpec((tq, MLA_HEADS * HEAD_PAD), lambda b, i: (b * nq + i, 0)),
                  pl.BlockSpec((s, MLA_HEADS * HEAD_PAD), lambda b, i: (b, 0)),
                  pl.BlockSpec((s, MLA_HEADS * HEAD_PAD), lambda b, i: (b, 0))],
        out_specs=pl.BlockSpec((tq, MLA_HEADS * MLA_V), lambda b, i: (b * nq + i, 0)),
        out_shape=jax.ShapeDtypeStruct((t, MLA_HEADS * MLA_V), BF16),
        scratch_shapes=[pltpu.VMEM((MLA_HEADS, s // tk, tq, tk), F32),
                        pltpu.VMEM((MLA_HEADS, tq, LANE), F32),
                        pltpu.VMEM((MLA_HEADS, tq, 2 * LANE), F32)],
        compiler_params=_cparams(("arbitrary", "arbitrary")),
        name="mla_attn",
    )(mq, mk, mv)


def _dsa_kernel(iq_ref, iw_ref, ikr_ref, dq_ref, dkk_ref, dvv_ref, tri_ref, o_ref,
                key_scr, plane_scr, eq_scr, gt_scr, thr_scr, s_scr, m_scr, acc_scr,
                *, tq, th, tk, topk):
    i = pl.program_id(1)
    q_end = (i + 1) * tq
    n_chunks = (q_end + tk - 1) // tk
    lane_q = lax.broadcasted_iota(I32, (th, LANE), 1)
    int_min = jnp.int32(-2 ** 31)
    rb = 64
    lane_rb = lax.broadcasted_iota(I32, (rb, tk), 1)
    row_rb = lax.broadcasted_iota(I32, (rb, 1), 0)

    for r0 in range(0, tq, th):
        iq = iq_ref[r0:r0 + th, :].astype(F32)
        iw = iw_ref[r0:r0 + th, :]
        q_heads = []
        w_heads = []
        for h in range(IDX_HEADS):
            grp = h // 4
            sub = iq[:, grp * LANE:(grp + 1) * LANE]
            lo = (h % 4) * IDX_DIM
            in_head = (lane_q >= lo) & (lane_q < lo + IDX_DIM)
            q_heads.append(jnp.where(in_head, sub, 0.0).astype(BF16))
            w_heads.append(iw[:, h:h + 1])
        q_stacks = [jnp.concatenate([qh[r:r + rb] for qh in q_heads], axis=0) for r in range(0, th, rb)]

        def score_chunk(c, _, r0=r0, q_stacks=q_stacks, w_heads=w_heads):
            off = pl.multiple_of(c * tk, tk)
            kc = ikr_ref[pl.ds(off, tk), :]
            for bi, r in enumerate(range(0, th, rb)):
                z = _dot_nt(q_stacks[bi], kc)
                acc = w_heads[0][r:r + rb] * jnp.maximum(z[0:rb], 0.0)
                for h in range(1, IDX_HEADS):
                    acc = acc + w_heads[h][r:r + rb] * jnp.maximum(z[h * rb:(h + 1) * rb], 0.0)
                bits = pltpu.bitcast(acc, I32)
                key = jnp.where(bits < 0, bits ^ jnp.int32(0x7FFFFFFF), bits)
                key = jnp.where(c * tk + lane_rb <= i * tq + r0 + r + row_rb, key, int_min)
                key_scr[c, r0 + r:r0 + r + rb, :] = key
            return 0

        _for_chunks(n_chunks, score_chunk)

    kf = jnp.float32(topk)
    n_groups = key_scr.shape[0] * (tk // LANE)
    cb = 128

    def row_total(cnt):
        return _dot(cnt.astype(F32).astype(BF16), jnp.ones((LANE, LANE), BF16))

    def select():
        def fill(c, _):
            key_scr[c] = jnp.full((tq, tk), int_min, I32)
            return 0
        few_groups = n_chunks * (tk // LANE) <= 16
        lax.fori_loop(n_chunks, jnp.where(few_groups, min(16 // (tk // LANE), key_scr.shape[0]), key_scr.shape[0]),
                      fill, 0)

        stage_mask = {16: 0x0000FFFF, 8: 0x00FF00FF, 4: 0x0F0F0F0F, 2: 0x33333333, 1: 0x55555555}

        def build_planes(blk, _, nw):
            r = pl.multiple_of(blk * 8, 8)
            w = []
            for g in range(nw):
                if g < n_groups:
                    gc, gl = divmod(g, tk // LANE)
                    w.append(key_scr[gc, pl.ds(r, 8), gl * LANE:(gl + 1) * LANE] ^ int_min)
                else:
                    w.append(jnp.zeros((8, LANE), I32))
            j = nw // 2
            while j:
                k = 0
                while k < nw:
                    t = (w[k] ^ lax.shift_right_logical(w[k + j], jnp.int32(j))) & jnp.int32(stage_mask[j])
                    w[k] = w[k] ^ t
                    w[k + j] = w[k + j] ^ lax.shift_left(t, jnp.int32(j))
                    k = (k + j + 1) & ~j
                j >>= 1
            if nw == 16:
                w = [lax.shift_right_logical(u, jnp.int32(16)) for u in w] + [u & jnp.int32(0xFFFF) for u in w]
            for b in range(32):
                plane_scr[b, pl.ds(r, 8), :] = w[b]
            return 0

        @pl.when(few_groups)
        def _():
            lax.fori_loop(0, tq // 8, functools.partial(build_planes, nw=16), 0)

        if n_groups > 16:
            @pl.when(jnp.logical_not(few_groups))
            def _():
                lax.fori_loop(0, tq // 8, functools.partial(build_planes, nw=32), 0)

        eq_scr[...] = jnp.full((tq, LANE), -1, I32)
        gt_scr[...] = jnp.zeros((tq, LANE), I32)
        thr_scr[...] = jnp.zeros((tq, LANE), I32)

        def bit_pass(it, _):
            bit = lax.shift_left(jnp.int32(1), 31 - it)
            for r in range(0, tq, cb):
                rows = slice(r, r + cb)
                eq, gtm = eq_scr[rows, :], gt_scr[rows, :]
                t = eq & plane_scr[it, rows, :]
                u = gtm | t
                take = jnp.sum(lax.population_count(u).astype(F32), axis=1, keepdims=True) >= kf
                eq_scr[rows, :] = jnp.where(take, t, eq ^ t)
                gt_scr[rows, :] = jnp.where(take, gtm, u)
                thr_scr[rows, :] = jnp.where(take, thr_scr[rows, :] | bit, thr_scr[rows, :])
            return 0
        lax.fori_loop(0, 32, bit_pass, 0)

        thr_u = thr_scr[:, 0:1]
        thr = jnp.maximum(thr_u ^ int_min, int_min + 1)
        c_gt = row_total(lax.population_count(gt_scr[...]))[:, 0:1]
        c_ge = c_gt + row_total(lax.population_count(eq_scr[...]))[:, 0:1]
        excess = (c_ge > kf) & (thr_u != 0)

        need = kf - c_gt

        def tie_break(r):
            rows = slice(r, r + cb)
            thr_b, need_b = thr[rows], need[rows]

            def demote(c, seen):
                kk = key_scr[c, rows, :]
                tie = kk == thr_b
                rank = seen + _dot(jnp.where(tie, 1.0, 0.0).astype(BF16), tri_ref[...])
                key_scr[c, rows, :] = jnp.where(tie & (rank > need_b), thr_b - 1, kk)
                return rank[:, tk - 1:tk]
            lax.fori_loop(0, n_chunks, demote, jnp.zeros((cb, 1), F32))

        for r in range(0, tq, cb):
            pl.when(jnp.max(jnp.where(excess[r:r + cb], 1.0, 0.0)) > 0.0)(functools.partial(tie_break, r))
        return thr

    thr = lax.cond(q_end > topk, select, lambda: jnp.full((tq, 1), int_min + 1, I32))

    heads = range(DSA_HEADS)
    for r0 in range(0, tq, th):
        rows = slice(r0, r0 + th)
        thr_h = thr[rows]
        dq = dq_ref[rows, :].astype(F32)
        qm = []
        for h in heads:
            sub = dq[:, (h // 2) * LANE:(h // 2 + 1) * LANE]
            lo = (h % 2) * DSA_HEAD_DIM
            in_head = (lane_q >= lo) & (lane_q < lo + DSA_HEAD_DIM)
            qm.append(jnp.where(in_head, sub, 0.0).astype(BF16))
        qm_stack = jnp.concatenate(qm, axis=0)

        m_scr[...] = jnp.full(m_scr.shape, NEG, F32)
        acc_scr[...] = jnp.zeros(acc_scr.shape, F32)

        def phase1(c, carry, rows=rows, thr_h=thr_h, qm_stack=qm_stack):
            off = pl.multiple_of(c * tk, tk)
            sel = key_scr[c, rows, :] >= thr_h
            s = _dot_nt(qm_stack, dkk_ref[pl.ds(off, tk), :])
            for h in heads:
                sm = jnp.where(sel, s[h * th:(h + 1) * th], NEG)
                s_scr[h, c] = sm
                m_scr[h] = jnp.maximum(m_scr[h], _fold_lanes(sm, jnp.maximum))
            return carry

        _for_chunks(n_chunks, phase1)
        for h in heads:
            m_scr[h] = jnp.broadcast_to(jnp.max(m_scr[h], axis=1, keepdims=True), (th, LANE))

        def phase2(c, carry):
            off = pl.multiple_of(c * tk, tk)
            ps = []
            for h in heads:
                p = jnp.exp2(s_scr[h, c] - jnp.concatenate([m_scr[h]] * (tk // LANE), axis=1))
                ps.append(p.astype(BF16))
            pv = _dot(jnp.concatenate(ps, axis=0), dvv_ref[pl.ds(off, tk), :])
            for h in heads:
                acc_scr[h] = acc_scr[h] + pv[h * th:(h + 1) * th]
            return carry

        _for_chunks(n_chunks, phase2)
        outs = [acc_scr[h] / pltpu.roll(acc_scr[h], DSA_HEAD_DIM, 1) for h in heads]
        lo_half = lane_q < DSA_HEAD_DIM
        o_ref[rows, :] = jnp.concatenate(
            [jnp.where(lo_half, outs[0], pltpu.roll(outs[1], DSA_HEAD_DIM, 1)),
             jnp.where(lo_half, outs[2], pltpu.roll(outs[3], DSA_HEAD_DIM, 1))], axis=1).astype(BF16)


def _dsa(iq, iw, ikr, dq, dkk, dvv, nb, tq, th, tk):
    t = iq.shape[0]
    s = t // nb
    nq = s // tq
    topk = min(DSA_TOPK_MAX, s // 4)
    assert s // LANE <= 32, "the bit-sliced selection packs one key per 128-lane group into a 32-bit word"
    tri = jnp.asarray(np.triu(np.ones((tk, tk), np.float32)), BF16)

    def qspec(w):
        return pl.BlockSpec((tq, w), lambda b, i: (b * nq + i, 0))

    def kspec(w):
        return pl.BlockSpec((s, w), lambda b, i: (b, 0))

    return pl.pallas_call(
        functools.partial(_dsa_kernel, tq=tq, th=th, tk=tk, topk=topk),
        grid=(nb, nq),
        in_specs=[qspec(W_IQ), qspec(LANE), kspec(LANE), qspec(W_DQ), kspec(LANE), kspec(LANE),
                  pl.BlockSpec((tk, tk), lambda b, i: (0, 0), pipeline_mode=pl.Buffered(1))],
        out_specs=qspec(W_DQ),
        out_shape=jax.ShapeDtypeStruct((t, DSA_HEADS * DSA_HEAD_DIM), BF16),
        scratch_shapes=[pltpu.VMEM((s // tk, tq, tk), I32),
                        pltpu.VMEM((32, tq, LANE), I32),
                        pltpu.VMEM((tq, LANE), I32),
                        pltpu.VMEM((tq, LANE), I32),
                        pltpu.VMEM((tq, LANE), I32),
                        pltpu.VMEM((DSA_HEADS, s // tk, th, tk), F32)] + [pltpu.VMEM((DSA_HEADS, th, LANE), F32)] * 2,
        compiler_params=_cparams(("arbitrary", "arbitrary")),
        name="dsa",
    )(iq, iw, ikr, dq, dkk, dvv, tri)


def _gla_kernel(gq_ref, gk_ref, gv_ref, gg_ref, gr_ref, ng_ref, segv_ref, seg64_ref, bmask_ref, o_ref,
                st_scr, kbuf, cbuf, vbuf, *, tg):
    j = pl.program_id(1)
    ch = GLA_CHUNK
    rowc = lax.broadcasted_iota(I32, (tg, 1), 0) % ch

    @pl.when(j == 0)
    def _():
        st_scr[...] = jnp.zeros_like(st_scr)
        kbuf[0:ch, :] = jnp.zeros((ch, LANE), F32)
        cbuf[0:ch, :] = jnp.zeros((ch, LANE), F32)
        vbuf[0:ch, :] = jnp.zeros((ch, 2 * LANE), F32)

    cbuf[ch:ch + tg, :] = gg_ref[...]
    sh = 1
    while sh < ch:
        cur = cbuf[ch:ch + tg, :]
        prev = cbuf[ch - sh:ch - sh + tg, :]
        cbuf[ch:ch + tg, :] = cur + jnp.where(rowc >= sh, prev, 0.0)
        sh *= 2
    cum = cbuf[ch:ch + tg, :]
    q = gq_ref[...]
    k = gk_ref[...]
    v = gv_ref[...]
    kbuf[ch:ch + tg, :] = k
    vbuf[ch:ch + tg, :] = v

    qd = (q * jnp.exp(cum)).astype(BF16)
    bmask = bmask_ref[...]
    inter = []
    for c in range(tg // ch):
        rs = slice(c * ch, (c + 1) * ch)
        st = st_scr[...]
        inter.append(_dot_nt(qd[rs], st.astype(BF16)))
        last = cum[c * ch + ch - 1:c * ch + ch, :]
        kp = k[rs] * jnp.exp(last - cum[rs])
        upd = lax.dot_general(v[rs].astype(BF16), kp.astype(BF16), (((0,), (0,)), ((), ())),
                              preferred_element_type=F32)
        st_scr[...] = jnp.exp(last) * st + upd * bmask
    o = jnp.concatenate(inter, axis=0)

    segv = segv_ref[...]
    cum2 = cum * LOG2E
    cbuf[ch:ch + tg, :] = cum2
    sub = 8
    rows_sub = rowc % sub
    for d in range(sub):
        kd = kbuf[ch - d:ch - d + tg, :]
        cd = cbuf[ch - d:ch - d + tg, :]
        vd = vbuf[ch - d:ch - d + tg, :]
        e = jnp.exp2(jnp.where(rows_sub >= d, cum2 - cd, NEG))
        o = o + _dot((q * kd * e).astype(BF16), segv) * vd

    lane_k3 = lax.broadcasted_iota(I32, (1, (ch // sub - 1) * LANE), 1) % LANE // GLA_DK
    lane_v = lax.broadcasted_iota(I32, (1, 2 * LANE), 1) // GLA_DV
    far = []
    for c in range(tg // ch):
        rs = slice(c * ch, (c + 1) * ch)
        qc, kc, cc, sb = q[rs], k[rs], cum2[rs], rowc[rs] // sub
        qh, kh = [], []
        for blk in range(1, ch // sub):
            ref_row = cc[blk * sub - 1:blk * sub, :]
            qh.append(qc * jnp.exp2(jnp.where(sb == blk, cc - ref_row, NEG)))
            kh.append(kc * jnp.exp2(jnp.where(sb < blk, ref_row - cc, NEG)))
        qh = jnp.concatenate(qh, axis=1)
        kh = jnp.concatenate(kh, axis=1).astype(BF16)
        q_heads = jnp.concatenate([jnp.where(lane_k3 == h, qh, 0.0) for h in range(GLA_HEADS)], axis=0)
        att = _dot_nt(q_heads.astype(BF16), kh)
        ov = _dot(att.astype(BF16), v[rs].astype(BF16))
        oc = jnp.where(lane_v == 0, ov[0:ch], 0.0)
        for h in range(1, GLA_HEADS):
            oc = oc + jnp.where(lane_v == h, ov[h * ch:(h + 1) * ch], 0.0)
        far.append(oc)
    o = o + jnp.concatenate(far, axis=0)

    sq = o * o
    hi = sq.astype(BF16)
    lo = (sq - hi.astype(F32)).astype(BF16)
    var = _dot(hi, seg64_ref[...]) + _dot(lo, seg64_ref[...])
    o_ref[...] = (gr_ref[...] * (o * lax.rsqrt(var + EPS) * ng_ref[...])).astype(BF16)


def _gla(gq, gk, gv, gg, gr, norm_g, nb, tg):
    t = gq.shape[0]
    nj = t // nb // tg
    ng = jnp.tile(norm_g.reshape(1, GLA_DV), (1, GLA_HEADS)).astype(F32)
    hk = np.arange(GLA_HEADS * GLA_DK) // GLA_DK
    hv = np.arange(GLA_HEADS * GLA_DV) // GLA_DV
    same = (hk[:, None] == hv[None, :])
    segv = jnp.asarray(same, BF16)
    bmask = jnp.asarray(same.T, F32)
    seg64 = jnp.asarray((hv[:, None] == hv[None, :]) / GLA_DV, BF16)

    def tok(w):
        return pl.BlockSpec((tg, w), lambda b, j: (b * nj + j, 0))

    def full(a):
        return pl.BlockSpec(a.shape, lambda b, j: (0,) * a.ndim, pipeline_mode=pl.Buffered(1))

    return pl.pallas_call(
        functools.partial(_gla_kernel, tg=tg),
        grid=(nb, nj),
        in_specs=[tok(LANE), tok(LANE), tok(W_GV), tok(LANE), tok(W_GV), full(ng), full(segv), full(seg64),
                  full(bmask)],
        out_specs=tok(W_GV),
        out_shape=jax.ShapeDtypeStruct((t, GLA_HEADS * GLA_DV), BF16),
        scratch_shapes=[pltpu.VMEM((2 * LANE, LANE), F32),
                        pltpu.VMEM((GLA_CHUNK + tg, LANE), F32),
                        pltpu.VMEM((GLA_CHUNK + tg, LANE), F32),
                        pltpu.VMEM((GLA_CHUNK + tg, 2 * LANE), F32)],
        compiler_params=_cparams(("arbitrary", "arbitrary")),
        name="gla",
    )(gq, gk, gv, gg, gr, ng, segv, seg64, bmask)


def _merge_ffn_kernel(x_ref, g_ref, wgate_ref, o0_ref, o1_ref, o2_ref, o3_ref, wb_ref, wo_ref,
                      g2_ref, wg_ref, wu_ref, wd_ref, fg_ref, out_ref, *, fc, final):
    x = x_ref[...]
    h = _rms(x, g_ref[...]).astype(BF16)
    merged = None
    for n, o_ref in enumerate((o0_ref, o1_ref, o2_ref, o3_ref)):
        gate = _dot(h, wgate_ref[:, n * D_MODEL:(n + 1) * D_MODEL])
        gate = 1.0 / (1.0 + jnp.exp(-gate))
        term = gate * _dot(o_ref[...], wb_ref[n])
        merged = term if merged is None else merged + term
    x = x + _dot(merged.astype(BF16), wo_ref[...])

    h = _rms(x, g2_ref[...]).astype(BF16)
    acc = x
    for c in range(D_FF // fc):
        gate = _dot(h, wg_ref[:, c * fc:(c + 1) * fc])
        up = _dot(h, wu_ref[:, c * fc:(c + 1) * fc])
        act = (gate / (1.0 + jnp.exp(-gate)) * up).astype(BF16)
        acc = acc + _dot(act, wd_ref[c * fc:(c + 1) * fc, :])
    if final:
        acc = _rms(acc, fg_ref[...])
    out_ref[...] = acc


def _merge_ffn(x2, g, wgate, branches, wb, wo, g2, wg, wu, wd, fg, tm, final):
    t, d = x2.shape

    def tok(w):
        return pl.BlockSpec((tm, w), lambda i: (i, 0))

    def full(a):
        return pl.BlockSpec(a.shape, lambda i: (0,) * a.ndim, pipeline_mode=pl.Buffered(1))

    return pl.pallas_call(
        functools.partial(_merge_ffn_kernel, fc=MXU_WIDTH, final=final),
        grid=(t // tm,),
        in_specs=[tok(d), full(g), full(wgate)] + [tok(BRANCH_WIDTH)] * N_BRANCH
        + [full(wb), full(wo), full(g2), full(wg), full(wu), full(wd), full(fg)],
        out_specs=tok(d),
        out_shape=jax.ShapeDtypeStruct((t, d), F32),
        compiler_params=_cparams(("arbitrary",)),
        name="merge_ffn",
    )(x2, g, wgate, *branches, wb, wo, g2, wg, wu, wd, fg)


def _tile(n, pref):
    while n % pref:
        pref //= 2
    return pref


def kernel(x, positions, attn_norm_g, w_in, mla_q_norm_g, mla_w_q_up, mla_kv_norm_g, mla_w_kv_up, conv_w, gla_w_gate_up, gla_b_gate, gla_norm_g, w_branch, w_out, ffn_norm_g, w_ffn_gate, w_ffn_up, w_ffn_down, final_norm_g):
    b, s, d = x.shape
    depth = w_in.shape[0]
    t = b * s
    assert d == D_MODEL and s % GLA_CHUNK == 0 and conv_w.shape[1] == CONV_K and w_in.shape[2] == O_END
    tm_prep = _tile(s, 512)
    tm_wide = _tile(t, 512)
    tq_mla = _tile(s, 512)
    tk_att = _tile(s, 512)
    tq_dsa = _tile(s, 512)
    th_dsa = _tile(tq_dsa, 256)
    tg = _tile(s, 1024)

    x2 = x.reshape(t, d)
    ctab, stab = _rope_tables(positions, _tile(t, 2048))
    row = lambda a: a.reshape(1, -1).astype(F32)

    w_small = _pack_small_weight(w_in)
    wq, wqs, wk, wv = _pack_mla_weights(mla_w_q_up, mla_w_kv_up)
    wgate = jnp.concatenate([gla_w_gate_up, jnp.zeros((depth, LANE - GLA_GATE_RANK, GLA_HEADS * GLA_DK), F32)],
                            axis=1).astype(BF16)
    w_gates = w_in[:, :, O_GATE:].astype(BF16)
    w_branch_b, w_out_b = w_branch.astype(BF16), w_out.astype(BF16)
    w_gate_b, w_up_b = w_ffn_gate.astype(BF16), w_ffn_up.astype(BF16)
    w_down_b = w_ffn_down.astype(BF16)

    for l in range(depth):
        (mq, mk, mv, o_conv, dq, dkk, dvv, iq, ikr, iw, gq, gk, gv, gg, gr) = _prep(
            x2, b, row(attn_norm_g[l]), w_small[l], row(mla_q_norm_g[l]), wq[l], wqs[l], row(mla_kv_norm_g[l]),
            wk[l], wv[l], ctab, stab, conv_w[l].astype(F32), wgate[l], row(gla_b_gate[l]), tm_prep)
        o_mla = _mla(mq, mk, mv, b, tq_mla, tk_att)
        o_dsa = _dsa(iq, iw, ikr, dq, dkk, dvv, b, tq_dsa, th_dsa, tk_att)
        o_gla = _gla(gq, gk, gv, gg, gr, gla_norm_g[l], b, tg)
        x2 = _merge_ffn(x2, row(attn_norm_g[l]), w_gates[l], (o_mla, o_conv, o_dsa, o_gla), w_branch_b[l], w_out_b[l],
                        row(ffn_norm_g[l]), w_gate_b[l], w_up_b[l], w_down_b[l], row(final_norm_g), tm_wide,
                        final=(l == depth - 1))
    return x2.reshape(b, s, d)
```
